```python
import jax, jax.numpy as jnp
from jax import lax
import numpy as np

D_MODEL = 2048
BATCH = 4
SEQ = 2048
DEPTH = 2
DEC_BATCH = 128
DEC_SEQ = 8
PAST_LEN = 16384
PAGE_SIZE = 128

N_MIXERS = 2
N_A_LAYERS = (DEPTH + 1) // 2
N_B_LAYERS = DEPTH // 2
TOK_WIDTH = 3 * D_MODEL // 4
MEM_WIDTH = D_MODEL - TOK_WIDTH
MEM_HEADS = 4
MEM_HEAD_DIM = MEM_WIDTH // MEM_HEADS
N_MEM = 256
A_HEAD_DIM = 64
A_HEADS = TOK_WIDTH // A_HEAD_DIM
A_DECAY_RANK = 96
A_ICLR_RANK = 96
A_GATE_RANK = 256
A_SPLITS = (TOK_WIDTH, 2 * TOK_WIDTH, 3 * TOK_WIDTH, 3 * TOK_WIDTH + A_DECAY_RANK, 3 * TOK_WIDTH + A_DECAY_RANK + A_ICLR_RANK)
A_PROJ = 3 * TOK_WIDTH + A_DECAY_RANK + A_ICLR_RANK + A_GATE_RANK
B_EXPAND = 128
B_HEADS = TOK_WIDTH // B_EXPAND
B_HEAD_V = TOK_WIDTH // B_HEADS
B_PROJ = 4 * TOK_WIDTH
CHUNK = 64
D_FF = 5632
CONV_W = 3
RMS_EPS = 1e-6
GN_EPS = 64e-5

kernel_name = 'hybrid_rwkv7_hgrn2_memxattn_convffn_step'


def rmsnorm(x, g):
    xf = x.astype(jnp.float32)
    y = xf * lax.rsqrt(jnp.mean(xf * xf, axis=-1, keepdims=True) + RMS_EPS)
    return (y * g.astype(jnp.float32)).astype(x.dtype)


def memory_attend(q, mem_k, mem_v):
    s = jnp.einsum('bthd,bmhd->bhtm', q, mem_k).astype(jnp.float32) * (MEM_HEAD_DIM ** -0.5)
    p = jax.nn.softmax(s, axis=-1).astype(mem_v.dtype)
    o = jnp.einsum('bhtm,bmhd->bthd', p, mem_v)
    return o.reshape(q.shape[0], q.shape[1], MEM_WIDTH)


def rwkv7_mix(p, p_prev, s0, P, j):
    b, t, _ = p.shape
    f32 = jnp.float32
    xm = p + P['a_mu'][j] * (p_prev - p)
    r, k, v, xw, xa, xg = jnp.split(xm, A_SPLITS, axis=-1)
    w = -jax.nn.softplus(-(P['a_w0'][j] + jnp.tanh(xw) @ P['a_w2'][j])) - 0.5
    decay = jnp.exp(-jnp.exp(w.astype(f32)))
    a = jax.nn.sigmoid(P['a_a0'][j] + xa @ P['a_a2'][j])
    g = jax.nn.sigmoid(xg) @ P['a_g2'][j]
    heads = lambda z: z.astype(f32).reshape(b, t, A_HEADS, A_HEAD_DIM)
    kk = heads(k * P['a_k_k'][j])
    kk = kk * lax.rsqrt(jnp.maximum(jnp.sum(kk * kk, axis=-1, keepdims=True), 1e-24))
    k = k * (1.0 + (a - 1.0) * P['a_k_a'][j])
    r_h, k_h, v_h, a_h, d_h = heads(r), heads(k), heads(v), heads(a), heads(decay)

    def step(S, inp):
        r_t, k_t, v_t, kk_t, b_t, d_t = inp
        S = (S * d_t[:, :, None, :]
             - jnp.einsum('bhvk,bhk->bhv', S, kk_t)[..., None] * b_t[:, :, None, :]
             + v_t[..., None] * k_t[:, :, None, :])
        return S, jnp.einsum('bhvk,bhk->bhv', S, r_t)

    tm = lambda z: jnp.swapaxes(z, 0, 1)
    s_fin, y = lax.scan(step, s0.astype(f32), (tm(r_h), tm(k_h), tm(v_h), tm(kk), tm(kk * a_h), tm(d_h)))
    y = tm(y)
    mu = jnp.mean(y, axis=-1, keepdims=True)
    var = jnp.mean(jnp.square(y - mu), axis=-1, keepdims=True)
    yn = ((y - mu) * lax.rsqrt(var + GN_EPS)).reshape(b, t, TOK_WIDTH)
    yn = yn * P['a_ln_w'][j].astype(f32) + P['a_ln_b'][j].astype(f32)
    r_k = P['a_r_k'][j].astype(f32).reshape(A_HEADS, A_HEAD_DIM)
    bonus = (jnp.sum(r_h * k_h * r_k, axis=-1, keepdims=True) * v_h).reshape(b, t, TOK_WIDTH)
    out = (yn + bonus) * g.astype(f32)
    return out.astype(p.dtype), s_fin


def gla_chunked(q, k, v, log_f, s0):
    b, t, h, _ = q.shape
    vd = v.shape[-1]
    c = CHUNK if t % CHUNK == 0 else t
    n = t // c
    to_chunks = lambda z: z.astype(jnp.float32).reshape(b, n, c, h, z.shape[-1]).transpose(1, 0, 3, 2, 4)
    qc, kc, vc, gc = to_chunks(q), to_chunks(k), to_chunks(v), to_chunks(log_f)
    bc = jnp.cumsum(gc, axis=3)
    mask = jnp.tril(jnp.ones((c, c), dtype=bool))
    mid = (c - 1) // 2

    def step(S, inp):
        q_, k_, v_, b_ = inp
        m = b_[:, :, mid:mid + 1, :]
        att = jnp.einsum('bhtk,bhsk->bhts', q_ * jnp.exp(b_ - m), k_ * jnp.exp(m - b_))
        att = jnp.where(mask, att, 0.0)
        o = jnp.einsum('bhts,bhsv->bhtv', att, v_) + jnp.einsum('bhtk,bhkv->bhtv', q_ * jnp.exp(b_), S)
        b_last = b_[:, :, -1:, :]
        S = jnp.exp(b_last[:, :, 0, :])[..., None] * S + jnp.einsum('bhsk,bhsv->bhkv', k_ * jnp.exp(b_last - b_), v_)
        return S, o

    s_fin, o = lax.scan(step, s0.astype(jnp.float32), (qc, kc, vc, bc))
    o = o.transpose(1, 0, 3, 2, 4).reshape(b, t, h, vd)
    return o, s_fin


def hgrn2_mix(p, s0, lb, g_norm):
    b, t, _ = p.shape
    f32 = jnp.float32
    q, f, i, og = jnp.split(p, 4, axis=-1)
    fg = lb + (1.0 - lb) * jax.nn.sigmoid(f.astype(f32))
    heads = lambda z, d: z.astype(f32).reshape(b, t, B_HEADS, d)
    o, s_fin = gla_chunked(heads(jax.nn.silu(q), B_EXPAND), heads(1.0 - fg, B_EXPAND),
                           heads(i, B_HEAD_V), heads(jnp.log(fg), B_EXPAND), s0)
    o = rmsnorm(o, g_norm).reshape(b, t, TOK_WIDTH)
    out = o * jax.nn.silu(og.astype(f32))
    return out.astype(p.dtype), s_fin


def conv_ffn(h, buf, w_up, conv_w, conv_b, w_down):
    t = h.shape[1]
    a, v = jnp.split(h @ w_up, 2, axis=-1)
    a_ext = jnp.concatenate([buf.astype(a.dtype), a], axis=1)
    c = conv_b
    for tap in range(CONV_W):
        c = c + a_ext[:, tap:tap + t] * conv_w[tap]
    y = (jax.nn.gelu(c) * v) @ w_down
    return y, a_ext[:, t:]


def trunk(x, mem_k, mem_v, st_rwkv, st_shift, st_hgrn, st_conv, P):
    b, t, _ = x.shape
    lb_all = jnp.cumsum(jax.nn.softmax(P['b_lower_bounds'].astype(jnp.float32), axis=0), axis=0)
    lb_all = lb_all - lb_all[0]
    new_rwkv, new_shift, new_hgrn, new_conv = [], [], [], []
    for layer in range(DEPTH):
        j = layer // N_MIXERS
        h = rmsnorm(x, P['norm_mix'][layer])
        if layer % N_MIXERS == 0:
            h_ext = jnp.concatenate([st_shift[j][:, None, :].astype(h.dtype), h], axis=1)
            p_ext = h_ext @ P['a_w_in'][j]
            p = p_ext[:, 1:]
            tok, s_new = rwkv7_mix(p[..., MEM_WIDTH:], p_ext[:, :-1, MEM_WIDTH:], st_rwkv[j], P, j)
            new_rwkv.append(s_new)
            new_shift.append(h[:, -1])
            w_out = P['a_w_out'][j]
        else:
            p = h @ P['b_w_in'][j]
            tok, s_new = hgrn2_mix(p[..., MEM_WIDTH:], st_hgrn[j], lb_all[layer], P['b_g_norm'][j])
            new_hgrn.append(s_new)
            w_out = P['b_w_out'][j]
        q_mem = p[..., :MEM_WIDTH].reshape(b, t, MEM_HEADS, MEM_HEAD_DIM)
        mem_o = memory_attend(q_mem, mem_k[layer], mem_v[layer])
        x = x + jnp.concatenate([tok, mem_o.astype(tok.dtype)], axis=-1) @ w_out
        f, c_new = conv_ffn(rmsnorm(x, P['norm_ffn'][layer]), st_conv[layer], P['ffn_w_up'][layer],
                            P['ffn_conv_w'][layer], P['ffn_conv_b'][layer], P['ffn_w_down'][layer])
        new_conv.append(c_new)
        x = x + f
    return (rmsnorm(x, P['norm_final']), jnp.stack(new_rwkv), jnp.stack(new_shift),
            jnp.stack(new_hgrn), jnp.stack(new_conv))


def setup_inputs(seed: int = 0) -> dict:
    key = jax.random.key(seed)
    ks = iter(jax.random.split(key, 48))
    nrm = lambda shape, scale=1.0: jax.random.normal(next(ks), shape, jnp.float32) * scale
    uni = lambda shape, lo, hi: jax.random.uniform(next(ks), shape, jnp.float32, lo, hi)
    D = D_MODEL
    return {
        'x_prompt': nrm((BATCH, SEQ, D)),
        'x_sample': nrm((DEC_BATCH, DEC_SEQ, D)),
        'mem_prompt': nrm((BATCH, N_MEM, D)),
        'cache_mem_k': nrm((DEPTH, DEC_BATCH, N_MEM, MEM_HEADS, MEM_HEAD_DIM)),
        'cache_mem_v': nrm((DEPTH, DEC_BATCH, N_MEM, MEM_HEADS, MEM_HEAD_DIM)),
        'state_rwkv': nrm((N_A_LAYERS, DEC_BATCH, A_HEADS, A_HEAD_DIM, A_HEAD_DIM), 0.3),
        'state_shift': nrm((N_A_LAYERS, DEC_BATCH, D)),
        'state_hgrn': nrm((N_B_LAYERS, DEC_BATCH, B_HEADS, B_EXPAND, B_HEAD_V), 0.3),
        'state_conv': nrm((DEPTH, DEC_BATCH, CONV_W - 1, D_FF)),
        'norm_mix': 1.0 + nrm((DEPTH, D), 0.02),
        'norm_ffn': 1.0 + nrm((DEPTH, D), 0.02),
        'norm_final': 1.0 + nrm((D,), 0.02),
        'mem_norm': 1.0 + nrm((DEPTH, D), 0.02),
        'w_mem_kv': nrm((DEPTH, D, 2 * MEM_WIDTH), D ** -0.5),
        'a_w_in': nrm((N_A_LAYERS, D, MEM_WIDTH + A_PROJ), D ** -0.5),
        'a_mu': uni((N_A_LAYERS, A_PROJ), 0.0, 1.0),
        'a_w0': uni((N_A_LAYERS, TOK_WIDTH), -5.0, 0.0),
        'a_w2': nrm((N_A_LAYERS, A_DECAY_RANK, TOK_WIDTH), 0.1 * A_DECAY_RANK ** -0.5),
        'a_a0': nrm((N_A_LAYERS, TOK_WIDTH), 0.1),
        'a_a2': nrm((N_A_LAYERS, A_ICLR_RANK, TOK_WIDTH), 0.1 * A_ICLR_RANK ** -0.5),
        'a_g2': nrm((N_A_LAYERS, A_GATE_RANK, TOK_WIDTH), A_GATE_RANK ** -0.5),
        'a_k_k': 0.85 + nrm((N_A_LAYERS, TOK_WIDTH), 0.05),
        'a_k_a': 1.0 + nrm((N_A_LAYERS, TOK_WIDTH), 0.05),
        'a_r_k': nrm((N_A_LAYERS, TOK_WIDTH), 0.1),
        'a_ln_w': 1.0 + nrm((N_A_LAYERS, TOK_WIDTH), 0.02),
        'a_ln_b': nrm((N_A_LAYERS, TOK_WIDTH), 0.02),
        'a_w_out': nrm((N_A_LAYERS, D, D), D ** -0.5),
        'b_w_in': nrm((N_B_LAYERS, D, MEM_WIDTH + B_PROJ), D ** -0.5),
        'b_lower_bounds': nrm((DEPTH, TOK_WIDTH), 0.1),
        'b_g_norm': 1.0 + nrm((N_B_LAYERS, B_HEAD_V), 0.02),
        'b_w_out': nrm((N_B_LAYERS, D, D), D ** -0.5),
        'ffn_w_up': nrm((DEPTH, D, 2 * D_FF), D ** -0.5),
        'ffn_conv_w': nrm((DEPTH, CONV_W, D_FF), CONV_W ** -0.5),
        'ffn_conv_b': nrm((DEPTH, D_FF), 0.02),
        'ffn_w_down': nrm((DEPTH, D_FF, D), D_FF ** -0.5),
    }


def reference(x_prompt, x_sample, mem_prompt, cache_mem_k, cache_mem_v, state_rwkv, state_shift,
              state_hgrn, state_conv, norm_mix, norm_ffn, norm_final, mem_norm, w_mem_kv,
              a_w_in, a_mu, a_w0, a_w2, a_a0, a_a2, a_g2, a_k_k, a_k_a, a_r_k, a_ln_w, a_ln_b, a_w_out,
              b_w_in, b_lower_bounds, b_g_norm, b_w_out, ffn_w_up, ffn_conv_w, ffn_conv_b, ffn_w_down):
    P = dict(norm_mix=norm_mix, norm_ffn=norm_ffn, norm_final=norm_final,
             a_w_in=a_w_in, a_mu=a_mu, a_w0=a_w0, a_w2=a_w2, a_a0=a_a0, a_a2=a_a2, a_g2=a_g2,
             a_k_k=a_k_k, a_k_a=a_k_a, a_r_k=a_r_k, a_ln_w=a_ln_w, a_ln_b=a_ln_b, a_w_out=a_w_out,
             b_w_in=b_w_in, b_lower_bounds=b_lower_bounds, b_g_norm=b_g_norm, b_w_out=b_w_out,
             ffn_w_up=ffn_w_up, ffn_conv_w=ffn_conv_w, ffn_conv_b=ffn_conv_b, ffn_w_down=ffn_w_down)
    bp = x_prompt.shape[0]
    dt = x_prompt.dtype
    mem_h = rmsnorm(mem_prompt[None], mem_norm[:, None, None, :])
    mem_kv = jnp.einsum('lbmd,ldk->lbmk', mem_h, w_mem_kv)
    mem_k_prompt = mem_kv[..., :MEM_WIDTH].reshape(DEPTH, bp, N_MEM, MEM_HEADS, MEM_HEAD_DIM)
    mem_v_prompt = mem_kv[..., MEM_WIDTH:].reshape(DEPTH, bp, N_MEM, MEM_HEADS, MEM_HEAD_DIM)
    y_prompt, rwkv_prompt, shift_prompt, hgrn_prompt, conv_prompt = trunk(
        x_prompt, mem_k_prompt, mem_v_prompt,
        jnp.zeros((N_A_LAYERS, bp, A_HEADS, A_HEAD_DIM, A_HEAD_DIM), dt),
        jnp.zeros((N_A_LAYERS, bp, D_MODEL), dt),
        jnp.zeros((N_B_LAYERS, bp, B_HEADS, B_EXPAND, B_HEAD_V), dt),
        jnp.zeros((DEPTH, bp, CONV_W - 1, D_FF), dt), P)
    y_sample, rwkv_sample, shift_sample, hgrn_sample, conv_sample = trunk(
        x_sample, cache_mem_k, cache_mem_v, state_rwkv, state_shift, state_hgrn, state_conv, P)
    return (y_prompt, y_sample, mem_k_prompt, mem_v_prompt, rwkv_prompt, rwkv_sample,
            shift_prompt, shift_sample, hgrn_prompt, hgrn_sample, conv_prompt, conv_sample)
```

```python
import functools

import jax
import jax.numpy as jnp
from jax import lax
from jax.experimental import pallas as pl
from jax.experimental.pallas import tpu as pltpu

F32 = jnp.float32
BF16 = jnp.bfloat16

D_MODEL = 2048
DEPTH = 2
TOK_WIDTH = 1536
MEM_WIDTH = 512
MEM_HEADS = 4
MEM_HEAD_DIM = 128
N_MEM = 256
A_HEAD_DIM = 64
A_HEADS = 24
A_LORA_PAD = 128
A_GATE_RANK = 256
A_TOK_COLS = 3 * TOK_WIDTH + 2 * A_LORA_PAD + A_GATE_RANK
B_HEADS = 12
B_HEAD_DIM = 128
B_TOK_COLS = 4 * TOK_WIDTH
D_FF = 5632
RMS_EPS = 1e-6
GN_EPS = 64e-5

LANES = 128
VMEM_LIMIT = 56 * 1024 * 1024
A_GROUP = 4


def _cparams(*sem):
    return pltpu.CompilerParams(dimension_semantics=sem, vmem_limit_bytes=VMEM_LIMIT)


def _dot(a, b):
    return jnp.dot(a, b, preferred_element_type=F32)


def _dot_nt(a, b):
    return lax.dot_general(a, b, (((1,), (1,)), ((), ())), preferred_element_type=F32)


def _dot_tn(a, b):
    return lax.dot_general(a, b, (((0,), (0,)), ((), ())), preferred_element_type=F32)


def _bf(x):
    return x.astype(BF16)


def _split(x):
    hi = x.astype(BF16)
    lo = (x - hi.astype(F32)).astype(BF16)
    return hi, lo


def _sigmoid(x):
    return 1.0 / (1.0 + jnp.exp(-x))


def _rms(x, g):
    ms = jnp.mean(x * x, axis=-1, keepdims=True)
    return x * lax.rsqrt(ms + RMS_EPS) * g


def _rmsnorm_kernel(x_ref, g_ref, o_ref):
    o_ref[...] = _rms(x_ref[...], g_ref[...])


def rmsnorm(x, g, tm):
    m, d = x.shape
    return pl.pallas_call(
        _rmsnorm_kernel,
        grid=(m // tm,),
        in_specs=[pl.BlockSpec((tm, d), lambda i: (i, 0)), pl.BlockSpec((1, d), lambda i: (0, 0))],
        out_specs=pl.BlockSpec((tm, d), lambda i: (i, 0)),
        out_shape=jax.ShapeDtypeStruct((m, d), F32),
        compiler_params=_cparams("parallel"),
        name="rmsnorm",
    )(x, g.reshape(1, d))


NORM_ROWS = 256


def _norm_to_scratch(x_ref, g, hb_ref, tm, dst_off=0):
    for r in range(0, tm, NORM_ROWS):
        n = min(NORM_ROWS, tm - r)
        hb_ref[dst_off + r:dst_off + r + n, :] = _bf(_rms(x_ref[r:r + n, :], g))


def _norm_matmul_kernel(x_ref, g_ref, w_ref, o_ref, hb_ref, *, tm):
    @pl.when(pl.program_id(1) == 0)
    def _():
        _norm_to_scratch(x_ref, g_ref[...], hb_ref, tm)

    o_ref[...] = _dot(hb_ref[...], w_ref[...])


def norm_matmul(x, g, w, tm, tn):
    m, d = x.shape
    n = w.shape[1]
    return pl.pallas_call(
        functools.partial(_norm_matmul_kernel, tm=tm),
        grid=(m // tm, n // tn),
        in_specs=[pl.BlockSpec((tm, d), lambda i, j: (i, 0)),
                  pl.BlockSpec((1, d), lambda i, j: (0, 0)),
                  pl.BlockSpec((d, tn), lambda i, j: (0, j))],
        out_specs=pl.BlockSpec((tm, tn), lambda i, j: (i, j)),
        out_shape=jax.ShapeDtypeStruct((m, n), F32),
        scratch_shapes=[pltpu.VMEM((tm, d), BF16)],
        compiler_params=_cparams("parallel", "arbitrary"),
        name="norm_matmul",
    )(x, g.reshape(1, d), w)


def _mm_kernel(*refs, n_pairs, has_res):
    a = refs[:n_pairs]
    w = refs[n_pairs:2 * n_pairs]
    o_ref = refs[-1]
    acc = _dot(_bf(a[0][...]), w[0][...])
    for i in range(1, n_pairs):
        acc = acc + _dot(_bf(a[i][...]), w[i][...])
    if has_res:
        acc = refs[2 * n_pairs][...] + acc
    o_ref[...] = acc


def matmul(a_list, w_list, res, n_out, tm, tn):
    m = a_list[0].shape[0]
    n_pairs = len(a_list)
    in_specs = [pl.BlockSpec((tm, a.shape[1]), lambda i, j: (i, 0)) for a in a_list]
    in_specs += [pl.BlockSpec((w.shape[0], tn), lambda i, j: (0, j)) for w in w_list]
    args = list(a_list) + list(w_list)
    if res is not None:
        in_specs.append(pl.BlockSpec((tm, tn), lambda i, j: (i, j)))
        args.append(res)
    return pl.pallas_call(
        functools.partial(_mm_kernel, n_pairs=n_pairs, has_res=res is not None),
        grid=(m // tm, n_out // tn),
        in_specs=in_specs,
        out_specs=pl.BlockSpec((tm, tn), lambda i, j: (i, j)),
        out_shape=jax.ShapeDtypeStruct((m, n_out), F32),
        compiler_params=_cparams("parallel", "parallel"),
        name="matmul",
    )(*args)


def _attn_kernel(q_ref, k_ref, v_ref, o_ref, *, n_seq, tq):
    scale = MEM_HEAD_DIM ** -0.5
    rows = []
    for s in range(n_seq):
        heads = []
        for h in range(MEM_HEADS):
            sl = slice(h * MEM_HEAD_DIM, (h + 1) * MEM_HEAD_DIM)
            q = _bf(q_ref[s * tq:(s + 1) * tq, sl])
            k = _bf(k_ref[s * N_MEM:(s + 1) * N_MEM, sl])
            v = _bf(v_ref[s * N_MEM:(s + 1) * N_MEM, sl])
            sc = _dot_nt(q, k) * scale
            e = jnp.exp(sc - jnp.max(sc, axis=-1, keepdims=True))
            p = e / jnp.sum(e, axis=-1, keepdims=True)
            heads.append(_dot(_bf(p), v))
        rows.append(jnp.concatenate(heads, axis=1))
    o_ref[...] = _bf(jnp.concatenate(rows, axis=0) if n_seq > 1 else rows[0])


def mem_attention(p, q_colblock, k_arr, k_colblock, v_arr, v_colblock, n_batch, seq, n_seq, tq):
    m = p.shape[0]
    q_tiles = seq // tq if n_seq == 1 else 1
    rows = n_seq * tq
    if n_seq == 1:
        grid = (n_batch, q_tiles)
        q_map = lambda b, t: (b * q_tiles + t, q_colblock)
        k_map = lambda b, t: (b, k_colblock)
        v_map = lambda b, t: (b, v_colblock)
        o_map = lambda b, t: (b * q_tiles + t, 0)
        sem = ("parallel", "parallel")
    else:
        grid = (n_batch // n_seq,)
        q_map = lambda b: (b, q_colblock)
        k_map = lambda b: (b, k_colblock)
        v_map = lambda b: (b, v_colblock)
        o_map = lambda b: (b, 0)
        sem = ("parallel",)
    return pl.pallas_call(
        functools.partial(_attn_kernel, n_seq=n_seq, tq=tq),
        grid=grid,
        in_specs=[pl.BlockSpec((rows, MEM_WIDTH), q_map),
                  pl.BlockSpec((n_seq * N_MEM, MEM_WIDTH), k_map),
                  pl.BlockSpec((n_seq * N_MEM, MEM_WIDTH), v_map)],
        out_specs=pl.BlockSpec((rows, MEM_WIDTH), o_map),
        out_shape=jax.ShapeDtypeStruct((m, MEM_WIDTH), BF16),
        compiler_params=_cparams(*sem),
        name="mem_attention",
    )(p, k_arr, v_arr)


FFN_HALO = 16


def _gelu_gate(c, v):
    return _bf(jax.nn.gelu(c) * v)


def _ffn_up_prompt_kernel(x_ref, xh_ref, g_ref, wa_ref, wv_ref, cw_ref, cb_ref, u_ref, cs_ref, hb_ref,
                          *, tm, tiles_per_seq):
    i = pl.program_id(0)

    @pl.when(pl.program_id(1) == 0)
    def _():
        g = g_ref[...]
        hb_ref[0:FFN_HALO, :] = _bf(_rms(xh_ref[...], g))
        _norm_to_scratch(x_ref, g, hb_ref, tm, dst_off=FFN_HALO)

    a_ext = _dot(hb_ref[...], wa_ref[...])
    v = _dot(hb_ref[FFN_HALO:, :], wv_ref[...])
    rows = lax.broadcasted_iota(jnp.int32, (FFN_HALO + tm, 1), 0)
    n_zero = jnp.where((i % tiles_per_seq) == 0, FFN_HALO, 0)
    a_ext = jnp.where(rows < n_zero, 0.0, a_ext)
    a0 = a_ext[FFN_HALO:]
    a1 = pltpu.roll(a_ext, 1, 0)[FFN_HALO:]
    a2 = pltpu.roll(a_ext, 2, 0)[FFN_HALO:]
    c = cb_ref[...] + a2 * cw_ref[0:1, :] + a1 * cw_ref[1:2, :] + a0 * cw_ref[2:3, :]
    u_ref[...] = _gelu_gate(c, v)
    cs_ref[0] = a0[tm - 8:tm][6:8]


def ffn_up_prompt(x, g, w_up, cw, cb, n_batch, seq, tm, tn):
    m, d = x.shape
    nf = D_FF // tn
    tiles_per_seq = seq // tm
    halo_blocks = tm // FFN_HALO
    u, cs = pl.pallas_call(
        functools.partial(_ffn_up_prompt_kernel, tm=tm, tiles_per_seq=tiles_per_seq),
        grid=(m // tm, nf),
        in_specs=[pl.BlockSpec((tm, d), lambda i, j: (i, 0)),
                  pl.BlockSpec((FFN_HALO, d), lambda i, j: (jnp.maximum(i * halo_blocks - 1, 0), 0)),
                  pl.BlockSpec((1, d), lambda i, j: (0, 0)),
                  pl.BlockSpec((d, tn), lambda i, j: (0, j)),
                  pl.BlockSpec((d, tn), lambda i, j: (0, nf + j)),
                  pl.BlockSpec((3, tn), lambda i, j: (0, j)),
                  pl.BlockSpec((1, tn), lambda i, j: (0, j))],
        out_specs=[pl.BlockSpec((tm, tn), lambda i, j: (i, j)),
                   pl.BlockSpec((1, 2, tn), lambda i, j: (i, 0, j))],
        out_shape=[jax.ShapeDtypeStruct((m, D_FF), BF16),
                   jax.ShapeDtypeStruct((m // tm, 2, D_FF), F32)],
        scratch_shapes=[pltpu.VMEM((FFN_HALO + tm, d), BF16)],
        compiler_params=_cparams("parallel", "arbitrary"),
        name="ffn_up_prompt",
    )(x, x, g.reshape(1, d), w_up, w_up, cw, cb.reshape(1, D_FF))
    return u, cs.reshape(n_batch, tiles_per_seq, 2, D_FF)[:, -1]


def _ffn_up_sample_kernel(x_ref, g_ref, wa_ref, wv_ref, cw_ref, cb_ref, p1_ref, p2_ref,
                          u_ref, c6_ref, c7_ref, hb_ref, a_ref, *, tm, seq):
    @pl.when(pl.program_id(1) == 0)
    def _():
        _norm_to_scratch(x_ref, g_ref[...], hb_ref, tm)

    a = _dot(hb_ref[...], wa_ref[...])
    v = _dot(hb_ref[...], wv_ref[...])
    t = lax.broadcasted_iota(jnp.int32, (tm, 1), 0) % seq
    a1 = jnp.where(t >= 1, pltpu.roll(a, 1, 0), p1_ref[...])
    a2 = jnp.where(t >= 2, pltpu.roll(a, 2, 0), p2_ref[...])
    c = cb_ref[...] + a2 * cw_ref[0:1, :] + a1 * cw_ref[1:2, :] + a * cw_ref[2:3, :]
    u_ref[...] = _gelu_gate(c, v)
    for q in range(a.shape[1] // LANES):
        sl = slice(q * LANES, (q + 1) * LANES)
        a_ref[q] = a[:, sl]
        c6_ref[:, sl] = a_ref[q, pl.ds(seq - 2, tm // seq, stride=seq), :]
        c7_ref[:, sl] = a_ref[q, pl.ds(seq - 1, tm // seq, stride=seq), :]


def ffn_up_sample(x, g, w_up, cw, cb, p1, p2, seq, tm, tn):
    m, d = x.shape
    nf = D_FF // tn
    nb = tm // seq
    return pl.pallas_call(
        functools.partial(_ffn_up_sample_kernel, tm=tm, seq=seq),
        grid=(m // tm, nf),
        in_specs=[pl.BlockSpec((tm, d), lambda i, j: (i, 0)),
                  pl.BlockSpec((1, d), lambda i, j: (0, 0)),
                  pl.BlockSpec((d, tn), lambda i, j: (0, j)),
                  pl.BlockSpec((d, tn), lambda i, j: (0, nf + j)),
                  pl.BlockSpec((3, tn), lambda i, j: (0, j)),
                  pl.BlockSpec((1, tn), lambda i, j: (0, j)),
                  pl.BlockSpec((tm, tn), lambda i, j: (i, j)),
                  pl.BlockSpec((tm, tn), lambda i, j: (i, j))],
        out_specs=[pl.BlockSpec((tm, tn), lambda i, j: (i, j)),
                   pl.BlockSpec((nb, tn), lambda i, j: (i, j)),
                   pl.BlockSpec((nb, tn), lambda i, j: (i, j))],
        out_shape=[jax.ShapeDtypeStruct((m, D_FF), BF16),
                   jax.ShapeDtypeStruct((m // seq, D_FF), F32),
                   jax.ShapeDtypeStruct((m // seq, D_FF), F32)],
        scratch_shapes=[pltpu.VMEM((tm, d), BF16), pltpu.VMEM((tn // LANES, tm, LANES), F32)],
        compiler_params=_cparams("parallel", "arbitrary"),
        name="ffn_up_sample",
    )(x, g.reshape(1, d), w_up, w_up, cw, cb.reshape(1, D_FF), p1, p2)


def _seg_sum(x, e):
    parts = []
    for j in range(x.shape[1] // LANES):
        hi, lo = _split(x[:, j * LANES:(j + 1) * LANES])
        parts.append(_dot(hi, e) + _dot(lo, e))
    return jnp.concatenate(parts, axis=1)


def _unit_lower_solve(a, rhs, c):
    mm = lambda x, y: _dot(_bf(x), _bf(y))
    rows = lax.broadcasted_iota(jnp.int32, (c, c), 0)
    cols = lax.broadcasted_iota(jnp.int32, (c, c), 1)
    eye = (rows == cols).astype(F32)
    blk = min(c, 16)
    if c > blk:
        same = (rows // blk) == (cols // blk)
        ad = jnp.where(same, a, 0.0)
        ao = a - ad
    else:
        ad = a
    t = eye - ad
    pw = ad
    span = 2
    while span < blk:
        pw = mm(pw, pw)
        t = t + mm(t, pw)
        span *= 2
    x = mm(t, rhs)
    if c > blk:
        assert c // blk <= 4
        n = mm(t, ao)
        x = x + mm(mm(n, n), x)
        x = x - mm(n, x)
    return x


def _rwkv_kernel(p_ref, ps_ref, s0_ref, mu_ref, vec_ref, w2_ref, a2_ref, g2_ref, e_ref, lt_ref,
                 tok_ref, sout_ref,
                 prev_ref, s_ref, xk_ref, xr_ref, kb_ref, bb_ref, v_ref, kh_ref, bh_ref, gam_ref, y_ref,
                 *, c):
    ci = pl.program_id(1)
    tw = TOK_WIDTH
    gw = A_GROUP * A_HEAD_DIM
    n_groups = tw // gw

    @pl.when(ci == 0)
    def _():
        s_ref[...] = s0_ref[0]
        prev_ref[...] = ps_ref[0]

    p = p_ref[...]
    rows = lax.broadcasted_iota(jnp.int32, (c, 1), 0)
    p_prev = jnp.where(rows == 0, prev_ref[...], pltpu.roll(p, 1, 0))
    prev_ref[...] = p_ref[c - 1:c, :]
    xm = p + mu_ref[...] * (p_prev - p)

    r = xm[:, 0:tw]
    k = xm[:, tw:2 * tw]
    v = xm[:, 2 * tw:3 * tw]
    xw = xm[:, 3 * tw:3 * tw + A_LORA_PAD]
    xa = xm[:, 3 * tw + A_LORA_PAD:3 * tw + 2 * A_LORA_PAD]
    xg = xm[:, 3 * tw + 2 * A_LORA_PAD:]
    w0, a0, k_k, k_a, r_k, ln_w, ln_b = (vec_ref[i:i + 1, :] for i in range(7))

    z = -(w0 + _dot(_bf(jnp.tanh(xw)), w2_ref[...]))
    softplus = jnp.maximum(z, 0.0) + jnp.log(1.0 + jnp.exp(-jnp.abs(z)))
    ell = -jnp.exp(-softplus - 0.5)
    a = _sigmoid(a0 + _dot(_bf(xa), a2_ref[...]))
    gate = _dot(_bf(_sigmoid(xg)), g2_ref[...])
    kkraw = k * k_k
    k2 = k * (1.0 + (a - 1.0) * k_a)
    e = e_ref[...]
    kap = kkraw * lax.rsqrt(jnp.maximum(_seg_sum(kkraw * kkraw, e), 1e-24))
    bet = kap * a

    ell_hi, ell_lo = _split(ell)
    gc = _dot(lt_ref[...], ell_hi) + _dot(lt_ref[...], ell_lo)
    glast = jnp.sum(ell, axis=0, keepdims=True)
    egi = jnp.exp(-gc)
    el = jnp.exp(glast - gc)
    xk = kap * jnp.exp(gc - ell)
    xr = r * jnp.exp(gc)
    kb = k2 * egi
    bb = bet * egi
    kh = k2 * el
    bh = bet * el
    gam = jnp.exp(glast)
    for gi in range(n_groups):
        sl = slice(gi * gw, (gi + 1) * gw)
        xk_ref[gi] = xk[:, sl]
        xr_ref[gi] = xr[:, sl]
        kb_ref[gi] = kb[:, sl]
        bb_ref[gi] = bb[:, sl]
        v_ref[gi] = v[:, sl]
        kh_ref[gi] = kh[:, sl]
        bh_ref[gi] = bh[:, sl]
        gam_ref[gi] = gam[:, sl]

    ti = lax.broadcasted_iota(jnp.int32, (c, c), 0)
    si = lax.broadcasted_iota(jnp.int32, (c, c), 1)
    strict = si < ti
    incl = si <= ti

    def group_body(gi, carry):
        xk_g, xr_g, kb_g, bb_g = xk_ref[gi], xr_ref[gi], kb_ref[gi], bb_ref[gi]
        v_g, kh_g, bh_g, gam_g = v_ref[gi], kh_ref[gi], bh_ref[gi], gam_ref[gi]
        for hh in range(A_GROUP):
            sl = slice(hh * A_HEAD_DIM, (hh + 1) * A_HEAD_DIM)
            hidx = gi * A_GROUP + hh
            s = s_ref[hidx]
            sb = _bf(s)
            xk_h, xr_h, kb_h, bb_h = _bf(xk_g[:, sl]), _bf(xr_g[:, sl]), _bf(kb_g[:, sl]), _bf(bb_g[:, sl])
            v_h = _bf(v_g[:, sl])
            a_kk = jnp.where(strict, _dot_nt(xk_h, kb_h), 0.0)
            a_kb = jnp.where(strict, _dot_nt(xk_h, bb_h), 0.0)
            a_rk = jnp.where(incl, _dot_nt(xr_h, kb_h), 0.0)
            a_rb = jnp.where(incl, _dot_nt(xr_h, bb_h), 0.0)
            rhs = -(_dot_nt(xk_h, sb) + _dot(_bf(a_kk), v_h))
            u = _unit_lower_solve(a_kb, rhs, c)
            u_h = _bf(u)
            y_ref[gi, :, sl] = _dot_nt(xr_h, sb) + _dot(_bf(a_rk), v_h) + _dot(_bf(a_rb), u_h)
            s_ref[hidx] = (s * gam_g[:, sl] + _dot_tn(v_h, _bf(kh_g[:, sl]))
                           + _dot_tn(u_h, _bf(bh_g[:, sl])))
        return carry

    lax.fori_loop(0, n_groups, group_body, 0)

    y = jnp.concatenate([y_ref[gi] for gi in range(n_groups)], axis=1)
    inv_n = 1.0 / A_HEAD_DIM
    mean = _seg_sum(y, e) * inv_n
    d = y - mean
    var = _seg_sum(d * d, e) * inv_n
    yn = d * lax.rsqrt(var + GN_EPS) * ln_w + ln_b
    bonus = _seg_sum(r * k2 * r_k, e) * v
    tok_ref[...] = ((yn + bonus) * gate).astype(tok_ref.dtype)

    @pl.when(ci == pl.num_programs(1) - 1)
    def _():
        sout_ref[0] = s_ref[...]


def rwkv_mix(p, pshift, s0, mu, vecs, w2, a2, g2, n_batch, seq, c, out_dtype):
    m = p.shape[0]
    n_chunks = seq // c
    gw = A_GROUP * A_HEAD_DIM
    n_groups = TOK_WIDTH // gw
    lane = jnp.arange(LANES)
    e = (lane[:, None] // A_HEAD_DIM == lane[None, :] // A_HEAD_DIM).astype(BF16)
    t = jnp.arange(c)
    ltri = (t[None, :] <= t[:, None]).astype(BF16)
    const = lambda shape: pl.BlockSpec(shape, lambda b, ci: (0,) * len(shape))
    grp = lambda rows: pltpu.VMEM((n_groups, rows, gw), F32)
    return pl.pallas_call(
        functools.partial(_rwkv_kernel, c=c),
        grid=(n_batch, n_chunks),
        in_specs=[pl.BlockSpec((c, A_TOK_COLS), lambda b, ci: (b * n_chunks + ci, 0)),
                  pl.BlockSpec((1, 1, A_TOK_COLS), lambda b, ci: (b, 0, 0)),
                  pl.BlockSpec((1, A_HEADS, A_HEAD_DIM, A_HEAD_DIM), lambda b, ci: (b, 0, 0, 0)),
                  const((1, A_TOK_COLS)), const((8, TOK_WIDTH)),
                  const((A_LORA_PAD, TOK_WIDTH)), const((A_LORA_PAD, TOK_WIDTH)),
                  const((A_GATE_RANK, TOK_WIDTH)), const((LANES, LANES)), const((c, c))],
        out_specs=[pl.BlockSpec((c, TOK_WIDTH), lambda b, ci: (b * n_chunks + ci, 0)),
                   pl.BlockSpec((1, A_HEADS, A_HEAD_DIM, A_HEAD_DIM), lambda b, ci: (b, 0, 0, 0))],
        out_shape=[jax.ShapeDtypeStruct((m, TOK_WIDTH), out_dtype),
                   jax.ShapeDtypeStruct((n_batch, A_HEADS, A_HEAD_DIM, A_HEAD_DIM), F32)],
        scratch_shapes=[pltpu.VMEM((1, A_TOK_COLS), F32),
                        pltpu.VMEM((A_HEADS, A_HEAD_DIM, A_HEAD_DIM), F32),
                        grp(c), grp(c), grp(c), grp(c), grp(c), grp(c), grp(c), grp(1), grp(c)],
        compiler_params=_cparams("parallel", "arbitrary"),
        name="rwkv_mix",
    )(p, pshift, s0, mu, vecs, w2, a2, g2, e, ltri)


def _hgrn_kernel(p_ref, s0_ref, lbp_ref, gn_ref, lt_ref, tok_ref, sout_ref, st_ref, *, c, layer):
    ci = pl.program_id(1)
    tw = TOK_WIDTH
    hd = B_HEAD_DIM

    @pl.when(ci == 0)
    def _():
        for h in range(B_HEADS):
            st_ref[h] = s0_ref[0, h].T

    lbp = lbp_ref[...]
    mx = jnp.max(lbp, axis=0, keepdims=True)
    ex = jnp.exp(lbp - mx)
    den = jnp.sum(ex, axis=0, keepdims=True)
    lb = jnp.zeros((1, tw), F32)
    for i in range(1, layer + 1):
        lb = lb + ex[i:i + 1, :] / den

    q = p_ref[:, 0:tw]
    f = p_ref[:, tw:2 * tw]
    iv = p_ref[:, 2 * tw:3 * tw]
    og = p_ref[:, 3 * tw:4 * tw]
    fg = lb + (1.0 - lb) * _sigmoid(f)
    lf = jnp.log(fg)
    kk = 1.0 - fg
    qq = q * _sigmoid(q)
    lf_hi, lf_lo = _split(lf)
    bc = _dot(lt_ref[...], lf_hi) + _dot(lt_ref[...], lf_lo)
    blast = jnp.sum(lf, axis=0, keepdims=True)
    rows = lax.broadcasted_iota(jnp.int32, (c, 1), 0)
    mid = jnp.sum(jnp.where(rows == (c - 1) // 2, bc, 0.0), axis=0, keepdims=True)
    qe = qq * jnp.exp(bc - mid)
    ke = kk * jnp.exp(mid - bc)
    qs = qq * jnp.exp(bc)
    kl = kk * jnp.exp(blast - bc)
    gam = jnp.exp(blast)
    gate = og * _sigmoid(og)
    ti = lax.broadcasted_iota(jnp.int32, (c, c), 0)
    si = lax.broadcasted_iota(jnp.int32, (c, c), 1)
    incl = si <= ti
    gn = gn_ref[...]
    for h in range(B_HEADS):
        sl = slice(h * hd, (h + 1) * hd)
        st = st_ref[h]
        v_h = _bf(iv[:, sl])
        att = jnp.where(incl, _dot_nt(_bf(qe[:, sl]), _bf(ke[:, sl])), 0.0)
        o = _dot(_bf(att), v_h) + _dot_nt(_bf(qs[:, sl]), _bf(st))
        st_ref[h] = st * gam[:, sl] + _dot_tn(v_h, _bf(kl[:, sl]))
        on = o * lax.rsqrt(jnp.mean(o * o, axis=-1, keepdims=True) + RMS_EPS) * gn
        tok_ref[:, sl] = (on * gate[:, sl]).astype(tok_ref.dtype)

    @pl.when(ci == pl.num_programs(1) - 1)
    def _():
        for h in range(B_HEADS):
            sout_ref[0, h] = st_ref[h].T


def hgrn_mix(p, s0, lbp, gn, n_batch, seq, c, layer, out_dtype):
    m = p.shape[0]
    n_chunks = seq // c
    t = jnp.arange(c)
    ltri = (t[None, :] <= t[:, None]).astype(BF16)
    const = lambda shape: pl.BlockSpec(shape, lambda b, ci: (0,) * len(shape))
    st_spec = pl.BlockSpec((1, B_HEADS, B_HEAD_DIM, B_HEAD_DIM), lambda b, ci: (b, 0, 0, 0))
    return pl.pallas_call(
        functools.partial(_hgrn_kernel, c=c, layer=layer),
        grid=(n_batch, n_chunks),
        in_specs=[pl.BlockSpec((c, B_TOK_COLS), lambda b, ci: (b * n_chunks + ci, 0)),
                  st_spec, const((DEPTH, TOK_WIDTH)), const((1, B_HEAD_DIM)), const((c, c))],
        out_specs=[pl.BlockSpec((c, TOK_WIDTH), lambda b, ci: (b * n_chunks + ci, 0)), st_spec],
        out_shape=[jax.ShapeDtypeStruct((m, TOK_WIDTH), out_dtype),
                   jax.ShapeDtypeStruct((n_batch, B_HEADS, B_HEAD_DIM, B_HEAD_DIM), F32)],
        scratch_shapes=[pltpu.VMEM((B_HEADS, B_HEAD_DIM, B_HEAD_DIM), F32)],
        compiler_params=_cparams("parallel", "arbitrary"),
        name="hgrn_mix",
    )(p, s0, lbp, gn.reshape(1, B_HEAD_DIM), ltri)


PROMPT_CHUNK = 64


def _trunk(x, n_batch, seq, mem, st_rwkv, pshift, st_hgrn, conv_hist, w, is_prompt):
    m = x.shape[0]
    c = PROMPT_CHUNK if is_prompt else seq
    tm_big = 1024
    tok_dtype = BF16 if is_prompt else F32
    new_conv = []

    def attend(p, q_colblock, layer):
        k_arr, k_cb, v_arr, v_cb = mem[layer]
        if is_prompt:
            return mem_attention(p, q_colblock, k_arr, k_cb, v_arr, v_cb, n_batch, seq, 1, 512)
        return mem_attention(p, q_colblock, k_arr, k_cb, v_arr, v_cb, n_batch, seq, 8, seq)

    def ffn(x, layer):
        if is_prompt:
            u, cs = ffn_up_prompt(x, w["norm_ffn"][layer], w["ffn_w_up"][layer], w["ffn_conv_w"][layer],
                                  w["ffn_conv_b"][layer], n_batch, seq, 512, 512)
        else:
            p1, p2 = conv_hist[layer]
            u, c6, c7 = ffn_up_sample(x, w["norm_ffn"][layer], w["ffn_w_up"][layer], w["ffn_conv_w"][layer],
                                      w["ffn_conv_b"][layer], p1, p2, seq, 512, 512)
            cs = jnp.stack([c6, c7], axis=1)
        new_conv.append(cs)
        return matmul([u], [w["ffn_w_down"][layer]], x, D_MODEL, 512, 512)

    p = norm_matmul(x, w["norm_mix"][0], w["a_w_in"], tm_big, 512)
    tok, s_rwkv = rwkv_mix(p, pshift, st_rwkv, w["a_mu"], w["a_vecs"], w["a_w2"], w["a_a2"], w["a_g2"],
                           n_batch, seq, c, tok_dtype)
    mem_o = attend(p, A_TOK_COLS // MEM_WIDTH, 0)
    x = matmul([tok, mem_o], [w["a_w_out_tok"], w["a_w_out_mem"]], x, D_MODEL, tm_big, 512)
    x = ffn(x, 0)
    p = norm_matmul(x, w["norm_mix"][1], w["b_w_in"], tm_big, 512)
    tok, s_hgrn = hgrn_mix(p, st_hgrn, w["b_lower_bounds"], w["b_g_norm"], n_batch, seq, c, 1, tok_dtype)
    mem_o = attend(p, B_TOK_COLS // MEM_WIDTH, 1)
    x = matmul([tok, mem_o], [w["b_w_out_tok"], w["b_w_out_mem"]], x, D_MODEL, tm_big, 512)
    x = ffn(x, 1)
    y = rmsnorm(x, w["norm_final"], 256)
    return y, s_rwkv, s_hgrn, jnp.stack(new_conv)


def kernel(x_prompt, x_sample, mem_prompt, cache_mem_k, cache_mem_v, state_rwkv, state_shift, state_hgrn, state_conv, norm_mix, norm_ffn, norm_final, mem_norm, w_mem_kv, a_w_in, a_mu, a_w0, a_w2, a_a0, a_a2, a_g2, a_k_k, a_k_a, a_r_k, a_ln_w, a_ln_b, a_w_out, b_w_in, b_lower_bounds, b_g_norm, b_w_out, ffn_w_up, ffn_conv_w, ffn_conv_b, ffn_w_down):
    bp, sp, d = x_prompt.shape
    bs, ss, _ = x_sample.shape

    wa = a_w_in[0]
    rkv_end = MEM_WIDTH + 3 * TOK_WIDTH
    zc = jnp.zeros((d, A_LORA_PAD - 96), F32)
    wa_p = _bf(jnp.concatenate([wa[:, MEM_WIDTH:rkv_end], wa[:, rkv_end:rkv_end + 96], zc,
                                wa[:, rkv_end + 96:rkv_end + 192], zc, wa[:, rkv_end + 192:],
                                wa[:, :MEM_WIDTH]], axis=1))
    mu = a_mu[0]
    zv = jnp.zeros((A_LORA_PAD - 96,), F32)
    mu_p = jnp.concatenate([mu[:3 * TOK_WIDTH + 96], zv, mu[3 * TOK_WIDTH + 96:3 * TOK_WIDTH + 192], zv,
                            mu[3 * TOK_WIDTH + 192:]]).reshape(1, A_TOK_COLS)
    zr = jnp.zeros((A_LORA_PAD - 96, TOK_WIDTH), F32)
    wb = b_w_in[0]
    w = dict(
        norm_mix=norm_mix, norm_ffn=norm_ffn, norm_final=norm_final,
        a_w_in=wa_p, a_mu=mu_p,
        a_vecs=jnp.stack([a_w0[0], a_a0[0], a_k_k[0], a_k_a[0], a_r_k[0], a_ln_w[0], a_ln_b[0],
                          jnp.zeros((TOK_WIDTH,), F32)]),
        a_w2=_bf(jnp.concatenate([a_w2[0], zr], axis=0)),
        a_a2=_bf(jnp.concatenate([a_a2[0], zr], axis=0)),
        a_g2=_bf(a_g2[0]),
        a_w_out_tok=_bf(a_w_out[0][:TOK_WIDTH]), a_w_out_mem=_bf(a_w_out[0][TOK_WIDTH:]),
        b_w_in=_bf(jnp.concatenate([wb[:, MEM_WIDTH:], wb[:, :MEM_WIDTH]], axis=1)),
        b_lower_bounds=b_lower_bounds, b_g_norm=b_g_norm[0],
        b_w_out_tok=_bf(b_w_out[0][:TOK_WIDTH]), b_w_out_mem=_bf(b_w_out[0][TOK_WIDTH:]),
        ffn_w_up=_bf(ffn_w_up), ffn_conv_w=ffn_conv_w, ffn_conv_b=ffn_conv_b, ffn_w_down=_bf(ffn_w_down),
    )

    mem_rows = mem_prompt.reshape(bp * N_MEM, d)
    mem_kv = [norm_matmul(mem_rows, mem_norm[l], _bf(w_mem_kv[l]), bp * N_MEM, 512) for l in range(DEPTH)]
    mem_k_prompt = jnp.stack([kv[:, :MEM_WIDTH] for kv in mem_kv]).reshape(DEPTH, bp, N_MEM, MEM_HEADS, MEM_HEAD_DIM)
    mem_v_prompt = jnp.stack([kv[:, MEM_WIDTH:] for kv in mem_kv]).reshape(DEPTH, bp, N_MEM, MEM_HEADS, MEM_HEAD_DIM)
    xp = x_prompt.reshape(bp * sp, d)
    y_p, rwkv_p, hgrn_p, conv_p = _trunk(
        xp, bp, sp, [(kv, 0, kv, 1) for kv in mem_kv],
        jnp.zeros((bp, A_HEADS, A_HEAD_DIM, A_HEAD_DIM), F32), jnp.zeros((bp, 1, A_TOK_COLS), F32),
        jnp.zeros((bp, B_HEADS, B_HEAD_DIM, B_HEAD_DIM), F32), None, w, True)
    shift_p = rmsnorm(x_prompt[:, -1], norm_mix[0], bp)

    xs = x_sample.reshape(bs * ss, d)
    pshift = matmul([state_shift[0]], [wa_p], None, A_TOK_COLS, bs, 512).reshape(bs, 1, A_TOK_COLS)
    mem_s = [(cache_mem_k[l].reshape(bs * N_MEM, MEM_WIDTH), 0, cache_mem_v[l].reshape(bs * N_MEM, MEM_WIDTH), 0)
             for l in range(DEPTH)]
    hist = []
    for l in range(DEPTH):
        h = state_conv[l]
        p1 = jnp.pad(h[:, 1:2], ((0, 0), (0, ss - 1), (0, 0))).reshape(bs * ss, D_FF)
        p2 = jnp.pad(h, ((0, 0), (0, ss - 2), (0, 0))).reshape(bs * ss, D_FF)
        hist.append((p1, p2))
    y_s, rwkv_s, hgrn_s, conv_s = _trunk(xs, bs, ss, mem_s, state_rwkv[0], pshift, state_hgrn[0], hist, w, False)
    shift_s = rmsnorm(x_sample[:, -1], norm_mix[0], bs)

    return (y_p.reshape(bp, sp, d), y_s.reshape(bs, ss, d), mem_k_prompt, mem_v_prompt,
            rwkv_p[None], rwkv_s[None], shift_p[None], shift_s[None], hgrn_p[None], hgrn_s[None],
            conv_p, conv_s)
```

```python
import functools

import jax
import jax.numpy as jnp
from jax import lax
from jax.experimental import pallas as pl
from jax.experimental.pallas import tpu as pltpu

F32 = jnp.float32
BF16 = jnp.bfloat16

D_MODEL = 2048
DEPTH = 2
TOK_WIDTH = 1536
MEM_WIDTH = 512
MEM_HEADS = 4
MEM_HEAD_DIM = 128
N_MEM = 256
A_HEAD_DIM = 64
A_HEADS = 24
A_LORA_PAD = 128
A_GATE_RANK = 256
A_TOK_COLS = 3 * TOK_WIDTH + 2 * A_LORA_PAD + A_GATE_RANK
B_HEADS = 12
B_HEAD_DIM = 128
B_TOK_COLS = 4 * TOK_WIDTH
D_FF = 5632
RMS_EPS = 1e-6
GN_EPS = 64e-5

LANES = 128
VMEM_LIMIT = 56 * 1024 * 1024


def _cparams(*sem):
    return pltpu.CompilerParams(dimension_semantics=sem, vmem_limit_bytes=VMEM_LIMIT)


def _dot(a, b):
    return jnp.dot(a, b, preferred_element_type=F32)


def _dot_nt(a, b):
    return lax.dot_general(a, b, (((1,), (1,)), ((), ())), preferred_element_type=F32)


def _dot_tn(a, b):
    return lax.dot_general(a, b, (((0,), (0,)), ((), ())), preferred_element_type=F32)


def _bf(x):
    return x.astype(BF16)


def _split(x):
    hi = x.astype(BF16)
    lo = (x - hi.astype(F32)).astype(BF16)
    return hi, lo


def _sigmoid(x):
    return 1.0 / (1.0 + jnp.exp(-x))


def _rms(x, g):
    ms = jnp.mean(x * x, axis=-1, keepdims=True)
    return x * lax.rsqrt(ms + RMS_EPS) * g


def _rmsnorm_kernel(x_ref, g_ref, o_ref):
    o_ref[...] = _rms(x_ref[...], g_ref[...])


def rmsnorm(x, g, tm):
    m, d = x.shape
    return pl.pallas_call(
        _rmsnorm_kernel,
        grid=(m // tm,),
        in_specs=[pl.BlockSpec((tm, d), lambda i: (i, 0)), pl.BlockSpec((1, d), lambda i: (0, 0))],
        out_specs=pl.BlockSpec((tm, d), lambda i: (i, 0)),
        out_shape=jax.ShapeDtypeStruct((m, d), F32),
        compiler_params=_cparams("parallel"),
        name="rmsnorm",
    )(x, g.reshape(1, d))


NORM_ROWS = 256


def _norm_to_scratch(x_ref, g, hb_ref, tm, dst_off=0):
    for r in range(0, tm, NORM_ROWS):
        n = min(NORM_ROWS, tm - r)
        hb_ref[dst_off + r:dst_off + r + n, :] = _bf(_rms(x_ref[r:r + n, :], g))


def _norm_matmul_kernel(x_ref, g_ref, w_ref, o_ref, hb_ref, *, tm):
    @pl.when(pl.program_id(1) == 0)
    def _():
        _norm_to_scratch(x_ref, g_ref[...], hb_ref, tm)

    o_ref[...] = _dot(hb_ref[...], w_ref[...])


def norm_matmul(x, g, w, tm, tn):
    m, d = x.shape
    n = w.shape[1]
    return pl.pallas_call(
        functools.partial(_norm_matmul_kernel, tm=tm),
        grid=(m // tm, n // tn),
        in_specs=[pl.BlockSpec((tm, d), lambda i, j: (i, 0)),
                  pl.BlockSpec((1, d), lambda i, j: (0, 0)),
                  pl.BlockSpec((d, tn), lambda i, j: (0, j))],
        out_specs=pl.BlockSpec((tm, tn), lambda i, j: (i, j)),
        out_shape=jax.ShapeDtypeStruct((m, n), F32),
        scratch_shapes=[pltpu.VMEM((tm, d), BF16)],
        compiler_params=_cparams("parallel", "arbitrary"),
        name="norm_matmul",
    )(x, g.reshape(1, d), w)


def _mm_kernel(*refs, n_pairs, has_res):
    a = refs[:n_pairs]
    w = refs[n_pairs:2 * n_pairs]
    o_ref = refs[-1]
    acc = _dot(_bf(a[0][...]), w[0][...])
    for i in range(1, n_pairs):
        acc = acc + _dot(_bf(a[i][...]), w[i][...])
    if has_res:
        acc = refs[2 * n_pairs][...] + acc
    o_ref[...] = acc


def matmul(a_list, w_list, res, n_out, tm, tn):
    m = a_list[0].shape[0]
    n_pairs = len(a_list)
    in_specs = [pl.BlockSpec((tm, a.shape[1]), lambda i, j: (i, 0)) for a in a_list]
    in_specs += [pl.BlockSpec((w.shape[0], tn), lambda i, j: (0, j)) for w in w_list]
    args = list(a_list) + list(w_list)
    if res is not None:
        in_specs.append(pl.BlockSpec((tm, tn), lambda i, j: (i, j)))
        args.append(res)
    return pl.pallas_call(
        functools.partial(_mm_kernel, n_pairs=n_pairs, has_res=res is not None),
        grid=(m // tm, n_out // tn),
        in_specs=in_specs,
        out_specs=pl.BlockSpec((tm, tn), lambda i, j: (i, j)),
        out_shape=jax.ShapeDtypeStruct((m, n_out), F32),
        compiler_params=_cparams("parallel", "parallel"),
        name="matmul",
    )(*args)


def _attn_kernel(q_ref, k_ref, v_ref, o_ref, *, n_seq, tq, cache_layout):
    scale = MEM_HEAD_DIM ** -0.5
    rows = []
    for s in range(n_seq):
        heads = []
        for h in range(MEM_HEADS):
            sl = slice(h * MEM_HEAD_DIM, (h + 1) * MEM_HEAD_DIM)
            q = _bf(q_ref[s * tq:(s + 1) * tq, sl])
            if cache_layout:
                k = _bf(k_ref[0, s, :, h, :])
                v = _bf(v_ref[0, s, :, h, :])
            else:
                k = _bf(k_ref[s * N_MEM:(s + 1) * N_MEM, sl])
                v = _bf(v_ref[s * N_MEM:(s + 1) * N_MEM, sl])
            sc = _dot_nt(q, k) * scale
            e = jnp.exp(sc - jnp.max(sc, axis=-1, keepdims=True))
            p = e / jnp.sum(e, axis=-1, keepdims=True)
            heads.append(_dot(_bf(p), v))
        rows.append(jnp.concatenate(heads, axis=1))
    o_ref[...] = _bf(jnp.concatenate(rows, axis=0) if n_seq > 1 else rows[0])


def mem_attention(p, q_colblock, kv, n_batch, seq, tq):
    m = p.shape[0]
    q_tiles = seq // tq
    return pl.pallas_call(
        functools.partial(_attn_kernel, n_seq=1, tq=tq, cache_layout=False),
        grid=(n_batch, q_tiles),
        in_specs=[pl.BlockSpec((tq, MEM_WIDTH), lambda b, t: (b * q_tiles + t, q_colblock)),
                  pl.BlockSpec((N_MEM, MEM_WIDTH), lambda b, t: (b, 0)),
                  pl.BlockSpec((N_MEM, MEM_WIDTH), lambda b, t: (b, 1))],
        out_specs=pl.BlockSpec((tq, MEM_WIDTH), lambda b, t: (b * q_tiles + t, 0)),
        out_shape=jax.ShapeDtypeStruct((m, MEM_WIDTH), BF16),
        compiler_params=_cparams("parallel", "parallel"),
        name="mem_attention",
    )(p, kv, kv)


def mem_attention_cache(p, q_colblock, cache_k, cache_v, layer, seq, n_seq):
    m = p.shape[0]
    n_batch = cache_k.shape[1]
    kv_spec = pl.BlockSpec((1, n_seq, N_MEM, MEM_HEADS, MEM_HEAD_DIM), lambda b: (layer, b, 0, 0, 0))
    return pl.pallas_call(
        functools.partial(_attn_kernel, n_seq=n_seq, tq=seq, cache_layout=True),
        grid=(n_batch // n_seq,),
        in_specs=[pl.BlockSpec((n_seq * seq, MEM_WIDTH), lambda b: (b, q_colblock)), kv_spec, kv_spec],
        out_specs=pl.BlockSpec((n_seq * seq, MEM_WIDTH), lambda b: (b, 0)),
        out_shape=jax.ShapeDtypeStruct((m, MEM_WIDTH), BF16),
        compiler_params=_cparams("parallel"),
        name="mem_attention_cache",
    )(p, cache_k, cache_v)


FFN_HALO = 16


def _gelu_gate(c, v):
    return _bf(jax.nn.gelu(c) * v)


def _ffn_up_prompt_kernel(x_ref, xh_ref, g_ref, wa_ref, wv_ref, cw_ref, cb_ref, u_ref, cs_ref, hb_ref,
                          *, tm, tiles_per_seq):
    i = pl.program_id(0)

    @pl.when(pl.program_id(1) == 0)
    def _():
        g = g_ref[...]
        hb_ref[0:FFN_HALO, :] = _bf(_rms(xh_ref[...], g))
        _norm_to_scratch(x_ref, g, hb_ref, tm, dst_off=FFN_HALO)

    a_ext = _dot(hb_ref[...], wa_ref[...])
    v = _dot(hb_ref[FFN_HALO:, :], wv_ref[...])
    rows = lax.broadcasted_iota(jnp.int32, (FFN_HALO + tm, 1), 0)
    n_zero = jnp.where((i % tiles_per_seq) == 0, FFN_HALO, 0)
    a_ext = jnp.where(rows < n_zero, 0.0, a_ext)
    a0 = a_ext[FFN_HALO:]
    a1 = pltpu.roll(a_ext, 1, 0)[FFN_HALO:]
    a2 = pltpu.roll(a_ext, 2, 0)[FFN_HALO:]
    c = cb_ref[...] + a2 * cw_ref[0:1, :] + a1 * cw_ref[1:2, :] + a0 * cw_ref[2:3, :]
    u_ref[...] = _gelu_gate(c, v)
    cs_ref[0] = a0[tm - 8:tm][6:8]


def ffn_up_prompt(x, g, w_up, cw, cb, n_batch, seq, tm, tn):
    m, d = x.shape
    nf = D_FF // tn
    tiles_per_seq = seq // tm
    halo_blocks = tm // FFN_HALO
    u, cs = pl.pallas_call(
        functools.partial(_ffn_up_prompt_kernel, tm=tm, tiles_per_seq=tiles_per_seq),
        grid=(m // tm, nf),
        in_specs=[pl.BlockSpec((tm, d), lambda i, j: (i, 0)),
                  pl.BlockSpec((FFN_HALO, d), lambda i, j: (jnp.maximum(i * halo_blocks - 1, 0), 0)),
                  pl.BlockSpec((1, d), lambda i, j: (0, 0)),
                  pl.BlockSpec((d, tn), lambda i, j: (0, j)),
                  pl.BlockSpec((d, tn), lambda i, j: (0, nf + j)),
                  pl.BlockSpec((3, tn), lambda i, j: (0, j)),
                  pl.BlockSpec((1, tn), lambda i, j: (0, j))],
        out_specs=[pl.BlockSpec((tm, tn), lambda i, j: (i, j)),
                   pl.BlockSpec((1, 2, tn), lambda i, j: (i, 0, j))],
        out_shape=[jax.ShapeDtypeStruct((m, D_FF), BF16),
                   jax.ShapeDtypeStruct((m // tm, 2, D_FF), F32)],
        scratch_shapes=[pltpu.VMEM((FFN_HALO + tm, d), BF16)],
        compiler_params=_cparams("parallel", "arbitrary"),
        name="ffn_up_prompt",
    )(x, x, g.reshape(1, d), w_up, w_up, cw, cb.reshape(1, D_FF))
    return u, cs.reshape(n_batch, tiles_per_seq, 2, D_FF)[:, -1]


def _ffn_up_sample_kernel(x_ref, g_ref, wa_ref, wv_ref, cw_ref, cb_ref, p1_ref, p2_ref,
                          u_ref, c6_ref, c7_ref, hb_ref, a_ref, *, tm, seq):
    @pl.when(pl.program_id(1) == 0)
    def _():
        _norm_to_scratch(x_ref, g_ref[...], hb_ref, tm)

    a = _dot(hb_ref[...], wa_ref[...])
    v = _dot(hb_ref[...], wv_ref[...])
    t = lax.broadcasted_iota(jnp.int32, (tm, 1), 0) % seq
    a1 = jnp.where(t >= 1, pltpu.roll(a, 1, 0), p1_ref[...])
    a2 = jnp.where(t >= 2, pltpu.roll(a, 2, 0), p2_ref[...])
    c = cb_ref[...] + a2 * cw_ref[0:1, :] + a1 * cw_ref[1:2, :] + a * cw_ref[2:3, :]
    u_ref[...] = _gelu_gate(c, v)
    for q in range(a.shape[1] // LANES):
        sl = slice(q * LANES, (q + 1) * LANES)
        a_ref[q] = a[:, sl]
        c6_ref[:, sl] = a_ref[q, pl.ds(seq - 2, tm // seq, stride=seq), :]
        c7_ref[:, sl] = a_ref[q, pl.ds(seq - 1, tm // seq, stride=seq), :]


def ffn_up_sample(x, g, w_up, cw, cb, p1, p2, seq, tm, tn):
    m, d = x.shape
    nf = D_FF // tn
    nb = tm // seq
    return pl.pallas_call(
        functools.partial(_ffn_up_sample_kernel, tm=tm, seq=seq),
        grid=(m // tm, nf),
        in_specs=[pl.BlockSpec((tm, d), lambda i, j: (i, 0)),
                  pl.BlockSpec((1, d), lambda i, j: (0, 0)),
                  pl.BlockSpec((d, tn), lambda i, j: (0, j)),
                  pl.BlockSpec((d, tn), lambda i, j: (0, nf + j)),
                  pl.BlockSpec((3, tn), lambda i, j: (0, j)),
                  pl.BlockSpec((1, tn), lambda i, j: (0, j)),
                  pl.BlockSpec((tm, tn), lambda i, j: (i, j)),
                  pl.BlockSpec((tm, tn), lambda i, j: (i, j))],
        out_specs=[pl.BlockSpec((tm, tn), lambda i, j: (i, j)),
                   pl.BlockSpec((nb, tn), lambda i, j: (i, j)),
                   pl.BlockSpec((nb, tn), lambda i, j: (i, j))],
        out_shape=[jax.ShapeDtypeStruct((m, D_FF), BF16),
                   jax.ShapeDtypeStruct((m // seq, D_FF), F32),
                   jax.ShapeDtypeStruct((m // seq, D_FF), F32)],
        scratch_shapes=[pltpu.VMEM((tm, d), BF16), pltpu.VMEM((tn // LANES, tm, LANES), F32)],
        compiler_params=_cparams("parallel", "arbitrary"),
        name="ffn_up_sample",
    )(x, g.reshape(1, d), w_up, w_up, cw, cb.reshape(1, D_FF), p1, p2)


def _seg_sum(x, e):
    parts = []
    for j in range(x.shape[1] // LANES):
        hi, lo = _split(x[:, j * LANES:(j + 1) * LANES])
        parts.append(_dot(hi, e) + _dot(lo, e))
    return jnp.concatenate(parts, axis=1)


def _unit_lower_solve(a_list, rhs_list, c):
    mm = lambda x, y: _dot(_bf(x), _bf(y))
    n = range(len(a_list))
    rows = lax.broadcasted_iota(jnp.int32, (c, c), 0)
    cols = lax.broadcasted_iota(jnp.int32, (c, c), 1)
    eye = (rows == cols).astype(F32)
    blk = min(c, 16)
    if c > blk:
        assert c // blk <= 4
        same = (rows // blk) == (cols // blk)
        ad = [jnp.where(same, a, 0.0) for a in a_list]
        ao = [a_list[i] - ad[i] for i in n]
    else:
        ad = a_list
    t = [eye - ad[i] for i in n]
    pw = ad
    span = 2
    while span < blk:
        pw = [mm(pw[i], pw[i]) for i in n]
        t = [t[i] + mm(t[i], pw[i]) for i in n]
        span *= 2
    x = [mm(t[i], rhs_list[i]) for i in n]
    if c > blk:
        nn = [mm(t[i], ao[i]) for i in n]
        n2 = [mm(nn[i], nn[i]) for i in n]
        x = [x[i] + mm(n2[i], x[i]) for i in n]
        x = [x[i] - mm(nn[i], x[i]) for i in n]
    return x


def _rwkv_kernel(p_ref, ps_ref, s0_ref, mu_ref, vec_ref, w2_ref, a2_ref, g2_ref, e_ref, lt_ref,
                 tok_ref, sout_ref, prev_ref, s_ref, y_ref, *, c):
    ci = pl.program_id(1)
    tw = TOK_WIDTH

    @pl.when(ci == 0)
    def _():
        s_ref[...] = s0_ref[0]
        prev_ref[...] = ps_ref[0]

    p = p_ref[...]
    rows = lax.broadcasted_iota(jnp.int32, (c, 1), 0)
    p_prev = jnp.where(rows == 0, prev_ref[...], pltpu.roll(p, 1, 0))
    prev_ref[...] = p_ref[c - 1:c, :]
    xm = p + mu_ref[...] * (p_prev - p)

    r = xm[:, 0:tw]
    k = xm[:, tw:2 * tw]
    v = xm[:, 2 * tw:3 * tw]
    xw = xm[:, 3 * tw:3 * tw + A_LORA_PAD]
    xa = xm[:, 3 * tw + A_LORA_PAD:3 * tw + 2 * A_LORA_PAD]
    xg = xm[:, 3 * tw + 2 * A_LORA_PAD:]
    w0, a0, k_k, k_a, r_k, ln_w, ln_b = (vec_ref[i:i + 1, :] for i in range(7))

    z = -(w0 + _dot(_bf(jnp.tanh(xw)), w2_ref[...]))
    softplus = jnp.maximum(z, 0.0) + jnp.log(1.0 + jnp.exp(-jnp.abs(z)))
    ell = -jnp.exp(-softplus - 0.5)
    a = _sigmoid(a0 + _dot(_bf(xa), a2_ref[...]))
    gate = _dot(_bf(_sigmoid(xg)), g2_ref[...])
    kkraw = k * k_k
    k2 = k * (1.0 + (a - 1.0) * k_a)
    e = e_ref[...]
    kap = kkraw * lax.rsqrt(jnp.maximum(_seg_sum(kkraw * kkraw, e), 1e-24))
    bet = kap * a

    ell_hi, ell_lo = _split(ell)
    gc = _dot(lt_ref[...], ell_hi) + _dot(lt_ref[...], ell_lo)
    glast = jnp.sum(ell, axis=0, keepdims=True)
    egi = jnp.exp(-gc)
    el = jnp.exp(glast - gc)
    xk = kap * jnp.exp(gc - ell)
    xr = r * jnp.exp(gc)
    kb = k2 * egi
    bb = bet * egi
    kh = k2 * el
    bh = bet * el
    gam = jnp.exp(glast)
    ti = lax.broadcasted_iota(jnp.int32, (c, c), 0)
    si = lax.broadcasted_iota(jnp.int32, (c, c), 1)
    strict = si < ti
    incl = si <= ti

    hs = range(A_HEADS)
    sl = [slice(h * A_HEAD_DIM, (h + 1) * A_HEAD_DIM) for h in hs]
    s_old = [s_ref[h] for h in hs]
    sb = [_bf(s) for s in s_old]
    xk_h = [_bf(xk[:, sl[h]]) for h in hs]
    xr_h = [_bf(xr[:, sl[h]]) for h in hs]
    kb_h = [_bf(kb[:, sl[h]]) for h in hs]
    bb_h = [_bf(bb[:, sl[h]]) for h in hs]
    v_h = [_bf(v[:, sl[h]]) for h in hs]
    a_kk = [jnp.where(strict, _dot_nt(xk_h[h], kb_h[h]), 0.0) for h in hs]
    a_kb = [jnp.where(strict, _dot_nt(xk_h[h], bb_h[h]), 0.0) for h in hs]
    a_rk = [jnp.where(incl, _dot_nt(xr_h[h], kb_h[h]), 0.0) for h in hs]
    a_rb = [jnp.where(incl, _dot_nt(xr_h[h], bb_h[h]), 0.0) for h in hs]
    rhs = [-(_dot_nt(xk_h[h], sb[h]) + _dot(_bf(a_kk[h]), v_h[h])) for h in hs]
    u_h = [_bf(u) for u in _unit_lower_solve(a_kb, rhs, c)]
    y_h = [_dot_nt(xr_h[h], sb[h]) + _dot(_bf(a_rk[h]), v_h[h]) + _dot(_bf(a_rb[h]), u_h[h]) for h in hs]
    s_new = [s_old[h] * gam[:, sl[h]] + _dot_tn(v_h[h], _bf(kh[:, sl[h]])) + _dot_tn(u_h[h], _bf(bh[:, sl[h]]))
             for h in hs]
    for h in hs:
        y_ref[:, sl[h]] = y_h[h]
        s_ref[h] = s_new[h]

    y = y_ref[...]
    inv_n = 1.0 / A_HEAD_DIM
    mean = _seg_sum(y, e) * inv_n
    d = y - mean
    var = _seg_sum(d * d, e) * inv_n
    yn = d * lax.rsqrt(var + GN_EPS) * ln_w + ln_b
    bonus = _seg_sum(r * k2 * r_k, e) * v
    tok_ref[...] = ((yn + bonus) * gate).astype(tok_ref.dtype)

    @pl.when(ci == pl.num_programs(1) - 1)
    def _():
        sout_ref[0] = s_ref[...]


def rwkv_mix(p, pshift, s0, mu, vecs, w2, a2, g2, n_batch, seq, c, out_dtype):
    m = p.shape[0]
    n_chunks = seq // c
    lane = jnp.arange(LANES)
    e = (lane[:, None] // A_HEAD_DIM == lane[None, :] // A_HEAD_DIM).astype(BF16)
    t = jnp.arange(c)
    ltri = (t[None, :] <= t[:, None]).astype(BF16)
    const = lambda shape: pl.BlockSpec(shape, lambda b, ci: (0,) * len(shape))
    return pl.pallas_call(
        functools.partial(_rwkv_kernel, c=c),
        grid=(n_batch, n_chunks),
        in_specs=[pl.BlockSpec((c, A_TOK_COLS), lambda b, ci: (b * n_chunks + ci, 0)),
                  pl.BlockSpec((1, 1, A_TOK_COLS), lambda b, ci: (b, 0, 0)),
                  pl.BlockSpec((1, A_HEADS, A_HEAD_DIM, A_HEAD_DIM), lambda b, ci: (b, 0, 0, 0)),
                  const((1, A_TOK_COLS)), const((8, TOK_WIDTH)),
                  const((A_LORA_PAD, TOK_WIDTH)), const((A_LORA_PAD, TOK_WIDTH)),
                  const((A_GATE_RANK, TOK_WIDTH)), const((LANES, LANES)), const((c, c))],
        out_specs=[pl.BlockSpec((c, TOK_WIDTH), lambda b, ci: (b * n_chunks + ci, 0)),
                   pl.BlockSpec((1, A_HEADS, A_HEAD_DIM, A_HEAD_DIM), lambda b, ci: (b, 0, 0, 0))],
        out_shape=[jax.ShapeDtypeStruct((m, TOK_WIDTH), out_dtype),
                   jax.ShapeDtypeStruct((n_batch, A_HEADS, A_HEAD_DIM, A_HEAD_DIM), F32)],
        scratch_shapes=[pltpu.VMEM((1, A_TOK_COLS), F32),
                        pltpu.VMEM((A_HEADS, A_HEAD_DIM, A_HEAD_DIM), F32),
                        pltpu.VMEM((c, TOK_WIDTH), F32)],
        compiler_params=_cparams("parallel", "arbitrary"),
        name="rwkv_mix",
    )(p, pshift, s0, mu, vecs, w2, a2, g2, e, ltri)


def _hgrn_kernel(p_ref, s0_ref, lbp_ref, gn_ref, lt_ref, tok_ref, sout_ref, st_ref, *, c, layer):
    ci = pl.program_id(1)
    tw = TOK_WIDTH
    hd = B_HEAD_DIM

    @pl.when(ci == 0)
    def _():
        for h in range(B_HEADS):
            st_ref[h] = s0_ref[0, h].T

    lbp = lbp_ref[...]
    mx = jnp.max(lbp, axis=0, keepdims=True)
    ex = jnp.exp(lbp - mx)
    den = jnp.sum(ex, axis=0, keepdims=True)
    lb = jnp.zeros((1, tw), F32)
    for i in range(1, layer + 1):
        lb = lb + ex[i:i + 1, :] / den

    q = p_ref[:, 0:tw]
    f = p_ref[:, tw:2 * tw]
    iv = p_ref[:, 2 * tw:3 * tw]
    og = p_ref[:, 3 * tw:4 * tw]
    fg = lb + (1.0 - lb) * _sigmoid(f)
    lf = jnp.log(fg)
    kk = 1.0 - fg
    qq = q * _sigmoid(q)
    lf_hi, lf_lo = _split(lf)
    bc = _dot(lt_ref[...], lf_hi) + _dot(lt_ref[...], lf_lo)
    blast = jnp.sum(lf, axis=0, keepdims=True)
    rows = lax.broadcasted_iota(jnp.int32, (c, 1), 0)
    mid = jnp.sum(jnp.where(rows == (c - 1) // 2, bc, 0.0), axis=0, keepdims=True)
    qe = qq * jnp.exp(bc - mid)
    ke = kk * jnp.exp(mid - bc)
    qs = qq * jnp.exp(bc)
    kl = kk * jnp.exp(blast - bc)
    gam = jnp.exp(blast)
    gate = og * _sigmoid(og)
    ti = lax.broadcasted_iota(jnp.int32, (c, c), 0)
    si = lax.broadcasted_iota(jnp.int32, (c, c), 1)
    incl = si <= ti
    gn = gn_ref[...]
    for h in range(B_HEADS):
        sl = slice(h * hd, (h + 1) * hd)
        st = st_ref[h]
        v_h = _bf(iv[:, sl])
        att = jnp.where(incl, _dot_nt(_bf(qe[:, sl]), _bf(ke[:, sl])), 0.0)
        o = _dot(_bf(att), v_h) + _dot_nt(_bf(qs[:, sl]), _bf(st))
        st_ref[h] = st * gam[:, sl] + _dot_tn(v_h, _bf(kl[:, sl]))
        on = o * lax.rsqrt(jnp.mean(o * o, axis=-1, keepdims=True) + RMS_EPS) * gn
        tok_ref[:, sl] = (on * gate[:, sl]).astype(tok_ref.dtype)

    @pl.when(ci == pl.num_programs(1) - 1)
    def _():
        for h in range(B_HEADS):
            sout_ref[0, h] = st_ref[h].T


def hgrn_mix(p, s0, lbp, gn, n_batch, seq, c, layer, out_dtype):
    m = p.shape[0]
    n_chunks = seq // c
    t = jnp.arange(c)
    ltri = (t[None, :] <= t[:, None]).astype(BF16)
    const = lambda shape: pl.BlockSpec(shape, lambda b, ci: (0,) * len(shape))
    st_spec = pl.BlockSpec((1, B_HEADS, B_HEAD_DIM, B_HEAD_DIM), lambda b, ci: (b, 0, 0, 0))
    return pl.pallas_call(
        functools.partial(_hgrn_kernel, c=c, layer=layer),
        grid=(n_batch, n_chunks),
        in_specs=[pl.BlockSpec((c, B_TOK_COLS), lambda b, ci: (b * n_chunks + ci, 0)),
                  st_spec, const((DEPTH, TOK_WIDTH)), const((1, B_HEAD_DIM)), const((c, c))],
        out_specs=[pl.BlockSpec((c, TOK_WIDTH), lambda b, ci: (b * n_chunks + ci, 0)), st_spec],
        out_shape=[jax.ShapeDtypeStruct((m, TOK_WIDTH), out_dtype),
                   jax.ShapeDtypeStruct((n_batch, B_HEADS, B_HEAD_DIM, B_HEAD_DIM), F32)],
        scratch_shapes=[pltpu.VMEM((B_HEADS, B_HEAD_DIM, B_HEAD_DIM), F32)],
        compiler_params=_cparams("parallel", "arbitrary"),
        name="hgrn_mix",
    )(p, s0, lbp, gn.reshape(1, B_HEAD_DIM), ltri)


PROMPT_CHUNK = 64


def _trunk(x, n_batch, seq, mem, st_rwkv, pshift, st_hgrn, conv_hist, w, is_prompt):
    m = x.shape[0]
    c = PROMPT_CHUNK if is_prompt else seq
    tm_big = 1024
    tok_dtype = BF16 if is_prompt else F32
    new_conv = []

    def attend(p, q_colblock, layer):
        if is_prompt:
            return mem_attention(p, q_colblock, mem[layer], n_batch, seq, 512)
        return mem_attention_cache(p, q_colblock, mem[0], mem[1], layer, seq, 8)

    def ffn(x, layer):
        if is_prompt:
            u, cs = ffn_up_prompt(x, w["norm_ffn"][layer], w["ffn_w_up"][layer], w["ffn_conv_w"][layer],
                                  w["ffn_conv_b"][layer], n_batch, seq, 1024, 512)
        else:
            p1, p2 = conv_hist[layer]
            u, c6, c7 = ffn_up_sample(x, w["norm_ffn"][layer], w["ffn_w_up"][layer], w["ffn_conv_w"][layer],
                                      w["ffn_conv_b"][layer], p1, p2, seq, 512, 512)
            cs = jnp.stack([c6, c7], axis=1)
        new_conv.append(cs)
        return matmul([u], [w["ffn_w_down"][layer]], x, D_MODEL, 1024, 512)

    p = norm_matmul(x, w["norm_mix"][0], w["a_w_in"], tm_big, 512)
    tok, s_rwkv = rwkv_mix(p, pshift, st_rwkv, w["a_mu"], w["a_vecs"], w["a_w2"], w["a_a2"], w["a_g2"],
                           n_batch, seq, c, tok_dtype)
    mem_o = attend(p, A_TOK_COLS // MEM_WIDTH, 0)
    x = matmul([tok, mem_o], [w["a_w_out_tok"], w["a_w_out_mem"]], x, D_MODEL, tm_big, 512)
    x = ffn(x, 0)
    p = norm_matmul(x, w["norm_mix"][1], w["b_w_in"], tm_big, 512)
    tok, s_hgrn = hgrn_mix(p, st_hgrn, w["b_lower_bounds"], w["b_g_norm"], n_batch, seq, c, 1, tok_dtype)
    mem_o = attend(p, B_TOK_COLS // MEM_WIDTH, 1)
    x = matmul([tok, mem_o], [w["b_w_out_tok"], w["b_w_out_mem"]], x, D_MODEL, tm_big, 512)
    x = ffn(x, 1)
    y = rmsnorm(x, w["norm_final"], 256)
    return y, s_rwkv, s_hgrn, jnp.stack(new_conv)


def kernel(x_prompt, x_sample, mem_prompt, cache_mem_k, cache_mem_v, state_rwkv, state_shift, state_hgrn, state_conv, norm_mix, norm_ffn, norm_final, mem_norm, w_mem_kv, a_w_in, a_mu, a_w0, a_w2, a_a0, a_a2, a_g2, a_k_k, a_k_a, a_r_k, a_ln_w, a_ln_b, a_w_out, b_w_in, b_lower_bounds, b_g_norm, b_w_out, ffn_w_up, ffn_conv_w, ffn_conv_b, ffn_w_down):
    bp, sp, d = x_prompt.shape
    bs, ss, _ = x_sample.shape

    wa = a_w_in[0]
    rkv_end = MEM_WIDTH + 3 * TOK_WIDTH
    zc = jnp.zeros((d, A_LORA_PAD - 96), F32)
    wa_p = _bf(jnp.concatenate([wa[:, MEM_WIDTH:rkv_end], wa[:, rkv_end:rkv_end + 96], zc,
                                wa[:, rkv_end + 96:rkv_end + 192], zc, wa[:, rkv_end + 192:],
                                wa[:, :MEM_WIDTH]], axis=1))
    mu = a_mu[0]
    zv = jnp.zeros((A_LORA_PAD - 96,), F32)
    mu_p = jnp.concatenate([mu[:3 * TOK_WIDTH + 96], zv, mu[3 * TOK_WIDTH + 96:3 * TOK_WIDTH + 192], zv,
                            mu[3 * TOK_WIDTH + 192:]]).reshape(1, A_TOK_COLS)
    zr = jnp.zeros((A_LORA_PAD - 96, TOK_WIDTH), F32)
    wb = b_w_in[0]
    w = dict(
        norm_mix=norm_mix, norm_ffn=norm_ffn, norm_final=norm_final,
        a_w_in=wa_p, a_mu=mu_p,
        a_vecs=jnp.stack([a_w0[0], a_a0[0], a_k_k[0], a_k_a[0], a_r_k[0], a_ln_w[0], a_ln_b[0],
                          jnp.zeros((TOK_WIDTH,), F32)]),
        a_w2=_bf(jnp.concatenate([a_w2[0], zr], axis=0)),
        a_a2=_bf(jnp.concatenate([a_a2[0], zr], axis=0)),
        a_g2=_bf(a_g2[0]),
        a_w_out_tok=_bf(a_w_out[0][:TOK_WIDTH]), a_w_out_mem=_bf(a_w_out[0][TOK_WIDTH:]),
        b_w_in=_bf(jnp.concatenate([wb[:, MEM_WIDTH:], wb[:, :MEM_WIDTH]], axis=1)),
        b_lower_bounds=b_lower_bounds, b_g_norm=b_g_norm[0],
        b_w_out_tok=_bf(b_w_out[0][:TOK_WIDTH]), b_w_out_mem=_bf(b_w_out[0][TOK_WIDTH:]),
        ffn_w_up=_bf(ffn_w_up), ffn_conv_w=ffn_conv_w, ffn_conv_b=ffn_conv_b, ffn_w_down=_bf(ffn_w_down),
    )

    mem_rows = mem_prompt.reshape(bp * N_MEM, d)
    mem_kv = [norm_matmul(mem_rows, mem_norm[l], _bf(w_mem_kv[l]), bp * N_MEM, 512) for l in range(DEPTH)]
    mem_k_prompt = jnp.stack([kv[:, :MEM_WIDTH] for kv in mem_kv]).reshape(DEPTH, bp, N_MEM, MEM_HEADS, MEM_HEAD_DIM)
    mem_v_prompt = jnp.stack([kv[:, MEM_WIDTH:] for kv in mem_kv]).reshape(DEPTH, bp, N_MEM, MEM_HEADS, MEM_HEAD_DIM)
    xp = x_prompt.reshape(bp * sp, d)
    y_p, rwkv_p, hgrn_p, conv_p = _trunk(
        xp, bp, sp, mem_kv,
        jnp.zeros((bp, A_HEADS, A_HEAD_DIM, A_HEAD_DIM), F32), jnp.zeros((bp, 1, A_TOK_COLS), F32),
        jnp.zeros((bp, B_HEADS, B_HEAD_DIM, B_HEAD_DIM), F32), None, w, True)
    shift_p = rmsnorm(x_prompt[:, -1], norm_mix[0], bp)

    xs = x_sample.reshape(bs * ss, d)
    pshift = matmul([state_shift[0]], [wa_p], None, A_TOK_COLS, bs, 512).reshape(bs, 1, A_TOK_COLS)
    hist = []
    for l in range(DEPTH):
        h = state_conv[l]
        p1 = jnp.pad(h[:, 1:2], ((0, 0), (0, ss - 1), (0, 0))).reshape(bs * ss, D_FF)
        p2 = jnp.pad(h, ((0, 0), (0, ss - 2), (0, 0))).reshape(bs * ss, D_FF)
        hist.append((p1, p2))
    y_s, rwkv_s, hgrn_s, conv_s = _trunk(xs, bs, ss, (cache_mem_k, cache_mem_v), state_rwkv[0], pshift, state_hgrn[0], hist, w, False)
    shift_s = rmsnorm(x_sample[:, -1], norm_mix[0], bs)

    return (y_p.reshape(bp, sp, d), y_s.reshape(bs, ss, d), mem_k_prompt, mem_v_prompt,
            rwkv_p[None], rwkv_s[None], shift_p[None], shift_s[None], hgrn_p[None], hgrn_s[None],
            conv_p, conv_s)
```

```python
import functools

import jax
import jax.numpy as jnp
from jax import lax
from jax.experimental import pallas as pl
from jax.experimental.pallas import tpu as pltpu

F32 = jnp.float32
BF16 = jnp.bfloat16

D_MODEL = 2048
DEPTH = 2
TOK_WIDTH = 1536
MEM_WIDTH = 512
MEM_HEADS = 4
MEM_HEAD_DIM = 128
N_MEM = 256
A_HEAD_DIM = 64
A_HEADS = 24
A_LORA_PAD = 128
A_GATE_RANK = 256
A_TOK_COLS = 3 * TOK_WIDTH + 2 * A_LORA_PAD + A_GATE_RANK
B_HEADS = 12
B_HEAD_DIM = 128
B_TOK_COLS = 4 * TOK_WIDTH
D_FF = 5632
RMS_EPS = 1e-6
GN_EPS = 64e-5

LANES = 128
VMEM_LIMIT = 56 * 1024 * 1024


def _cparams(*sem):
    return pltpu.CompilerParams(dimension_semantics=sem, vmem_limit_bytes=VMEM_LIMIT)


def _dot(a, b):
    return jnp.dot(a, b, preferred_element_type=F32)


def _dot_nt(a, b):
    return lax.dot_general(a, b, (((1,), (1,)), ((), ())), preferred_element_type=F32)


def _dot_tn(a, b):
    return lax.dot_general(a, b, (((0,), (0,)), ((), ())), preferred_element_type=F32)


def _bf(x):
    return x.astype(BF16)


def _split(x):
    hi = x.astype(BF16)
    lo = (x - hi.astype(F32)).astype(BF16)
    return hi, lo


def _sigmoid(x):
    return 1.0 / (1.0 + jnp.exp(-x))


def _rms(x, g):
    ms = jnp.mean(x * x, axis=-1, keepdims=True)
    return x * lax.rsqrt(ms + RMS_EPS) * g


def _rmsnorm_kernel(x_ref, g_ref, o_ref):
    o_ref[...] = _rms(x_ref[...], g_ref[...])


def rmsnorm(x, g, tm):
    m, d = x.shape
    return pl.pallas_call(
        _rmsnorm_kernel,
        grid=(m // tm,),
        in_specs=[pl.BlockSpec((tm, d), lambda i: (i, 0)), pl.BlockSpec((1, d), lambda i: (0, 0))],
        out_specs=pl.BlockSpec((tm, d), lambda i: (i, 0)),
        out_shape=jax.ShapeDtypeStruct((m, d), F32),
        compiler_params=_cparams("parallel"),
        name="rmsnorm",
    )(x, g.reshape(1, d))


NORM_ROWS = 256


def _norm_to_scratch(x_ref, g, hb_ref, tm, dst_off=0):
    for r in range(0, tm, NORM_ROWS):
        n = min(NORM_ROWS, tm - r)
        hb_ref[dst_off + r:dst_off + r + n, :] = _bf(_rms(x_ref[r:r + n, :], g))


def _norm_matmul_kernel(x_ref, g_ref, w_ref, o_ref, hb_ref, *, tm):
    @pl.when(pl.program_id(1) == 0)
    def _():
        _norm_to_scratch(x_ref, g_ref[...], hb_ref, tm)

    o_ref[...] = _dot(hb_ref[...], w_ref[...])


def norm_matmul(x, g, w, tm, tn):
    m, d = x.shape
    n = w.shape[1]
    return pl.pallas_call(
        functools.partial(_norm_matmul_kernel, tm=tm),
        grid=(m // tm, n // tn),
        in_specs=[pl.BlockSpec((tm, d), lambda i, j: (i, 0)),
                  pl.BlockSpec((1, d), lambda i, j: (0, 0)),
                  pl.BlockSpec((d, tn), lambda i, j: (0, j))],
        out_specs=pl.BlockSpec((tm, tn), lambda i, j: (i, j)),
        out_shape=jax.ShapeDtypeStruct((m, n), F32),
        scratch_shapes=[pltpu.VMEM((tm, d), BF16)],
        compiler_params=_cparams("parallel", "arbitrary"),
        name="norm_matmul",
    )(x, g.reshape(1, d), w)


def _mm_kernel(*refs, n_pairs, has_res):
    a = refs[:n_pairs]
    w = refs[n_pairs:2 * n_pairs]
    o_ref = refs[-1]
    acc = _dot(_bf(a[0][...]), w[0][...])
    for i in range(1, n_pairs):
        acc = acc + _dot(_bf(a[i][...]), w[i][...])
    if has_res:
        acc = refs[2 * n_pairs][...] + acc
    o_ref[...] = acc


def matmul(a_list, w_list, res, n_out, tm, tn):
    m = a_list[0].shape[0]
    n_pairs = len(a_list)
    in_specs = [pl.BlockSpec((tm, a.shape[1]), lambda i, j: (i, 0)) for a in a_list]
    in_specs += [pl.BlockSpec((w.shape[0], tn), lambda i, j: (0, j)) for w in w_list]
    args = list(a_list) + list(w_list)
    if res is not None:
        in_specs.append(pl.BlockSpec((tm, tn), lambda i, j: (i, j)))
        args.append(res)
    return pl.pallas_call(
        functools.partial(_mm_kernel, n_pairs=n_pairs, has_res=res is not None),
        grid=(m // tm, n_out // tn),
        in_specs=in_specs,
        out_specs=pl.BlockSpec((tm, tn), lambda i, j: (i, j)),
        out_shape=jax.ShapeDtypeStruct((m, n_out), F32),
        compiler_params=_cparams("parallel", "parallel"),
        name="matmul",
    )(*args)


def _attn_kernel(q_ref, k_ref, v_ref, o_ref, *, n_seq, tq, cache_layout):
    scale = MEM_HEAD_DIM ** -0.5
    pairs = [(s, h) for s in range(n_seq) for h in range(MEM_HEADS)]
    hsl = [slice(h * MEM_HEAD_DIM, (h + 1) * MEM_HEAD_DIM) for h in range(MEM_HEADS)]
    q = [_bf(q_ref[s * tq:(s + 1) * tq, hsl[h]]) for s, h in pairs]
    if cache_layout:
        k = [_bf(k_ref[0, s, pl.ds(h, N_MEM, stride=MEM_HEADS), :]) for s, h in pairs]
        v = [_bf(v_ref[0, s, pl.ds(h, N_MEM, stride=MEM_HEADS), :]) for s, h in pairs]
    else:
        k = [_bf(k_ref[s * N_MEM:(s + 1) * N_MEM, hsl[h]]) for s, h in pairs]
        v = [_bf(v_ref[s * N_MEM:(s + 1) * N_MEM, hsl[h]]) for s, h in pairs]
    sc = [_dot_nt(q[i], k[i]) * scale for i in range(len(pairs))]
    e = [jnp.exp(x - jnp.max(x, axis=-1, keepdims=True)) for x in sc]
    p = [_bf(x / jnp.sum(x, axis=-1, keepdims=True)) for x in e]
    o = [_dot(p[i], v[i]) for i in range(len(pairs))]
    rows = [jnp.concatenate(o[s * MEM_HEADS:(s + 1) * MEM_HEADS], axis=1) for s in range(n_seq)]
    o_ref[...] = _bf(jnp.concatenate(rows, axis=0) if n_seq > 1 else rows[0])


def mem_attention(p, q_colblock, kv, n_batch, seq, tq):
    m = p.shape[0]
    q_tiles = seq // tq
    return pl.pallas_call(
        functools.partial(_attn_kernel, n_seq=1, tq=tq, cache_layout=False),
        grid=(n_batch, q_tiles),
        in_specs=[pl.BlockSpec((tq, MEM_WIDTH), lambda b, t: (b * q_tiles + t, q_colblock)),
                  pl.BlockSpec((N_MEM, MEM_WIDTH), lambda b, t: (b, 0)),
                  pl.BlockSpec((N_MEM, MEM_WIDTH), lambda b, t: (b, 1))],
        out_specs=pl.BlockSpec((tq, MEM_WIDTH), lambda b, t: (b * q_tiles + t, 0)),
        out_shape=jax.ShapeDtypeStruct((m, MEM_WIDTH), BF16),
        compiler_params=_cparams("parallel", "parallel"),
        name="mem_attention",
    )(p, kv, kv)


def mem_attention_cache(p, q_colblock, cache_k, cache_v, layer, seq, n_seq):
    m = p.shape[0]
    depth, n_batch = cache_k.shape[:2]
    cache_k = cache_k.reshape(depth, n_batch, N_MEM * MEM_HEADS, MEM_HEAD_DIM)
    cache_v = cache_v.reshape(depth, n_batch, N_MEM * MEM_HEADS, MEM_HEAD_DIM)
    kv_spec = pl.BlockSpec((1, n_seq, N_MEM * MEM_HEADS, MEM_HEAD_DIM), lambda b: (layer, b, 0, 0))
    return pl.pallas_call(
        functools.partial(_attn_kernel, n_seq=n_seq, tq=seq, cache_layout=True),
        grid=(n_batch // n_seq,),
        in_specs=[pl.BlockSpec((n_seq * seq, MEM_WIDTH), lambda b: (b, q_colblock)), kv_spec, kv_spec],
        out_specs=pl.BlockSpec((n_seq * seq, MEM_WIDTH), lambda b: (b, 0)),
        out_shape=jax.ShapeDtypeStruct((m, MEM_WIDTH), BF16),
        compiler_params=_cparams("parallel"),
        name="mem_attention_cache",
    )(p, cache_k, cache_v)


FFN_HALO = 16


def _gelu_gate(c, v):
    return _bf(jax.nn.gelu(c) * v)


def _ffn_up_prompt_kernel(x_ref, xh_ref, g_ref, wa_ref, wv_ref, cw_ref, cb_ref, u_ref, cs_ref, hb_ref,
                          *, tm, tiles_per_seq):
    i = pl.program_id(0)

    @pl.when(pl.program_id(1) == 0)
    def _():
        g = g_ref[...]
        hb_ref[0:FFN_HALO, :] = _bf(_rms(xh_ref[...], g))
        _norm_to_scratch(x_ref, g, hb_ref, tm, dst_off=FFN_HALO)

    a_ext = _dot(hb_ref[...], wa_ref[...])
    v = _dot(hb_ref[FFN_HALO:, :], wv_ref[...])
    rows = lax.broadcasted_iota(jnp.int32, (FFN_HALO + tm, 1), 0)
    n_zero = jnp.where((i % tiles_per_seq) == 0, FFN_HALO, 0)
    a_ext = jnp.where(rows < n_zero, 0.0, a_ext)
    a0 = a_ext[FFN_HALO:]
    a1 = pltpu.roll(a_ext, 1, 0)[FFN_HALO:]
    a2 = pltpu.roll(a_ext, 2, 0)[FFN_HALO:]
    c = cb_ref[...] + a2 * cw_ref[0:1, :] + a1 * cw_ref[1:2, :] + a0 * cw_ref[2:3, :]
    u_ref[...] = _gelu_gate(c, v)
    cs_ref[0] = a0[tm - 8:tm][6:8]


def ffn_up_prompt(x, g, w_up, cw, cb, n_batch, seq, tm, tn):
    m, d = x.shape
    nf = D_FF // tn
    tiles_per_seq = seq // tm
    halo_blocks = tm // FFN_HALO
    u, cs = pl.pallas_call(
        functools.partial(_ffn_up_prompt_kernel, tm=tm, tiles_per_seq=tiles_per_seq),
        grid=(m // tm, nf),
        in_specs=[pl.BlockSpec((tm, d), lambda i, j: (i, 0)),
                  pl.BlockSpec((FFN_HALO, d), lambda i, j: (jnp.maximum(i * halo_blocks - 1, 0), 0)),
                  pl.BlockSpec((1, d), lambda i, j: (0, 0)),
                  pl.BlockSpec((d, tn), lambda i, j: (0, j)),
                  pl.BlockSpec((d, tn), lambda i, j: (0, nf + j)),
                  pl.BlockSpec((3, tn), lambda i, j: (0, j)),
                  pl.BlockSpec((1, tn), lambda i, j: (0, j))],
        out_specs=[pl.BlockSpec((tm, tn), lambda i, j: (i, j)),
                   pl.BlockSpec((1, 2, tn), lambda i, j: (i, 0, j))],
        out_shape=[jax.ShapeDtypeStruct((m, D_FF), BF16),
                   jax.ShapeDtypeStruct((m // tm, 2, D_FF), F32)],
        scratch_shapes=[pltpu.VMEM((FFN_HALO + tm, d), BF16)],
        compiler_params=_cparams("parallel", "arbitrary"),
        name="ffn_up_prompt",
    )(x, x, g.reshape(1, d), w_up, w_up, cw, cb.reshape(1, D_FF))
    return u, cs.reshape(n_batch, tiles_per_seq, 2, D_FF)[:, -1]


def _ffn_up_sample_kernel(x_ref, g_ref, wa_ref, wv_ref, cw_ref, cb_ref, p1_ref, p2_ref,
                          u_ref, c6_ref, c7_ref, hb_ref, a_ref, *, tm, seq):
    @pl.when(pl.program_id(1) == 0)
    def _():
        _norm_to_scratch(x_ref, g_ref[...], hb_ref, tm)

    a = _dot(hb_ref[...], wa_ref[...])
    v = _dot(hb_ref[...], wv_ref[...])
    t = lax.broadcasted_iota(jnp.int32, (tm, 1), 0) % seq
    a1 = jnp.where(t >= 1, pltpu.roll(a, 1, 0), p1_ref[...])
    a2 = jnp.where(t >= 2, pltpu.roll(a, 2, 0), p2_ref[...])
    c = cb_ref[...] + a2 * cw_ref[0:1, :] + a1 * cw_ref[1:2, :] + a * cw_ref[2:3, :]
    u_ref[...] = _gelu_gate(c, v)
    for q in range(a.shape[1] // LANES):
        sl = slice(q * LANES, (q + 1) * LANES)
        a_ref[q] = a[:, sl]
        c6_ref[:, sl] = a_ref[q, pl.ds(seq - 2, tm // seq, stride=seq), :]
        c7_ref[:, sl] = a_ref[q, pl.ds(seq - 1, tm // seq, stride=seq), :]


def ffn_up_sample(x, g, w_up, cw, cb, p1, p2, seq, tm, tn):
    m, d = x.shape
    nf = D_FF // tn
    nb = tm // seq
    return pl.pallas_call(
        functools.partial(_ffn_up_sample_kernel, tm=tm, seq=seq),
        grid=(m // tm, nf),
        in_specs=[pl.BlockSpec((tm, d), lambda i, j: (i, 0)),
                  pl.BlockSpec((1, d), lambda i, j: (0, 0)),
                  pl.BlockSpec((d, tn), lambda i, j: (0, j)),
                  pl.BlockSpec((d, tn), lambda i, j: (0, nf + j)),
                  pl.BlockSpec((3, tn), lambda i, j: (0, j)),
                  pl.BlockSpec((1, tn), lambda i, j: (0, j)),
                  pl.BlockSpec((tm, tn), lambda i, j: (i, j)),
                  pl.BlockSpec((tm, tn), lambda i, j: (i, j))],
        out_specs=[pl.BlockSpec((tm, tn), lambda i, j: (i, j)),
                   pl.BlockSpec((nb, tn), lambda i, j: (i, j)),
                   pl.BlockSpec((nb, tn), lambda i, j: (i, j))],
        out_shape=[jax.ShapeDtypeStruct((m, D_FF), BF16),
                   jax.ShapeDtypeStruct((m // seq, D_FF), F32),
                   jax.ShapeDtypeStruct((m // seq, D_FF), F32)],
        scratch_shapes=[pltpu.VMEM((tm, d), BF16), pltpu.VMEM((tn // LANES, tm, LANES), F32)],
        compiler_params=_cparams("parallel", "arbitrary"),
        name="ffn_up_sample",
    )(x, g.reshape(1, d), w_up, w_up, cw, cb.reshape(1, D_FF), p1, p2)


def _seg_sum(x, e, two_pass):
    parts = []
    for j in range(x.shape[1] // LANES):
        xj = x[:, j * LANES:(j + 1) * LANES]
        if two_pass:
            hi, lo = _split(xj)
            parts.append(_dot(hi, e) + _dot(lo, e))
        else:
            parts.append(_dot(_bf(xj), e))
    return jnp.concatenate(parts, axis=1)


def _bcast_rows(x, idx, nb, c):
    parts = [jnp.broadcast_to(x[b * c + idx:b * c + idx + 1, :], (c, x.shape[1])) for b in range(nb)]
    return parts[0] if nb == 1 else jnp.concatenate(parts, axis=0)


def _unit_lower_solve(a_list, rhs_list, c):
    mm = lambda x, y: _dot(_bf(x), _bf(y))
    n = range(len(a_list))
    rows = lax.broadcasted_iota(jnp.int32, (c, c), 0)
    cols = lax.broadcasted_iota(jnp.int32, (c, c), 1)
    eye = (rows == cols).astype(F32)
    blk = min(c, 16)
    if c > blk:
        assert c // blk <= 4
        same = (rows // blk) == (cols // blk)
        ad = [jnp.where(same, a, 0.0) for a in a_list]
        ao = [a_list[i] - ad[i] for i in n]
    else:
        ad = a_list
    t = [eye - ad[i] for i in n]
    pw = ad
    span = 2
    while span < blk:
        pw = [mm(pw[i], pw[i]) for i in n]
        t = [t[i] + mm(t[i], pw[i]) for i in n]
        span *= 2
    x = [mm(t[i], rhs_list[i]) for i in n]
    if c > blk:
        nn = [mm(t[i], ao[i]) for i in n]
        n2 = [mm(nn[i], nn[i]) for i in n]
        x = [x[i] + mm(n2[i], x[i]) for i in n]
        x = [x[i] - mm(nn[i], x[i]) for i in n]
    return x


def _rwkv_kernel(p_ref, ps_ref, s0_ref, mu_ref, vec_ref, w2_ref, a2_ref, g2_ref, e_ref, lt_ref,
                 tok_ref, sout_ref, prev_ref, s_ref, y_ref, *, c, nb):
    ci = pl.program_id(1)
    tw = TOK_WIDTH

    @pl.when(ci == 0)
    def _():
        for b in range(nb):
            s_ref[b * A_HEADS:(b + 1) * A_HEADS] = s0_ref[b]
        prev_ref[...] = ps_ref[...]

    p = p_ref[...]
    rows = lax.broadcasted_iota(jnp.int32, (nb * c, 1), 0)
    p_prev = pltpu.roll(p, 1, 0)
    for b in range(nb):
        p_prev = jnp.where(rows == b * c, prev_ref[b], p_prev)
        prev_ref[b] = p_ref[b * c + c - 1:b * c + c, :]
    xm = p + mu_ref[...] * (p_prev - p)

    r = xm[:, 0:tw]
    k = xm[:, tw:2 * tw]
    v = xm[:, 2 * tw:3 * tw]
    xw = xm[:, 3 * tw:3 * tw + A_LORA_PAD]
    xa = xm[:, 3 * tw + A_LORA_PAD:3 * tw + 2 * A_LORA_PAD]
    xg = xm[:, 3 * tw + 2 * A_LORA_PAD:]
    w0, a0, k_k, k_a, r_k, ln_w, ln_b = (vec_ref[i:i + 1, :] for i in range(7))

    z = -(w0 + _dot(_bf(jnp.tanh(xw)), w2_ref[...]))
    softplus = jnp.maximum(z, 0.0) + jnp.log(1.0 + jnp.exp(-jnp.abs(z)))
    ell = -jnp.exp(-softplus - 0.5)
    a = _sigmoid(a0 + _dot(_bf(xa), a2_ref[...]))
    gate = _dot(_bf(_sigmoid(xg)), g2_ref[...])
    kkraw = k * k_k
    k2 = k * (1.0 + (a - 1.0) * k_a)
    e = e_ref[...]
    kap = kkraw * lax.rsqrt(jnp.maximum(_seg_sum(kkraw * kkraw, e, True), 1e-24))
    bet = kap * a

    ell_hi, ell_lo = _split(ell)
    gc = _dot(lt_ref[...], ell_hi) + _dot(lt_ref[...], ell_lo)
    glast = _bcast_rows(gc, c - 1, nb, c)
    egi = jnp.exp(-gc)
    el = jnp.exp(glast - gc)
    xk = kap * jnp.exp(gc - ell)
    xr = r * jnp.exp(gc)
    kb = k2 * egi
    bb = bet * egi
    kh = k2 * el
    bh = bet * el
    gam = jnp.exp(glast)
    ti = lax.broadcasted_iota(jnp.int32, (c, c), 0)
    si = lax.broadcasted_iota(jnp.int32, (c, c), 1)
    strict = si < ti
    incl = si <= ti

    ch = [(b, h) for b in range(nb) for h in range(A_HEADS)]
    n = range(len(ch))
    rs = [slice(b * c, (b + 1) * c) for b, _ in ch]
    ls = [slice(h * A_HEAD_DIM, (h + 1) * A_HEAD_DIM) for _, h in ch]
    s_old = [s_ref[i] for i in n]
    sb = [_bf(s) for s in s_old]
    xk_h = [_bf(xk[rs[i], ls[i]]) for i in n]
    xr_h = [_bf(xr[rs[i], ls[i]]) for i in n]
    kb_h = [_bf(kb[rs[i], ls[i]]) for i in n]
    bb_h = [_bf(bb[rs[i], ls[i]]) for i in n]
    v_f = [v[rs[i], ls[i]] for i in n]
    v_h = [_bf(x) for x in v_f]
    a_kk = [jnp.where(strict, _dot_nt(xk_h[i], kb_h[i]), 0.0) for i in n]
    a_kb = [jnp.where(strict, _dot_nt(xk_h[i], bb_h[i]), 0.0) for i in n]
    a_rk = [jnp.where(incl, _dot_nt(xr_h[i], kb_h[i]), 0.0) for i in n]
    a_rb = [jnp.where(incl, _dot_nt(xr_h[i], bb_h[i]), 0.0) for i in n]
    rhs = [-(_dot_nt(xk_h[i], sb[i]) + _dot(_bf(a_kk[i]), v_h[i])) for i in n]
    u = _unit_lower_solve(a_kb, rhs, c)
    u_h = [_bf(x) for x in u]
    y_h = [_dot_nt(xr_h[i], sb[i]) + _dot(_bf(a_rk[i]), v_h[i]) + _dot(_bf(a_rb[i]), u_h[i]) for i in n]
    vu = [_bf(jnp.concatenate([v_f[i], u[i]], axis=0)) for i in n]
    kbh = [_bf(jnp.concatenate([kh[rs[i], ls[i]], bh[rs[i], ls[i]]], axis=0)) for i in n]
    s_new = [s_old[i] * gam[ch[i][0] * c:ch[i][0] * c + 1, ls[i]] + _dot_tn(vu[i], kbh[i]) for i in n]
    for i in n:
        y_ref[rs[i], ls[i]] = y_h[i]
        s_ref[i] = s_new[i]

    y = y_ref[...]
    inv_n = 1.0 / A_HEAD_DIM
    mean = _seg_sum(y, e, False) * inv_n
    d = y - mean
    var = _seg_sum(d * d, e, False) * inv_n
    yn = d * lax.rsqrt(var + GN_EPS) * ln_w + ln_b
    bonus = _seg_sum(r * k2 * r_k, e, False) * v
    tok_ref[...] = ((yn + bonus) * gate).astype(tok_ref.dtype)

    @pl.when(ci == pl.num_programs(1) - 1)
    def _():
        for b in range(nb):
            sout_ref[b] = s_ref[b * A_HEADS:(b + 1) * A_HEADS]


def _block_ltri(nb, c):
    t = jnp.arange(nb * c)
    return ((t[None, :] <= t[:, None]) & (t[None, :] // c == t[:, None] // c)).astype(BF16)


def rwkv_mix(p, pshift, s0, mu, vecs, w2, a2, g2, n_batch, seq, c, nb, out_dtype):
    assert nb == 1 or (seq == c and c == 8)
    m = p.shape[0]
    n_chunks = seq // c
    rows = nb * c
    lane = jnp.arange(LANES)
    e = (lane[:, None] // A_HEAD_DIM == lane[None, :] // A_HEAD_DIM).astype(BF16)
    const = lambda shape: pl.BlockSpec(shape, lambda b, ci: (0,) * len(shape))
    st_spec = pl.BlockSpec((nb, A_HEADS, A_HEAD_DIM, A_HEAD_DIM), lambda b, ci: (b, 0, 0, 0))
    return pl.pallas_call(
        functools.partial(_rwkv_kernel, c=c, nb=nb),
        grid=(n_batch // nb, n_chunks),
        in_specs=[pl.BlockSpec((rows, A_TOK_COLS), lambda b, ci: (b * n_chunks + ci, 0)),
                  pl.BlockSpec((nb, 1, A_TOK_COLS), lambda b, ci: (b, 0, 0)),
                  st_spec,
                  const((1, A_TOK_COLS)), const((8, TOK_WIDTH)),
                  const((A_LORA_PAD, TOK_WIDTH)), const((A_LORA_PAD, TOK_WIDTH)),
                  const((A_GATE_RANK, TOK_WIDTH)), const((LANES, LANES)), const((rows, rows))],
        out_specs=[pl.BlockSpec((rows, TOK_WIDTH), lambda b, ci: (b * n_chunks + ci, 0)), st_spec],
        out_shape=[jax.ShapeDtypeStruct((m, TOK_WIDTH), out_dtype),
                   jax.ShapeDtypeStruct((n_batch, A_HEADS, A_HEAD_DIM, A_HEAD_DIM), F32)],
        scratch_shapes=[pltpu.VMEM((nb, 1, A_TOK_COLS), F32),
                        pltpu.VMEM((nb * A_HEADS, A_HEAD_DIM, A_HEAD_DIM), F32),
                        pltpu.VMEM((rows, TOK_WIDTH), F32)],
        compiler_params=_cparams("parallel", "arbitrary"),
        name="rwkv_mix",
    )(p, pshift, s0, mu, vecs, w2, a2, g2, e, _block_ltri(nb, c))


def _hgrn_kernel(p_ref, s0_ref, lbp_ref, gn_ref, lt_ref, tok_ref, sout_ref, st_ref, *, c, nb, layer, col0):
    ci = pl.program_id(1)
    tw = TOK_WIDTH
    hd = B_HEAD_DIM
    ch = [(b, h) for b in range(nb) for h in range(B_HEADS)]
    n = range(len(ch))

    @pl.when(ci == 0)
    def _():
        for i in n:
            st_ref[i] = s0_ref[ch[i][0], ch[i][1]].T

    lbp = lbp_ref[...]
    mx = jnp.max(lbp, axis=0, keepdims=True)
    ex = jnp.exp(lbp - mx)
    den = jnp.sum(ex, axis=0, keepdims=True)
    lb = jnp.zeros((1, tw), F32)
    for i in range(1, layer + 1):
        lb = lb + ex[i:i + 1, :] / den

    q = p_ref[:, col0:col0 + tw]
    f = p_ref[:, col0 + tw:col0 + 2 * tw]
    iv = p_ref[:, col0 + 2 * tw:col0 + 3 * tw]
    og = p_ref[:, col0 + 3 * tw:col0 + 4 * tw]
    fg = lb + (1.0 - lb) * _sigmoid(f)
    lf = jnp.log(fg)
    kk = 1.0 - fg
    qq = q * _sigmoid(q)
    lf_hi, lf_lo = _split(lf)
    bc = _dot(lt_ref[...], lf_hi) + _dot(lt_ref[...], lf_lo)
    blast = _bcast_rows(bc, c - 1, nb, c)
    mid = _bcast_rows(bc, (c - 1) // 2, nb, c)
    qe = qq * jnp.exp(bc - mid)
    ke = kk * jnp.exp(mid - bc)
    qs = qq * jnp.exp(bc)
    kl = kk * jnp.exp(blast - bc)
    gam = jnp.exp(blast)
    gate = og * _sigmoid(og)
    ti = lax.broadcasted_iota(jnp.int32, (c, c), 0)
    si = lax.broadcasted_iota(jnp.int32, (c, c), 1)
    incl = si <= ti
    gn = gn_ref[...]
    rs = [slice(b * c, (b + 1) * c) for b, _ in ch]
    ls = [slice(h * hd, (h + 1) * hd) for _, h in ch]
    st = [st_ref[i] for i in n]
    v_h = [_bf(iv[rs[i], ls[i]]) for i in n]
    att = [jnp.where(incl, _dot_nt(_bf(qe[rs[i], ls[i]]), _bf(ke[rs[i], ls[i]])), 0.0) for i in n]
    o = [_dot(_bf(att[i]), v_h[i]) + _dot_nt(_bf(qs[rs[i], ls[i]]), _bf(st[i])) for i in n]
    st_new = [st[i] * gam[ch[i][0] * c:ch[i][0] * c + 1, ls[i]] + _dot_tn(v_h[i], _bf(kl[rs[i], ls[i]])) for i in n]
    on = [x * lax.rsqrt(jnp.mean(x * x, axis=-1, keepdims=True) + RMS_EPS) * gn for x in o]
    for i in n:
        tok_ref[rs[i], ls[i]] = (on[i] * gate[rs[i], ls[i]]).astype(tok_ref.dtype)
        st_ref[i] = st_new[i]

    @pl.when(ci == pl.num_programs(1) - 1)
    def _():
        for i in n:
            sout_ref[ch[i][0], ch[i][1]] = st_ref[i].T


def hgrn_mix(p, col0, s0, lbp, gn, n_batch, seq, c, nb, layer, out_dtype):
    assert nb == 1 or (seq == c and c == 8)
    m, cols = p.shape
    n_chunks = seq // c
    rows = nb * c
    const = lambda shape: pl.BlockSpec(shape, lambda b, ci: (0,) * len(shape))
    st_spec = pl.BlockSpec((nb, B_HEADS, B_HEAD_DIM, B_HEAD_DIM), lambda b, ci: (b, 0, 0, 0))
    return pl.pallas_call(
        functools.partial(_hgrn_kernel, c=c, nb=nb, layer=layer, col0=col0),
        grid=(n_batch // nb, n_chunks),
        in_specs=[pl.BlockSpec((rows, cols), lambda b, ci: (b * n_chunks + ci, 0)),
                  st_spec, const((DEPTH, TOK_WIDTH)), const((1, B_HEAD_DIM)), const((rows, rows))],
        out_specs=[pl.BlockSpec((rows, TOK_WIDTH), lambda b, ci: (b * n_chunks + ci, 0)), st_spec],
        out_shape=[jax.ShapeDtypeStruct((m, TOK_WIDTH), out_dtype),
                   jax.ShapeDtypeStruct((n_batch, B_HEADS, B_HEAD_DIM, B_HEAD_DIM), F32)],
        scratch_shapes=[pltpu.VMEM((nb * B_HEADS, B_HEAD_DIM, B_HEAD_DIM), F32)],
        compiler_params=_cparams("parallel", "arbitrary"),
        name="hgrn_mix",
    )(p, s0, lbp, gn.reshape(1, B_HEAD_DIM), _block_ltri(nb, c))


PROMPT_CHUNK = 64
SAMPLE_SEQS_PER_STEP = 4


def _trunk(x, n_batch, seq, mem, st_rwkv, pshift, st_hgrn, conv_hist, w, is_prompt):
    m = x.shape[0]
    c = PROMPT_CHUNK if is_prompt else seq
    nb = 1 if is_prompt else SAMPLE_SEQS_PER_STEP
    tm_big = 1024
    tok_dtype = BF16 if is_prompt else F32
    new_conv = []

    def attend(p, q_colblock, layer):
        if is_prompt:
            return mem_attention(p, q_colblock, mem[layer], n_batch, seq, 512)
        return mem_attention_cache(p, q_colblock, mem[0], mem[1], layer, seq, 8)

    def ffn(x, layer):
        if is_prompt:
            u, cs = ffn_up_prompt(x, w["norm_ffn"][layer], w["ffn_w_up"][layer], w["ffn_conv_w"][layer],
                                  w["ffn_conv_b"][layer], n_batch, seq, 1024, 512)
        else:
            p1, p2 = conv_hist[layer]
            u, c6, c7 = ffn_up_sample(x, w["norm_ffn"][layer], w["ffn_w_up"][layer], w["ffn_conv_w"][layer],
                                      w["ffn_conv_b"][layer], p1, p2, seq, 512, 512)
            cs = jnp.stack([c6, c7], axis=1)
        new_conv.append(cs)
        return matmul([u], [w["ffn_w_down"][layer]], x, D_MODEL, 1024, 512)

    p = norm_matmul(x, w["norm_mix"][0], w["a_w_in"], tm_big, 512)
    tok, s_rwkv = rwkv_mix(p, pshift, st_rwkv, w["a_mu"], w["a_vecs"], w["a_w2"], w["a_a2"], w["a_g2"],
                           n_batch, seq, c, nb, tok_dtype)
    mem_o = attend(p, A_TOK_COLS // MEM_WIDTH, 0)
    x = matmul([tok, mem_o], [w["a_w_out_tok"], w["a_w_out_mem"]], x, D_MODEL, tm_big, 1024)
    x = ffn(x, 0)
    p = norm_matmul(x, w["norm_mix"][1], w["b_w_in"], tm_big, 512)
    tok, s_hgrn = hgrn_mix(p, MEM_WIDTH, st_hgrn, w["b_lower_bounds"], w["b_g_norm"], n_batch, seq, c, nb, 1,
                           tok_dtype)
    mem_o = attend(p, 0, 1)
    x = matmul([tok, mem_o], [w["b_w_out_tok"], w["b_w_out_mem"]], x, D_MODEL, tm_big, 1024)
    x = ffn(x, 1)
    y = rmsnorm(x, w["norm_final"], 256)
    return y, s_rwkv, s_hgrn, jnp.stack(new_conv)


def kernel(x_prompt, x_sample, mem_prompt, cache_mem_k, cache_mem_v, state_rwkv, state_shift, state_hgrn, state_conv, norm_mix, norm_ffn, norm_final, mem_norm, w_mem_kv, a_w_in, a_mu, a_w0, a_w2, a_a0, a_a2, a_g2, a_k_k, a_k_a, a_r_k, a_ln_w, a_ln_b, a_w_out, b_w_in, b_lower_bounds, b_g_norm, b_w_out, ffn_w_up, ffn_conv_w, ffn_conv_b, ffn_w_down):
    bp, sp, d = x_prompt.shape
    bs, ss, _ = x_sample.shape

    wa = a_w_in[0]
    rkv_end = MEM_WIDTH + 3 * TOK_WIDTH
    zc = jnp.zeros((d, A_LORA_PAD - 96), F32)
    wa_p = _bf(jnp.concatenate([wa[:, MEM_WIDTH:rkv_end], wa[:, rkv_end:rkv_end + 96], zc,
                                wa[:, rkv_end + 96:rkv_end + 192], zc, wa[:, rkv_end + 192:],
                                wa[:, :MEM_WIDTH]], axis=1))
    mu = a_mu[0]
    zv = jnp.zeros((A_LORA_PAD - 96,), F32)
    mu_p = jnp.concatenate([mu[:3 * TOK_WIDTH + 96], zv, mu[3 * TOK_WIDTH + 96:3 * TOK_WIDTH + 192], zv,
                            mu[3 * TOK_WIDTH + 192:]]).reshape(1, A_TOK_COLS)
    zr = jnp.zeros((A_LORA_PAD - 96, TOK_WIDTH), F32)
    w = dict(
        norm_mix=norm_mix, norm_ffn=norm_ffn, norm_final=norm_final,
        a_w_in=wa_p, a_mu=mu_p,
        a_vecs=jnp.stack([a_w0[0], a_a0[0], a_k_k[0], a_k_a[0], a_r_k[0], a_ln_w[0], a_ln_b[0],
                          jnp.zeros((TOK_WIDTH,), F32)]),
        a_w2=_bf(jnp.concatenate([a_w2[0], zr], axis=0)),
        a_a2=_bf(jnp.concatenate([a_a2[0], zr], axis=0)),
        a_g2=_bf(a_g2[0]),
        a_w_out_tok=_bf(a_w_out[0][:TOK_WIDTH]), a_w_out_mem=_bf(a_w_out[0][TOK_WIDTH:]),
        b_w_in=_bf(b_w_in[0]),
        b_lower_bounds=b_lower_bounds, b_g_norm=b_g_norm[0],
        b_w_out_tok=_bf(b_w_out[0][:TOK_WIDTH]), b_w_out_mem=_bf(b_w_out[0][TOK_WIDTH:]),
        ffn_w_up=[_bf(ffn_w_up[l]) for l in range(DEPTH)], ffn_conv_w=ffn_conv_w, ffn_conv_b=ffn_conv_b,
        ffn_w_down=[_bf(ffn_w_down[l]) for l in range(DEPTH)],
    )

    mem_rows = mem_prompt.reshape(bp * N_MEM, d)
    mem_kv = [norm_matmul(mem_rows, mem_norm[l], _bf(w_mem_kv[l]), bp * N_MEM, 512) for l in range(DEPTH)]
    mem_k_prompt = jnp.stack([kv[:, :MEM_WIDTH] for kv in mem_kv]).reshape(DEPTH, bp, N_MEM, MEM_HEADS, MEM_HEAD_DIM)
    mem_v_prompt = jnp.stack([kv[:, MEM_WIDTH:] for kv in mem_kv]).reshape(DEPTH, bp, N_MEM, MEM_HEADS, MEM_HEAD_DIM)
    xp = x_prompt.reshape(bp * sp, d)
    y_p, rwkv_p, hgrn_p, conv_p = _trunk(
        xp, bp, sp, mem_kv,
        jnp.zeros((bp, A_HEADS, A_HEAD_DIM, A_HEAD_DIM), F32), jnp.zeros((bp, 1, A_TOK_COLS), F32),
        jnp.zeros((bp, B_HEADS, B_HEAD_DIM, B_HEAD_DIM), F32), None, w, True)
    shift_p = rmsnorm(x_prompt[:, -1], norm_mix[0], bp)

    xs = x_sample.reshape(bs * ss, d)
    pshift = matmul([state_shift[0]], [wa_p], None, A_TOK_COLS, bs, 512).reshape(bs, 1, A_TOK_COLS)
    hist = []
    for l in range(DEPTH):
        h = state_conv[l]
        p1 = jnp.pad(h[:, 1:2], ((0, 0), (0, ss - 1), (0, 0))).reshape(bs * ss, D_FF)
        p2 = jnp.pad(h, ((0, 0), (0, ss - 2), (0, 0))).reshape(bs * ss, D_FF)
        hist.append((p1, p2))
    y_s, rwkv_s, hgrn_s, conv_s = _trunk(xs, bs, ss, (cache_mem_k, cache_mem_v), state_rwkv[0], pshift, state_hgrn[0], hist, w, False)
    shift_s = rmsnorm(x_sample[:, -1], norm_mix[0], bs)

    return (y_p.reshape(bp, sp, d), y_s.reshape(bs, ss, d), mem_k_prompt, mem_v_prompt,
            rwkv_p[None], rwkv_s[None], shift_p[None], shift_s[None], hgrn_p[None], hgrn_s[None],
            conv_p, conv_s)
```

```python
import functools

import jax
import jax.numpy as jnp
from jax import lax
from jax.experimental import pallas as pl
from jax.experimental.pallas import tpu as pltpu

F32 = jnp.float32
BF16 = jnp.bfloat16

D_MODEL = 2048
DEPTH = 2
TOK_WIDTH = 1536
MEM_WIDTH = 512
MEM_HEADS = 4
MEM_HEAD_DIM = 128
N_MEM = 256
A_HEAD_DIM = 64
A_HEADS = 24
A_LORA = 96
A_GATE_RANK = 256
A_TOK_COLS = 5120
A_PROJ_COLS = MEM_WIDTH + A_TOK_COLS
B_HEADS = 12
B_HEAD_DIM = 128
D_FF = 5632
RMS_EPS = 1e-6
GN_EPS = 64e-5

LANES = 128
VMEM_LIMIT = 56 * 1024 * 1024


def _cparams(*sem):
    return pltpu.CompilerParams(dimension_semantics=sem, vmem_limit_bytes=VMEM_LIMIT)


def _dot(a, b):
    return jnp.dot(a, b, preferred_element_type=F32)


def _dot_nt(a, b):
    return lax.dot_general(a, b, (((1,), (1,)), ((), ())), preferred_element_type=F32)


def _dot_tn(a, b):
    return lax.dot_general(a, b, (((0,), (0,)), ((), ())), preferred_element_type=F32)


def _bf(x):
    return x.astype(BF16)


def _split(x):
    hi = x.astype(BF16)
    lo = (x - hi.astype(F32)).astype(BF16)
    return hi, lo


def _sigmoid(x):
    return 1.0 / (1.0 + jnp.exp(-x))


def _rms(x, g):
    ms = jnp.mean(x * x, axis=-1, keepdims=True)
    return x * lax.rsqrt(ms + RMS_EPS) * g


def _seg_bounds(segs, tm):
    bounds, off = [], 0
    for a in segs:
        bounds.append(off)
        off += a.shape[0] // tm
    return tuple(bounds), off


def _seg_row_specs(segs, tm, width, col_of_j):
    specs, off = [], 0
    for a in segs:
        n = a.shape[0] // tm
        if col_of_j:
            imap = lambda i, j, off=off, n=n: (jnp.clip(i - off, 0, n - 1), j)
        else:
            imap = lambda i, j, off=off, n=n: (jnp.clip(i - off, 0, n - 1), 0)
        specs.append(pl.BlockSpec((tm, width), imap))
        off += n
    return specs


def _for_segment(i, bounds, fn):
    if len(bounds) == 1:
        fn(0)
        return
    for s in range(len(bounds)):
        cond = i >= bounds[s]
        if s + 1 < len(bounds):
            cond = jnp.logical_and(cond, i < bounds[s + 1])
        pl.when(cond)(functools.partial(fn, s))


def _w_spec(w, tn, col_block0=0):
    if isinstance(w, tuple):
        arr, layer = w
        return arr, pl.BlockSpec((None, arr.shape[1], tn), lambda i, j: (layer, 0, col_block0 + j))
    return w, pl.BlockSpec((w.shape[0], tn), lambda i, j: (0, col_block0 + j))


def _rmsnorm_kernel(x_ref, g_ref, o_ref):
    o_ref[...] = _rms(x_ref[...], g_ref[...])


def rmsnorm(x, g, tm, row0=0, rows=None):
    d = x.shape[1]
    rows = x.shape[0] if rows is None else rows
    t0 = row0 // tm
    return pl.pallas_call(
        _rmsnorm_kernel,
        grid=(rows // tm,),
        in_specs=[pl.BlockSpec((tm, d), lambda i: (t0 + i, 0)), pl.BlockSpec((1, d), lambda i: (0, 0))],
        out_specs=pl.BlockSpec((tm, d), lambda i: (i, 0)),
        out_shape=jax.ShapeDtypeStruct((rows, d), F32),
        compiler_params=_cparams("parallel"),
        name="rmsnorm",
    )(x, g.reshape(1, d))


NORM_ROWS = 256


def _norm_to_scratch(x_ref, g, hb_ref, tm, dst_off=0):
    for r in range(0, tm, NORM_ROWS):
        n = min(NORM_ROWS, tm - r)
        hb_ref[dst_off + r:dst_off + r + n, :] = _bf(_rms(x_ref[r:r + n, :], g))


def _norm_matmul_kernel(*refs, n_seg, bounds, tm):
    x_refs = refs[:n_seg]
    g_ref, w_ref, o_ref, hb_ref = refs[n_seg:]

    @pl.when(pl.program_id(1) == 0)
    def _():
        g = g_ref[...]
        _for_segment(pl.program_id(0), bounds, lambda s: _norm_to_scratch(x_refs[s], g, hb_ref, tm))

    o_ref[...] = _dot(hb_ref[...], w_ref[...])


def norm_matmul(x_segs, g, w, tm, tn):
    d = x_segs[0].shape[1]
    bounds, n_tiles = _seg_bounds(x_segs, tm)
    w_arr, w_spec = _w_spec(w, tn)
    n = w_arr.shape[-1]
    return pl.pallas_call(
        functools.partial(_norm_matmul_kernel, n_seg=len(x_segs), bounds=bounds, tm=tm),
        grid=(n_tiles, n // tn),
        in_specs=_seg_row_specs(x_segs, tm, d, False) + [pl.BlockSpec((1, d), lambda i, j: (0, 0)), w_spec],
        out_specs=pl.BlockSpec((tm, tn), lambda i, j: (i, j)),
        out_shape=jax.ShapeDtypeStruct((n_tiles * tm, n), F32),
        scratch_shapes=[pltpu.VMEM((tm, d), BF16)],
        compiler_params=_cparams("parallel", "arbitrary"),
        name="norm_matmul",
    )(*x_segs, g.reshape(1, d), w_arr)


def _mm_kernel(*refs, a_counts, res_count, bounds):
    pos = 0
    a_refs = []
    for cnt in a_counts:
        a_refs.append(refs[pos:pos + cnt])
        pos += cnt
    w_refs = refs[pos:pos + len(a_counts)]
    pos += len(a_counts)
    res_refs = refs[pos:pos + res_count]
    o_ref = refs[-1]

    def compute(s):
        acc = None
        for k, segs in enumerate(a_refs):
            term = _dot(_bf(segs[min(s, len(segs) - 1)][...]), w_refs[k][...])
            acc = term if acc is None else acc + term
        if res_count:
            acc = res_refs[min(s, res_count - 1)][...] + acc
        o_ref[...] = acc

    _for_segment(pl.program_id(0), bounds, compute)


def matmul(a_list, w_list, res_segs, n_out, tm, tn):
    longest = max(a_list + ([res_segs] if res_segs else []), key=len)
    bounds, n_tiles = _seg_bounds(longest, tm)
    in_specs, args = [], []
    for segs in a_list:
        assert len(segs) in (1, len(bounds))
        in_specs += _seg_row_specs(segs, tm, segs[0].shape[1], False)
        args += list(segs)
    for w in w_list:
        w_arr, w_spec = _w_spec(w, tn)
        in_specs.append(w_spec)
        args.append(w_arr)
    if res_segs:
        assert len(res_segs) in (1, len(bounds))
        in_specs += _seg_row_specs(res_segs, tm, tn, True)
        args += list(res_segs)
    return pl.pallas_call(
        functools.partial(_mm_kernel, a_counts=tuple(len(s) for s in a_list),
                          res_count=len(res_segs) if res_segs else 0, bounds=bounds),
        grid=(n_tiles, n_out // tn),
        in_specs=in_specs,
        out_specs=pl.BlockSpec((tm, tn), lambda i, j: (i, j)),
        out_shape=jax.ShapeDtypeStruct((n_tiles * tm, n_out), F32),
        compiler_params=_cparams("parallel", "parallel"),
        name="matmul",
    )(*args)


def _mm_acc_kernel(*refs, n_seg, bounds):
    a_refs = refs[:n_seg]
    w_ref, res_ref, o_ref = refs[n_seg:]
    k = pl.program_id(2)

    def compute(s):
        term = _dot(_bf(a_refs[s][...]), w_ref[...])

        @pl.when(k == 0)
        def _():
            o_ref[...] = res_ref[...] + term

        @pl.when(k > 0)
        def _():
            o_ref[...] += term

    _for_segment(pl.program_id(0), bounds, compute)


def matmul_acc(a_segs, w, layer, res, tm, tn, tk):
    bounds, n_tiles = _seg_bounds(a_segs, tm)
    kdim, n_out = w.shape[1:]
    a_specs, off = [], 0
    for a in a_segs:
        n = a.shape[0] // tm
        a_specs.append(pl.BlockSpec((tm, tk), lambda i, j, k, off=off, n=n: (jnp.clip(i - off, 0, n - 1), k)))
        off += n
    return pl.pallas_call(
        functools.partial(_mm_acc_kernel, n_seg=len(a_segs), bounds=bounds),
        grid=(n_tiles, n_out // tn, kdim // tk),
        in_specs=a_specs + [pl.BlockSpec((None, tk, tn), lambda i, j, k: (layer, k, j)),
                            pl.BlockSpec((tm, tn), lambda i, j, k: (i, j))],
        out_specs=pl.BlockSpec((tm, tn), lambda i, j, k: (i, j)),
        out_shape=jax.ShapeDtypeStruct((n_tiles * tm, n_out), F32),
        compiler_params=_cparams("parallel", "parallel", "arbitrary"),
        name="matmul_acc",
    )(*a_segs, w, res)


def _attn_kernel(q_ref, k_ref, v_ref, o_ref, *, n_seq, tq, cache_layout):
    scale = MEM_HEAD_DIM ** -0.5
    pairs = [(s, h) for s in range(n_seq) for h in range(MEM_HEADS)]
    hsl = [slice(h * MEM_HEAD_DIM, (h + 1) * MEM_HEAD_DIM) for h in range(MEM_HEADS)]
    q = [_bf(q_ref[s * tq:(s + 1) * tq, hsl[h]]) for s, h in pairs]
    if cache_layout:
        k = [_bf(k_ref[0, s, pl.ds(h, N_MEM, stride=MEM_HEADS), :]) for s, h in pairs]
        v = [_bf(v_ref[0, s, pl.ds(h, N_MEM, stride=MEM_HEADS), :]) for s, h in pairs]
    else:
        k = [_bf(k_ref[s * N_MEM:(s + 1) * N_MEM, hsl[h]]) for s, h in pairs]
        v = [_bf(v_ref[s * N_MEM:(s + 1) * N_MEM, hsl[h]]) for s, h in pairs]
    sc = [_dot_nt(q[i], k[i]) * scale for i in range(len(pairs))]
    e = [jnp.exp(x - jnp.max(x, axis=-1, keepdims=True)) for x in sc]
    p = [_bf(x / jnp.sum(x, axis=-1, keepdims=True)) for x in e]
    o = [_dot(p[i], v[i]) for i in range(len(pairs))]
    rows = [jnp.concatenate(o[s * MEM_HEADS:(s + 1) * MEM_HEADS], axis=1) for s in range(n_seq)]
    o_ref[...] = _bf(jnp.concatenate(rows, axis=0) if n_seq > 1 else rows[0])


def mem_attention(p, row0, q_colblock, kv, n_batch, seq, tq):
    q_tiles = seq // tq
    t0 = row0 // tq
    return pl.pallas_call(
        functools.partial(_attn_kernel, n_seq=1, tq=tq, cache_layout=False),
        grid=(n_batch, q_tiles),
        in_specs=[pl.BlockSpec((tq, MEM_WIDTH), lambda b, t: (t0 + b * q_tiles + t, q_colblock)),
                  pl.BlockSpec((N_MEM, MEM_WIDTH), lambda b, t: (b, 0)),
                  pl.BlockSpec((N_MEM, MEM_WIDTH), lambda b, t: (b, 1))],
        out_specs=pl.BlockSpec((tq, MEM_WIDTH), lambda b, t: (b * q_tiles + t, 0)),
        out_shape=jax.ShapeDtypeStruct((n_batch * seq, MEM_WIDTH), BF16),
        compiler_params=_cparams("parallel", "parallel"),
        name="mem_attention",
    )(p, kv, kv)


def mem_attention_cache(p, row0, q_colblock, cache_k, cache_v, layer, seq, n_seq):
    depth, n_batch = cache_k.shape[:2]
    cache_k = cache_k.reshape(depth, n_batch, N_MEM * MEM_HEADS, MEM_HEAD_DIM)
    cache_v = cache_v.reshape(depth, n_batch, N_MEM * MEM_HEADS, MEM_HEAD_DIM)
    rows = n_seq * seq
    t0 = row0 // rows
    kv_spec = pl.BlockSpec((1, n_seq, N_MEM * MEM_HEADS, MEM_HEAD_DIM), lambda b: (layer, b, 0, 0))
    return pl.pallas_call(
        functools.partial(_attn_kernel, n_seq=n_seq, tq=seq, cache_layout=True),
        grid=(n_batch // n_seq,),
        in_specs=[pl.BlockSpec((rows, MEM_WIDTH), lambda b: (t0 + b, q_colblock)), kv_spec, kv_spec],
        out_specs=pl.BlockSpec((rows, MEM_WIDTH), lambda b: (b, 0)),
        out_shape=jax.ShapeDtypeStruct((n_batch * seq, MEM_WIDTH), BF16),
        compiler_params=_cparams("parallel"),
        name="mem_attention_cache",
    )(p, cache_k, cache_v)


FFN_HALO = 16


def _gelu_gate(c, v):
    return _bf(jax.nn.gelu(c) * v)


def _ffn_up_prompt_kernel(x_ref, xh_ref, g_ref, wa_ref, wv_ref, cw_ref, cb_ref, u_ref, cs_ref, hb_ref,
                          *, tm, tiles_per_seq):
    i = pl.program_id(0)

    @pl.when(pl.program_id(1) == 0)
    def _():
        g = g_ref[...]
        hb_ref[0:FFN_HALO, :] = _bf(_rms(xh_ref[...], g))
        _norm_to_scratch(x_ref, g, hb_ref, tm, dst_off=FFN_HALO)

    a_ext = _dot(hb_ref[...], wa_ref[...])
    v = _dot(hb_ref[FFN_HALO:, :], wv_ref[...])
    rows = lax.broadcasted_iota(jnp.int32, (FFN_HALO + tm, 1), 0)
    n_zero = jnp.where((i % tiles_per_seq) == 0, FFN_HALO, 0)
    a_ext = jnp.where(rows < n_zero, 0.0, a_ext)
    a0 = a_ext[FFN_HALO:]
    a1 = pltpu.roll(a_ext, 1, 0)[FFN_HALO:]
    a2 = pltpu.roll(a_ext, 2, 0)[FFN_HALO:]
    c = cb_ref[...] + a2 * cw_ref[0:1, :] + a1 * cw_ref[1:2, :] + a0 * cw_ref[2:3, :]
    u_ref[...] = _gelu_gate(c, v)
    cs_ref[0] = a0[tm - 8:tm][6:8]


def ffn_up_prompt(x, g, w_up, layer, cw, cb, n_batch, seq, tm, tn):
    d = x.shape[1]
    m = n_batch * seq
    nf = D_FF // tn
    tiles_per_seq = seq // tm
    halo_blocks = tm // FFN_HALO
    w_arr, wa_spec = _w_spec((w_up, layer), tn)
    _, wv_spec = _w_spec((w_up, layer), tn, nf)
    u, cs = pl.pallas_call(
        functools.partial(_ffn_up_prompt_kernel, tm=tm, tiles_per_seq=tiles_per_seq),
        grid=(m // tm, nf),
        in_specs=[pl.BlockSpec((tm, d), lambda i, j: (i, 0)),
                  pl.BlockSpec((FFN_HALO, d), lambda i, j: (jnp.maximum(i * halo_blocks - 1, 0), 0)),
                  pl.BlockSpec((1, d), lambda i, j: (0, 0)),
                  wa_spec, wv_spec,
                  pl.BlockSpec((3, tn), lambda i, j: (0, j)),
                  pl.BlockSpec((1, tn), lambda i, j: (0, j))],
        out_specs=[pl.BlockSpec((tm, tn), lambda i, j: (i, j)),
                   pl.BlockSpec((1, 2, tn), lambda i, j: (i, 0, j))],
        out_shape=[jax.ShapeDtypeStruct((m, D_FF), BF16),
                   jax.ShapeDtypeStruct((m // tm, 2, D_FF), F32)],
        scratch_shapes=[pltpu.VMEM((FFN_HALO + tm, d), BF16)],
        compiler_params=_cparams("parallel", "arbitrary"),
        name="ffn_up_prompt",
    )(x, x, g.reshape(1, d), w_arr, w_arr, cw, cb.reshape(1, D_FF))
    return u, cs.reshape(n_batch, tiles_per_seq, 2, D_FF)[:, -1]


def _ffn_up_sample_kernel(x_ref, g_ref, wa_ref, wv_ref, cw_ref, cb_ref, p1_ref, p2_ref,
                          u_ref, c6_ref, c7_ref, hb_ref, a_ref, *, tm, seq):
    @pl.when(pl.program_id(1) == 0)
    def _():
        _norm_to_scratch(x_ref, g_ref[...], hb_ref, tm)

    a = _dot(hb_ref[...], wa_ref[...])
    v = _dot(hb_ref[...], wv_ref[...])
    t = lax.broadcasted_iota(jnp.int32, (tm, 1), 0) % seq
    a1 = jnp.where(t >= 1, pltpu.roll(a, 1, 0), p1_ref[...])
    a2 = jnp.where(t >= 2, pltpu.roll(a, 2, 0), p2_ref[...])
    c = cb_ref[...] + a2 * cw_ref[0:1, :] + a1 * cw_ref[1:2, :] + a * cw_ref[2:3, :]
    u_ref[...] = _gelu_gate(c, v)
    for q in range(a.shape[1] // LANES):
        sl = slice(q * LANES, (q + 1) * LANES)
        a_ref[q] = a[:, sl]
        c6_ref[:, sl] = a_ref[q, pl.ds(seq - 2, tm // seq, stride=seq), :]
        c7_ref[:, sl] = a_ref[q, pl.ds(seq - 1, tm // seq, stride=seq), :]


def ffn_up_sample(x, row0, g, w_up, layer, cw, cb, p1, p2, seq, tm, tn):
    d = x.shape[1]
    m = p1.shape[0]
    nf = D_FF // tn
    nb = tm // seq
    t0 = row0 // tm
    w_arr, wa_spec = _w_spec((w_up, layer), tn)
    _, wv_spec = _w_spec((w_up, layer), tn, nf)
    return pl.pallas_call(
        functools.partial(_ffn_up_sample_kernel, tm=tm, seq=seq),
        grid=(m // tm, nf),
        in_specs=[pl.BlockSpec((tm, d), lambda i, j: (t0 + i, 0)),
                  pl.BlockSpec((1, d), lambda i, j: (0, 0)),
                  wa_spec, wv_spec,
                  pl.BlockSpec((3, tn), lambda i, j: (0, j)),
                  pl.BlockSpec((1, tn), lambda i, j: (0, j)),
                  pl.BlockSpec((tm, tn), lambda i, j: (i, j)),
                  pl.BlockSpec((tm, tn), lambda i, j: (i, j))],
        out_specs=[pl.BlockSpec((tm, tn), lambda i, j: (i, j)),
                   pl.BlockSpec((nb, tn), lambda i, j: (i, j)),
                   pl.BlockSpec((nb, tn), lambda i, j: (i, j))],
        out_shape=[jax.ShapeDtypeStruct((m, D_FF), BF16),
                   jax.ShapeDtypeStruct((m // seq, D_FF), F32),
                   jax.ShapeDtypeStruct((m // seq, D_FF), F32)],
        scratch_shapes=[pltpu.VMEM((tm, d), BF16), pltpu.VMEM((tn // LANES, tm, LANES), F32)],
        compiler_params=_cparams("parallel", "arbitrary"),
        name="ffn_up_sample",
    )(x, g.reshape(1, d), w_arr, w_arr, cw, cb.reshape(1, D_FF), p1, p2)


def _seg_sum(x, e, two_pass):
    parts = []
    for j in range(x.shape[1] // LANES):
        xj = x[:, j * LANES:(j + 1) * LANES]
        if two_pass:
            hi, lo = _split(xj)
            parts.append(_dot(hi, e) + _dot(lo, e))
        else:
            parts.append(_dot(_bf(xj), e))
    return jnp.concatenate(parts, axis=1)


def _bcast_rows(x, idx, nb, c):
    parts = [jnp.broadcast_to(x[b * c + idx:b * c + idx + 1, :], (c, x.shape[1])) for b in range(nb)]
    return parts[0] if nb == 1 else jnp.concatenate(parts, axis=0)


def _unit_lower_solve(a_list, rhs_list, c):
    mm = lambda x, y: _dot(_bf(x), _bf(y))
    n = range(len(a_list))
    rows = lax.broadcasted_iota(jnp.int32, (c, c), 0)
    cols = lax.broadcasted_iota(jnp.int32, (c, c), 1)
    eye = (rows == cols).astype(F32)
    blk = min(c, 16)
    if c > blk:
        assert c // blk <= 4
        same = (rows // blk) == (cols // blk)
        ad = [jnp.where(same, a, 0.0) for a in a_list]
        ao = [a_list[i] - ad[i] for i in n]
    else:
        ad = a_list
    t = [eye - ad[i] for i in n]
    pw = ad
    span = 2
    while span < blk:
        pw = [mm(pw[i], pw[i]) for i in n]
        t = [t[i] + mm(t[i], pw[i]) for i in n]
        span *= 2
    x = [mm(t[i], rhs_list[i]) for i in n]
    if c > blk:
        nn = [mm(t[i], ao[i]) for i in n]
        n2 = [mm(nn[i], nn[i]) for i in n]
        x = [x[i] + mm(n2[i], x[i]) for i in n]
        x = [x[i] - mm(nn[i], x[i]) for i in n]
    return x


def _rwkv_kernel(p_ref, ps_ref, s0_ref, mu_ref, vec_ref, w2_ref, a2_ref, g2_ref, e_ref, lt_ref,
                 tok_ref, sout_ref, prev_ref, s_ref, y_ref, *, c, nb):
    ci = pl.program_id(1)
    tw = TOK_WIDTH
    lora0 = 3 * tw

    @pl.when(ci == 0)
    def _():
        for b in range(nb):
            s_ref[b * A_HEADS:(b + 1) * A_HEADS] = s0_ref[b]
        prev_ref[...] = ps_ref[:, :, MEM_WIDTH:]

    p = p_ref[:, MEM_WIDTH:]
    rows = lax.broadcasted_iota(jnp.int32, (nb * c, 1), 0)
    p_prev = pltpu.roll(p, 1, 0)
    for b in range(nb):
        p_prev = jnp.where(rows == b * c, prev_ref[b], p_prev)
        prev_ref[b] = p_ref[b * c + c - 1:b * c + c, MEM_WIDTH:]
    xm = p + mu_ref[...] * (p_prev - p)

    r = xm[:, 0:tw]
    k = xm[:, tw:2 * tw]
    v = xm[:, 2 * tw:3 * tw]
    xw = xm[:, lora0:lora0 + 128]
    xa = xm[:, lora0:lora0 + 256]
    xg = xm[:, lora0 + 128:lora0 + 512]
    w0, a0, k_k, k_a, r_k, ln_w, ln_b = (vec_ref[i:i + 1, :] for i in range(7))

    z = -(w0 + _dot(_bf(jnp.tanh(xw)), w2_ref[...]))
    softplus = jnp.maximum(z, 0.0) + jnp.log(1.0 + jnp.exp(-jnp.abs(z)))
    ell = -jnp.exp(-softplus - 0.5)
    a = _sigmoid(a0 + _dot(_bf(xa), a2_ref[...]))
    gate = _dot(_bf(_sigmoid(xg)), g2_ref[...])
    kkraw = k * k_k
    k2 = k * (1.0 + (a - 1.0) * k_a)
    e = e_ref[...]
    kap = kkraw * lax.rsqrt(jnp.maximum(_seg_sum(kkraw * kkraw, e, True), 1e-24))
    bet = kap * a

    ell_hi, ell_lo = _split(ell)
    gc = _dot(lt_ref[...], ell_hi) + _dot(lt_ref[...], ell_lo)
    glast = _bcast_rows(gc, c - 1, nb, c)
    egi = jnp.exp(-gc)
    el = jnp.exp(glast - gc)
    xk = kap * jnp.exp(gc - ell)
    xr = r * jnp.exp(gc)
    kb = k2 * egi
    bb = bet * egi
    kh = k2 * el
    bh = bet * el
    gam = jnp.exp(glast)
    ti = lax.broadcasted_iota(jnp.int32, (c, c), 0)
    si = lax.broadcasted_iota(jnp.int32, (c, c), 1)
    strict = si < ti
    incl = si <= ti

    ch = [(b, h) for b in range(nb) for h in range(A_HEADS)]
    n = range(len(ch))
    rs = [slice(b * c, (b + 1) * c) for b, _ in ch]
    ls = [slice(h * A_HEAD_DIM, (h + 1) * A_HEAD_DIM) for _, h in ch]
    s_old = [s_ref[i] for i in n]
    sb = [_bf(s) for s in s_old]
    xk_h = [_bf(xk[rs[i], ls[i]]) for i in n]
    xr_h = [_bf(xr[rs[i], ls[i]]) for i in n]
    kb_h = [_bf(kb[rs[i], ls[i]]) for i in n]
    bb_h = [_bf(bb[rs[i], ls[i]]) for i in n]
    v_f = [v[rs[i], ls[i]] for i in n]
    v_h = [_bf(x) for x in v_f]
    a_kk = [jnp.where(strict, _dot_nt(xk_h[i], kb_h[i]), 0.0) for i in n]
    a_kb = [jnp.where(strict, _dot_nt(xk_h[i], bb_h[i]), 0.0) for i in n]
    a_rk = [jnp.where(incl, _dot_nt(xr_h[i], kb_h[i]), 0.0) for i in n]
    a_rb = [jnp.where(incl, _dot_nt(xr_h[i], bb_h[i]), 0.0) for i in n]
    rhs = [-(_dot_nt(xk_h[i], sb[i]) + _dot(_bf(a_kk[i]), v_h[i])) for i in n]
    u = _unit_lower_solve(a_kb, rhs, c)
    u_h = [_bf(x) for x in u]
    y_h = [_dot_nt(xr_h[i], sb[i]) + _dot(_bf(a_rk[i]), v_h[i]) + _dot(_bf(a_rb[i]), u_h[i]) for i in n]
    vu = [_bf(jnp.concatenate([v_f[i], u[i]], axis=0)) for i in n]
    kbh = [_bf(jnp.concatenate([kh[rs[i], ls[i]], bh[rs[i], ls[i]]], axis=0)) for i in n]
    s_new = [s_old[i] * gam[ch[i][0] * c:ch[i][0] * c + 1, ls[i]] + _dot_tn(vu[i], kbh[i]) for i in n]
    for i in n:
        y_ref[rs[i], ls[i]] = y_h[i]
        s_ref[i] = s_new[i]

    y = y_ref[...]
    inv_n = 1.0 / A_HEAD_DIM
    mean = _seg_sum(y, e, False) * inv_n
    d = y - mean
    var = _seg_sum(d * d, e, False) * inv_n
    yn = d * lax.rsqrt(var + GN_EPS) * ln_w + ln_b
    bonus = _seg_sum(r * k2 * r_k, e, False) * v
    tok_ref[...] = ((yn + bonus) * gate).astype(tok_ref.dtype)

    @pl.when(ci == pl.num_programs(1) - 1)
    def _():
        for b in range(nb):
            sout_ref[b] = s_ref[b * A_HEADS:(b + 1) * A_HEADS]


def _block_ltri(nb, c):
    t = jnp.arange(nb * c)
    return ((t[None, :] <= t[:, None]) & (t[None, :] // c == t[:, None] // c)).astype(BF16)


def rwkv_mix(p, row0, pshift, s0, mu, vecs, w2, a2, g2, n_batch, seq, c, nb):
    assert nb == 1 or (seq == c and c == 8)
    n_chunks = seq // c
    rows = nb * c
    t0 = row0 // rows
    lane = jnp.arange(LANES)
    e = (lane[:, None] // A_HEAD_DIM == lane[None, :] // A_HEAD_DIM).astype(BF16)
    const = lambda shape: pl.BlockSpec(shape, lambda b, ci: (0,) * len(shape))
    st_spec = pl.BlockSpec((nb, A_HEADS, A_HEAD_DIM, A_HEAD_DIM), lambda b, ci: (b, 0, 0, 0))
    return pl.pallas_call(
        functools.partial(_rwkv_kernel, c=c, nb=nb),
        grid=(n_batch // nb, n_chunks),
        in_specs=[pl.BlockSpec((rows, A_PROJ_COLS), lambda b, ci: (t0 + b * n_chunks + ci, 0)),
                  pl.BlockSpec((nb, 1, A_PROJ_COLS), lambda b, ci: (b, 0, 0)),
                  st_spec,
                  const((1, A_TOK_COLS)), const((8, TOK_WIDTH)),
                  const(w2.shape), const(a2.shape), const(g2.shape), const((LANES, LANES)), const((rows, rows))],
        out_specs=[pl.BlockSpec((rows, TOK_WIDTH), lambda b, ci: (b * n_chunks + ci, 0)), st_spec],
        out_shape=[jax.ShapeDtypeStruct((n_batch * seq, TOK_WIDTH), BF16),
                   jax.ShapeDtypeStruct((n_batch, A_HEADS, A_HEAD_DIM, A_HEAD_DIM), F32)],
        scratch_shapes=[pltpu.VMEM((nb, 1, A_TOK_COLS), F32),
                        pltpu.VMEM((nb * A_HEADS, A_HEAD_DIM, A_HEAD_DIM), F32),
                        pltpu.VMEM((rows, TOK_WIDTH), F32)],
        compiler_params=_cparams("parallel", "arbitrary"),
        name="rwkv_mix",
    )(p, pshift, s0, mu, vecs, w2, a2, g2, e, _block_ltri(nb, c))


def _hgrn_kernel(p_ref, s0_ref, lbp_ref, gn_ref, lt_ref, tok_ref, sout_ref, st_ref, *, c, nb, layer, col0):
    ci = pl.program_id(1)
    tw = TOK_WIDTH
    hd = B_HEAD_DIM
    ch = [(b, h) for b in range(nb) for h in range(B_HEADS)]
    n = range(len(ch))

    @pl.when(ci == 0)
    def _():
        for i in n:
            st_ref[i] = s0_ref[ch[i][0], ch[i][1]].T

    lbp = lbp_ref[...]
    mx = jnp.max(lbp, axis=0, keepdims=True)
    ex = jnp.exp(lbp - mx)
    den = jnp.sum(ex, axis=0, keepdims=True)
    lb = jnp.zeros((1, tw), F32)
    for i in range(1, layer + 1):
        lb = lb + ex[i:i + 1, :] / den

    q = p_ref[:, col0:col0 + tw]
    f = p_ref[:, col0 + tw:col0 + 2 * tw]
    iv = p_ref[:, col0 + 2 * tw:col0 + 3 * tw]
    og = p_ref[:, col0 + 3 * tw:col0 + 4 * tw]
    fg = lb + (1.0 - lb) * _sigmoid(f)
    lf = jnp.log(fg)
    kk = 1.0 - fg
    qq = q * _sigmoid(q)
    lf_hi, lf_lo = _split(lf)
    bc = _dot(lt_ref[...], lf_hi) + _dot(lt_ref[...], lf_lo)
    blast = _bcast_rows(bc, c - 1, nb, c)
    mid = _bcast_rows(bc, (c - 1) // 2, nb, c)
    qe = qq * jnp.exp(bc - mid)
    ke = kk * jnp.exp(mid - bc)
    qs = qq * jnp.exp(bc)
    kl = kk * jnp.exp(blast - bc)
    gam = jnp.exp(blast)
    gate = og * _sigmoid(og)
    ti = lax.broadcasted_iota(jnp.int32, (c, c), 0)
    si = lax.broadcasted_iota(jnp.int32, (c, c), 1)
    incl = si <= ti
    gn = gn_ref[...]
    rs = [slice(b * c, (b + 1) * c) for b, _ in ch]
    ls = [slice(h * hd, (h + 1) * hd) for _, h in ch]
    st = [st_ref[i] for i in n]
    v_h = [_bf(iv[rs[i], ls[i]]) for i in n]
    att = [jnp.where(incl, _dot_nt(_bf(qe[rs[i], ls[i]]), _bf(ke[rs[i], ls[i]])), 0.0) for i in n]
    o = [_dot(_bf(att[i]), v_h[i]) + _dot_nt(_bf(qs[rs[i], ls[i]]), _bf(st[i])) for i in n]
    st_new = [st[i] * gam[ch[i][0] * c:ch[i][0] * c + 1, ls[i]] + _dot_tn(v_h[i], _bf(kl[rs[i], ls[i]])) for i in n]
    on = [x * lax.rsqrt(jnp.mean(x * x, axis=-1, keepdims=True) + RMS_EPS) * gn for x in o]
    for i in n:
        tok_ref[rs[i], ls[i]] = (on[i] * gate[rs[i], ls[i]]).astype(tok_ref.dtype)
        st_ref[i] = st_new[i]

    @pl.when(ci == pl.num_programs(1) - 1)
    def _():
        for i in n:
            sout_ref[ch[i][0], ch[i][1]] = st_ref[i].T


def hgrn_mix(p, row0, col0, s0, lbp, gn, n_batch, seq, c, nb, layer):
    assert nb == 1 or (seq == c and c == 8)
    cols = p.shape[1]
    n_chunks = seq // c
    rows = nb * c
    t0 = row0 // rows
    const = lambda shape: pl.BlockSpec(shape, lambda b, ci: (0,) * len(shape))
    st_spec = pl.BlockSpec((nb, B_HEADS, B_HEAD_DIM, B_HEAD_DIM), lambda b, ci: (b, 0, 0, 0))
    return pl.pallas_call(
        functools.partial(_hgrn_kernel, c=c, nb=nb, layer=layer, col0=col0),
        grid=(n_batch // nb, n_chunks),
        in_specs=[pl.BlockSpec((rows, cols), lambda b, ci: (t0 + b * n_chunks + ci, 0)),
                  st_spec, const((DEPTH, TOK_WIDTH)), const((1, B_HEAD_DIM)), const((rows, rows))],
        out_specs=[pl.BlockSpec((rows, TOK_WIDTH), lambda b, ci: (b * n_chunks + ci, 0)), st_spec],
        out_shape=[jax.ShapeDtypeStruct((n_batch * seq, TOK_WIDTH), BF16),
                   jax.ShapeDtypeStruct((n_batch, B_HEADS, B_HEAD_DIM, B_HEAD_DIM), F32)],
        scratch_shapes=[pltpu.VMEM((nb * B_HEADS, B_HEAD_DIM, B_HEAD_DIM), F32)],
        compiler_params=_cparams("parallel", "arbitrary"),
        name="hgrn_mix",
    )(p, s0, lbp, gn.reshape(1, B_HEAD_DIM), _block_ltri(nb, c))


PROMPT_CHUNK = 64
SAMPLE_SEQS_PER_STEP = 4
TM_DENSE = 1024


def kernel(x_prompt, x_sample, mem_prompt, cache_mem_k, cache_mem_v, state_rwkv, state_shift, state_hgrn, state_conv, norm_mix, norm_ffn, norm_final, mem_norm, w_mem_kv, a_w_in, a_mu, a_w0, a_w2, a_a0, a_a2, a_g2, a_k_k, a_k_a, a_r_k, a_ln_w, a_ln_b, a_w_out, b_w_in, b_lower_bounds, b_g_norm, b_w_out, ffn_w_up, ffn_conv_w, ffn_conv_b, ffn_w_down):
    bp, sp, d = x_prompt.shape
    bs, ss, _ = x_sample.shape
    mp, ms = bp * sp, bs * ss
    nbs = SAMPLE_SEQS_PER_STEP

    pad_cols = A_PROJ_COLS - a_w_in.shape[2]
    wa = _bf(jnp.pad(a_w_in[0], ((0, 0), (0, pad_cols))))
    mu = jnp.pad(a_mu[0], (0, pad_cols)).reshape(1, A_TOK_COLS)
    vecs = jnp.stack([a_w0[0], a_a0[0], a_k_k[0], a_k_a[0], a_r_k[0], a_ln_w[0], a_ln_b[0],
                      jnp.zeros((TOK_WIDTH,), F32)])
    w2 = _bf(jnp.pad(a_w2[0], ((0, 128 - A_LORA), (0, 0))))
    a2 = _bf(jnp.pad(a_a2[0], ((A_LORA, 256 - 2 * A_LORA), (0, 0))))
    g2 = _bf(jnp.pad(a_g2[0], ((2 * A_LORA - 128, 384 - (2 * A_LORA - 128) - A_GATE_RANK), (0, 0))))
    wb = _bf(b_w_in[0])
    w_out = [_bf(a_w_out[0]), _bf(b_w_out[0])]
    w_up = _bf(ffn_w_up)
    w_down = _bf(ffn_w_down)

    mem_rows = mem_prompt.reshape(bp * N_MEM, d)
    mem_kv = [norm_matmul([mem_rows], mem_norm[l], _bf(w_mem_kv[l]), bp * N_MEM, 512) for l in range(DEPTH)]
    mem_k_prompt = jnp.stack([kv[:, :MEM_WIDTH] for kv in mem_kv]).reshape(DEPTH, bp, N_MEM, MEM_HEADS, MEM_HEAD_DIM)
    mem_v_prompt = jnp.stack([kv[:, MEM_WIDTH:] for kv in mem_kv]).reshape(DEPTH, bp, N_MEM, MEM_HEADS, MEM_HEAD_DIM)

    hist = []
    for l in range(DEPTH):
        h = state_conv[l]
        p1 = jnp.pad(h[:, 1:2], ((0, 0), (0, ss - 1), (0, 0))).reshape(ms, D_FF)
        p2 = jnp.pad(h, ((0, 0), (0, ss - 2), (0, 0))).reshape(ms, D_FF)
        hist.append((p1, p2))

    def ffn(x, layer):
        u_p, cs_p = ffn_up_prompt(x, norm_ffn[layer], w_up, layer, ffn_conv_w[layer], ffn_conv_b[layer],
                                  bp, sp, 1024, 512)
        u_s, c6, c7 = ffn_up_sample(x, mp, norm_ffn[layer], w_up, layer, ffn_conv_w[layer], ffn_conv_b[layer],
                                    hist[layer][0], hist[layer][1], ss, 512, 512)
        x = matmul_acc([u_p, u_s], w_down, layer, x, TM_DENSE, 1024, D_FF // 4)
        return x, cs_p, jnp.stack([c6, c7], axis=1)

    def out_proj(tok, mem_o, w, res_segs):
        return matmul([tok, mem_o], [w[:TOK_WIDTH], w[TOK_WIDTH:]], res_segs, D_MODEL, TM_DENSE, 1024)

    x_segs = [x_prompt.reshape(mp, d), x_sample.reshape(ms, d)]
    p = norm_matmul(x_segs, norm_mix[0], wa, TM_DENSE, 512)
    pshift = matmul([[state_shift[0]]], [wa], None, A_PROJ_COLS, bs, 512).reshape(bs, 1, A_PROJ_COLS)
    tok_p, rwkv_p = rwkv_mix(p, 0, jnp.zeros((bp, 1, A_PROJ_COLS), F32),
                             jnp.zeros((bp, A_HEADS, A_HEAD_DIM, A_HEAD_DIM), F32),
                             mu, vecs, w2, a2, g2, bp, sp, PROMPT_CHUNK, 1)
    tok_s, rwkv_s = rwkv_mix(p, mp, pshift, state_rwkv[0], mu, vecs, w2, a2, g2, bs, ss, ss, nbs)
    mo_p = mem_attention(p, 0, 0, mem_kv[0], bp, sp, 512)
    mo_s = mem_attention_cache(p, mp, 0, cache_mem_k, cache_mem_v, 0, ss, 8)
    x = out_proj([tok_p, tok_s], [mo_p, mo_s], w_out[0], x_segs)
    x, conv_p0, conv_s0 = ffn(x, 0)
    p = norm_matmul([x], norm_mix[1], wb, TM_DENSE, 512)
    tok_p, hgrn_p = hgrn_mix(p, 0, MEM_WIDTH, jnp.zeros((bp, B_HEADS, B_HEAD_DIM, B_HEAD_DIM), F32),
                             b_lower_bounds, b_g_norm[0], bp, sp, PROMPT_CHUNK, 1, 1)
    tok_s, hgrn_s = hgrn_mix(p, mp, MEM_WIDTH, state_hgrn[0], b_lower_bounds, b_g_norm[0], bs, ss, ss, nbs, 1)
    mo_p = mem_attention(p, 0, 0, mem_kv[1], bp, sp, 512)
    mo_s = mem_attention_cache(p, mp, 0, cache_mem_k, cache_mem_v, 1, ss, 8)
    x = out_proj([tok_p, tok_s], [mo_p, mo_s], w_out[1], [x])
    x, conv_p1, conv_s1 = ffn(x, 1)
    y_p = rmsnorm(x, norm_final, 256, 0, mp)
    y_s = rmsnorm(x, norm_final, 256, mp, ms)
    shift_p = rmsnorm(x_prompt[:, -1], norm_mix[0], bp)
    shift_s = rmsnorm(x_sample[:, -1], norm_mix[0], bs)

    return (y_p.reshape(bp, sp, d), y_s.reshape(bs, ss, d), mem_k_prompt, mem_v_prompt,
            rwkv_p[None], rwkv_s[None], shift_p[None], shift_s[None], hgrn_p[None], hgrn_s[None],
            jnp.stack([conv_p0, conv_p1]), jnp.stack([conv_s0, conv_s1]))
```

```python
import functools

import jax
import jax.numpy as jnp
from jax import lax
from jax.experimental import pallas as pl
from jax.experimental.pallas import tpu as pltpu

F32 = jnp.float32
BF16 = jnp.bfloat16

D_MODEL = 2048
DEPTH = 2
TOK_WIDTH = 1536
MEM_WIDTH = 512
MEM_HEADS = 4
MEM_HEAD_DIM = 128
N_MEM = 256
A_HEAD_DIM = 64
A_HEADS = 24
A_LORA = 96
A_GATE_RANK = 256
A_TOK_COLS = 5120
A_PROJ_COLS = MEM_WIDTH + A_TOK_COLS
B_HEADS = 12
B_HEAD_DIM = 128
D_FF = 5632
RMS_EPS = 1e-6
GN_EPS = 64e-5

LANES = 128
VMEM_LIMIT = 56 * 1024 * 1024


def _cparams(*sem):
    return pltpu.CompilerParams(dimension_semantics=sem, vmem_limit_bytes=VMEM_LIMIT)


def _dot(a, b):
    return jnp.dot(a, b, preferred_element_type=F32)


def _dot_nt(a, b):
    return lax.dot_general(a, b, (((1,), (1,)), ((), ())), preferred_element_type=F32)


def _dot_tn(a, b):
    return lax.dot_general(a, b, (((0,), (0,)), ((), ())), preferred_element_type=F32)


def _bf(x):
    return x.astype(BF16)


def _split(x):
    hi = x.astype(BF16)
    lo = (x - hi.astype(F32)).astype(BF16)
    return hi, lo


def _sigmoid(x):
    return 1.0 / (1.0 + jnp.exp(-x))


def _rms(x, g):
    ms = jnp.mean(x * x, axis=-1, keepdims=True)
    return x * lax.rsqrt(ms + RMS_EPS) * g


def _seg_bounds(segs, tm):
    bounds, off = [], 0
    for a in segs:
        bounds.append(off)
        off += a.shape[0] // tm
    return tuple(bounds), off


def _seg_row_specs(segs, tm, width, col_of_j, tile0=0):
    assert tile0 == 0 or len(segs) == 1
    specs, off = [], 0
    for a in segs:
        n = a.shape[0] // tm
        if col_of_j:
            imap = lambda i, j, off=off, n=n: (jnp.clip(i - off + tile0, 0, n - 1), j)
        else:
            imap = lambda i, j, off=off, n=n: (jnp.clip(i - off + tile0, 0, n - 1), 0)
        specs.append(pl.BlockSpec((tm, width), imap))
        off += n
    return specs


def _for_segment(i, bounds, fn):
    if len(bounds) == 1:
        fn(0)
        return
    for s in range(len(bounds)):
        cond = i >= bounds[s]
        if s + 1 < len(bounds):
            cond = jnp.logical_and(cond, i < bounds[s + 1])
        pl.when(cond)(functools.partial(fn, s))


def _w_spec(w, tn, col_block0=0):
    if isinstance(w, tuple):
        arr, layer = w
        return arr, pl.BlockSpec((None, arr.shape[1], tn), lambda i, j: (layer, 0, col_block0 + j))
    return w, pl.BlockSpec((w.shape[0], tn), lambda i, j: (0, col_block0 + j))


def _rmsnorm_kernel(x_ref, g_ref, o_ref):
    o_ref[...] = _rms(x_ref[...], g_ref[...])


def rmsnorm(x, g, tm, row0=0, rows=None):
    d = x.shape[1]
    rows = x.shape[0] if rows is None else rows
    t0 = row0 // tm
    return pl.pallas_call(
        _rmsnorm_kernel,
        grid=(rows // tm,),
        in_specs=[pl.BlockSpec((tm, d), lambda i: (t0 + i, 0)), pl.BlockSpec((1, d), lambda i: (0, 0))],
        out_specs=pl.BlockSpec((tm, d), lambda i: (i, 0)),
        out_shape=jax.ShapeDtypeStruct((rows, d), F32),
        compiler_params=_cparams("parallel"),
        name="rmsnorm",
    )(x, g.reshape(1, d))


NORM_ROWS = 256


def _norm_to_scratch(x_ref, g, hb_ref, tm, dst_off=0):
    for r in range(0, tm, NORM_ROWS):
        n = min(NORM_ROWS, tm - r)
        hb_ref[dst_off + r:dst_off + r + n, :] = _bf(_rms(x_ref[r:r + n, :], g))


def _norm_matmul_kernel(*refs, n_seg, bounds, tm):
    x_refs = refs[:n_seg]
    g_ref, w_ref, o_ref, hb_ref = refs[n_seg:]

    @pl.when(pl.program_id(1) == 0)
    def _():
        g = g_ref[...]
        _for_segment(pl.program_id(0), bounds, lambda s: _norm_to_scratch(x_refs[s], g, hb_ref, tm))

    o_ref[...] = _dot(hb_ref[...], w_ref[...]).astype(o_ref.dtype)


def norm_matmul(x_segs, g, w, tm, tn, out_dtype=F32):
    d = x_segs[0].shape[1]
    bounds, n_tiles = _seg_bounds(x_segs, tm)
    w_arr, w_spec = _w_spec(w, tn)
    n = w_arr.shape[-1]
    return pl.pallas_call(
        functools.partial(_norm_matmul_kernel, n_seg=len(x_segs), bounds=bounds, tm=tm),
        grid=(n_tiles, n // tn),
        in_specs=_seg_row_specs(x_segs, tm, d, False) + [pl.BlockSpec((1, d), lambda i, j: (0, 0)), w_spec],
        out_specs=pl.BlockSpec((tm, tn), lambda i, j: (i, j)),
        out_shape=jax.ShapeDtypeStruct((n_tiles * tm, n), out_dtype),
        scratch_shapes=[pltpu.VMEM((tm, d), BF16)],
        compiler_params=_cparams("parallel", "arbitrary"),
        name="norm_matmul",
    )(*x_segs, g.reshape(1, d), w_arr)


def _mm_kernel(*refs, a_counts, res_count, bounds):
    pos = 0
    a_refs = []
    for cnt in a_counts:
        a_refs.append(refs[pos:pos + cnt])
        pos += cnt
    w_refs = refs[pos:pos + len(a_counts)]
    pos += len(a_counts)
    res_refs = refs[pos:pos + res_count]
    o_ref = refs[-1]

    def compute(s):
        acc = None
        for k, segs in enumerate(a_refs):
            term = _dot(_bf(segs[min(s, len(segs) - 1)][...]), w_refs[k][...])
            acc = term if acc is None else acc + term
        if res_count:
            acc = res_refs[min(s, res_count - 1)][...] + acc
        o_ref[...] = acc

    _for_segment(pl.program_id(0), bounds, compute)


def matmul(a_list, w_list, res_segs, n_out, tm, tn, res_row0=0):
    longest = max(a_list + ([res_segs] if res_segs else []), key=len)
    bounds, n_tiles = _seg_bounds(longest, tm)
    in_specs, args = [], []
    for segs in a_list:
        assert len(segs) in (1, len(bounds))
        in_specs += _seg_row_specs(segs, tm, segs[0].shape[1], False)
        args += list(segs)
    for w in w_list:
        w_arr, w_spec = _w_spec(w, tn)
        in_specs.append(w_spec)
        args.append(w_arr)
    if res_segs:
        assert len(res_segs) in (1, len(bounds))
        in_specs += _seg_row_specs(res_segs, tm, tn, True, res_row0 // tm)
        args += list(res_segs)
    return pl.pallas_call(
        functools.partial(_mm_kernel, a_counts=tuple(len(s) for s in a_list),
                          res_count=len(res_segs) if res_segs else 0, bounds=bounds),
        grid=(n_tiles, n_out // tn),
        in_specs=in_specs,
        out_specs=pl.BlockSpec((tm, tn), lambda i, j: (i, j)),
        out_shape=jax.ShapeDtypeStruct((n_tiles * tm, n_out), F32),
        compiler_params=_cparams("parallel", "parallel"),
        name="matmul",
    )(*args)


def _attn_kernel(q_ref, k_ref, v_ref, o_ref, *, n_seq, tq, cache_layout):
    scale = MEM_HEAD_DIM ** -0.5
    pairs = [(s, h) for s in range(n_seq) for h in range(MEM_HEADS)]
    hsl = [slice(h * MEM_HEAD_DIM, (h + 1) * MEM_HEAD_DIM) for h in range(MEM_HEADS)]
    q = [_bf(q_ref[s * tq:(s + 1) * tq, hsl[h]]) for s, h in pairs]
    if cache_layout:
        k = [_bf(k_ref[0, s, pl.ds(h, N_MEM, stride=MEM_HEADS), :]) for s, h in pairs]
        v = [_bf(v_ref[0, s, pl.ds(h, N_MEM, stride=MEM_HEADS), :]) for s, h in pairs]
    else:
        k = [_bf(k_ref[s * N_MEM:(s + 1) * N_MEM, hsl[h]]) for s, h in pairs]
        v = [_bf(v_ref[s * N_MEM:(s + 1) * N_MEM, hsl[h]]) for s, h in pairs]
    sc = [_dot_nt(q[i], k[i]) * scale for i in range(len(pairs))]
    e = [jnp.exp(x - jnp.max(x, axis=-1, keepdims=True)) for x in sc]
    p = [_bf(x / jnp.sum(x, axis=-1, keepdims=True)) for x in e]
    o = [_dot(p[i], v[i]) for i in range(len(pairs))]
    rows = [jnp.concatenate(o[s * MEM_HEADS:(s + 1) * MEM_HEADS], axis=1) for s in range(n_seq)]
    o_ref[...] = _bf(jnp.concatenate(rows, axis=0) if n_seq > 1 else rows[0])


def mem_attention(p, row0, q_colblock, kv, n_batch, seq, tq):
    q_tiles = seq // tq
    t0 = row0 // tq
    return pl.pallas_call(
        functools.partial(_attn_kernel, n_seq=1, tq=tq, cache_layout=False),
        grid=(n_batch, q_tiles),
        in_specs=[pl.BlockSpec((tq, MEM_WIDTH), lambda b, t: (t0 + b * q_tiles + t, q_colblock)),
                  pl.BlockSpec((N_MEM, MEM_WIDTH), lambda b, t: (b, 0)),
                  pl.BlockSpec((N_MEM, MEM_WIDTH), lambda b, t: (b, 1))],
        out_specs=pl.BlockSpec((tq, MEM_WIDTH), lambda b, t: (b * q_tiles + t, 0)),
        out_shape=jax.ShapeDtypeStruct((n_batch * seq, MEM_WIDTH), BF16),
        compiler_params=_cparams("parallel", "parallel"),
        name="mem_attention",
    )(p, kv, kv)


def mem_attention_cache(p, row0, q_colblock, cache_k, cache_v, layer, seq, n_seq):
    depth, n_batch = cache_k.shape[:2]
    cache_k = cache_k.reshape(depth, n_batch, N_MEM * MEM_HEADS, MEM_HEAD_DIM)
    cache_v = cache_v.reshape(depth, n_batch, N_MEM * MEM_HEADS, MEM_HEAD_DIM)
    rows = n_seq * seq
    t0 = row0 // rows
    kv_spec = pl.BlockSpec((1, n_seq, N_MEM * MEM_HEADS, MEM_HEAD_DIM), lambda b: (layer, b, 0, 0))
    return pl.pallas_call(
        functools.partial(_attn_kernel, n_seq=n_seq, tq=seq, cache_layout=True),
        grid=(n_batch // n_seq,),
        in_specs=[pl.BlockSpec((rows, MEM_WIDTH), lambda b: (t0 + b, q_colblock)), kv_spec, kv_spec],
        out_specs=pl.BlockSpec((rows, MEM_WIDTH), lambda b: (b, 0)),
        out_shape=jax.ShapeDtypeStruct((n_batch * seq, MEM_WIDTH), BF16),
        compiler_params=_cparams("parallel"),
        name="mem_attention_cache",
    )(p, cache_k, cache_v)


FFN_HALO = 16


def _gelu_gate(c, v):
    return _bf(jax.nn.gelu(c) * v)


def _ffn_up_prompt_kernel(x_ref, xh_ref, g_ref, wa_ref, wv_ref, cw_ref, cb_ref, u_ref, cs_ref, hb_ref,
                          *, tm, tiles_per_seq):
    i = pl.program_id(0)

    @pl.when(pl.program_id(1) == 0)
    def _():
        g = g_ref[...]
        hb_ref[0:FFN_HALO, :] = _bf(_rms(xh_ref[...], g))
        _norm_to_scratch(x_ref, g, hb_ref, tm, dst_off=FFN_HALO)

    a_ext = _dot(hb_ref[...], wa_ref[...])
    v = _dot(hb_ref[FFN_HALO:, :], wv_ref[...])
    rows = lax.broadcasted_iota(jnp.int32, (FFN_HALO + tm, 1), 0)
    n_zero = jnp.where((i % tiles_per_seq) == 0, FFN_HALO, 0)
    a_ext = jnp.where(rows < n_zero, 0.0, a_ext)
    a0 = a_ext[FFN_HALO:]
    a1 = pltpu.roll(a_ext, 1, 0)[FFN_HALO:]
    a2 = pltpu.roll(a_ext, 2, 0)[FFN_HALO:]
    c = cb_ref[...] + a2 * cw_ref[0:1, :] + a1 * cw_ref[1:2, :] + a0 * cw_ref[2:3, :]
    u_ref[...] = _gelu_gate(c, v)
    cs_ref[0] = a0[tm - 8:tm][6:8]


def ffn_up_prompt(x, g, w_up, layer, cw, cb, n_batch, seq, tm, tn):
    d = x.shape[1]
    m = n_batch * seq
    nf = D_FF // tn
    tiles_per_seq = seq // tm
    halo_blocks = tm // FFN_HALO
    w_arr, wa_spec = _w_spec((w_up, layer), tn)
    _, wv_spec = _w_spec((w_up, layer), tn, nf)
    u, cs = pl.pallas_call(
        functools.partial(_ffn_up_prompt_kernel, tm=tm, tiles_per_seq=tiles_per_seq),
        grid=(m // tm, nf),
        in_specs=[pl.BlockSpec((tm, d), lambda i, j: (i, 0)),
                  pl.BlockSpec((FFN_HALO, d), lambda i, j: (jnp.maximum(i * halo_blocks - 1, 0), 0)),
                  pl.BlockSpec((1, d), lambda i, j: (0, 0)),
                  wa_spec, wv_spec,
                  pl.BlockSpec((3, tn), lambda i, j: (0, j)),
                  pl.BlockSpec((1, tn), lambda i, j: (0, j))],
        out_specs=[pl.BlockSpec((tm, tn), lambda i, j: (i, j)),
                   pl.BlockSpec((1, 2, tn), lambda i, j: (i, 0, j))],
        out_shape=[jax.ShapeDtypeStruct((m, D_FF), BF16),
                   jax.ShapeDtypeStruct((m // tm, 2, D_FF), F32)],
        scratch_shapes=[pltpu.VMEM((FFN_HALO + tm, d), BF16)],
        compiler_params=_cparams("parallel", "arbitrary"),
        name="ffn_up_prompt",
    )(x, x, g.reshape(1, d), w_arr, w_arr, cw, cb.reshape(1, D_FF))
    return u, cs.reshape(n_batch, tiles_per_seq, 2, D_FF)[:, -1]


def _ffn_up_sample_kernel(x_ref, g_ref, wa_ref, wv_ref, cw_ref, cb_ref, p1_ref, p2_ref,
                          u_ref, c6_ref, c7_ref, hb_ref, a_ref, *, tm, seq):
    @pl.when(pl.program_id(1) == 0)
    def _():
        _norm_to_scratch(x_ref, g_ref[...], hb_ref, tm)

    a = _dot(hb_ref[...], wa_ref[...])
    v = _dot(hb_ref[...], wv_ref[...])
    t = lax.broadcasted_iota(jnp.int32, (tm, 1), 0) % seq
    a1 = jnp.where(t >= 1, pltpu.roll(a, 1, 0), p1_ref[...])
    a2 = jnp.where(t >= 2, pltpu.roll(a, 2, 0), p2_ref[...])
    c = cb_ref[...] + a2 * cw_ref[0:1, :] + a1 * cw_ref[1:2, :] + a * cw_ref[2:3, :]
    u_ref[...] = _gelu_gate(c, v)
    for q in range(a.shape[1] // LANES):
        sl = slice(q * LANES, (q + 1) * LANES)
        a_ref[q] = a[:, sl]
        c6_ref[:, sl] = a_ref[q, pl.ds(seq - 2, tm // seq, stride=seq), :]
        c7_ref[:, sl] = a_ref[q, pl.ds(seq - 1, tm // seq, stride=seq), :]


def ffn_up_sample(x, row0, g, w_up, layer, cw, cb, p1, p2, seq, tm, tn):
    d = x.shape[1]
    m = p1.shape[0]
    nf = D_FF // tn
    nb = tm // seq
    t0 = row0 // tm
    w_arr, wa_spec = _w_spec((w_up, layer), tn)
    _, wv_spec = _w_spec((w_up, layer), tn, nf)
    return pl.pallas_call(
        functools.partial(_ffn_up_sample_kernel, tm=tm, seq=seq),
        grid=(m // tm, nf),
        in_specs=[pl.BlockSpec((tm, d), lambda i, j: (t0 + i, 0)),
                  pl.BlockSpec((1, d), lambda i, j: (0, 0)),
                  wa_spec, wv_spec,
                  pl.BlockSpec((3, tn), lambda i, j: (0, j)),
                  pl.BlockSpec((1, tn), lambda i, j: (0, j)),
                  pl.BlockSpec((tm, tn), lambda i, j: (i, j)),
                  pl.BlockSpec((tm, tn), lambda i, j: (i, j))],
        out_specs=[pl.BlockSpec((tm, tn), lambda i, j: (i, j)),
                   pl.BlockSpec((nb, tn), lambda i, j: (i, j)),
                   pl.BlockSpec((nb, tn), lambda i, j: (i, j))],
        out_shape=[jax.ShapeDtypeStruct((m, D_FF), BF16),
                   jax.ShapeDtypeStruct((m // seq, D_FF), F32),
                   jax.ShapeDtypeStruct((m // seq, D_FF), F32)],
        scratch_shapes=[pltpu.VMEM((tm, d), BF16), pltpu.VMEM((tn // LANES, tm, LANES), F32)],
        compiler_params=_cparams("parallel", "arbitrary"),
        name="ffn_up_sample",
    )(x, g.reshape(1, d), w_arr, w_arr, cw, cb.reshape(1, D_FF), p1, p2)


def _seg_sum(x, e, two_pass):
    parts = []
    for j in range(x.shape[1] // LANES):
        xj = x[:, j * LANES:(j + 1) * LANES]
        if two_pass:
            hi, lo = _split(xj)
            parts.append(_dot(hi, e) + _dot(lo, e))
        else:
            parts.append(_dot(_bf(xj), e))
    return jnp.concatenate(parts, axis=1)


def _bcast_rows(x, idx, nb, c):
    parts = [jnp.broadcast_to(x[b * c + idx:b * c + idx + 1, :], (c, x.shape[1])) for b in range(nb)]
    return parts[0] if nb == 1 else jnp.concatenate(parts, axis=0)


def _unit_lower_solve(a_list, rhs_list, c):
    mm = lambda x, y: _dot(_bf(x), _bf(y))
    n = range(len(a_list))
    rows = lax.broadcasted_iota(jnp.int32, (c, c), 0)
    cols = lax.broadcasted_iota(jnp.int32, (c, c), 1)
    eye = (rows == cols).astype(F32)
    blk = min(c, 16)
    if c > blk:
        assert c // blk <= 4
        same = (rows // blk) == (cols // blk)
        ad = [jnp.where(same, a, 0.0) for a in a_list]
        ao = [a_list[i] - ad[i] for i in n]
    else:
        ad = a_list
    t = [eye - ad[i] for i in n]
    pw = ad
    span = 2
    while span < blk:
        pw = [mm(pw[i], pw[i]) for i in n]
        t = [t[i] + mm(t[i], pw[i]) for i in n]
        span *= 2
    x = [mm(t[i], rhs_list[i]) for i in n]
    if c > blk:
        nn = [mm(t[i], ao[i]) for i in n]
        n2 = [mm(nn[i], nn[i]) for i in n]
        x = [x[i] + mm(n2[i], x[i]) for i in n]
        x = [x[i] - mm(nn[i], x[i]) for i in n]
    return x


def _rwkv_chains(n, ch, c, s_ref, y_ref, xk, xr, kb, bb, v, kh, bh, gam, strict, incl):
    rs = {i: slice(ch[i][0] * c, (ch[i][0] + 1) * c) for i in n}
    ls = {i: slice(ch[i][1] * A_HEAD_DIM, (ch[i][1] + 1) * A_HEAD_DIM) for i in n}
    s_old = {i: s_ref[i] for i in n}
    sb = {i: _bf(s_old[i]) for i in n}
    xk_h = {i: _bf(xk[rs[i], ls[i]]) for i in n}
    xr_h = {i: _bf(xr[rs[i], ls[i]]) for i in n}
    kb_h = {i: _bf(kb[rs[i], ls[i]]) for i in n}
    bb_h = {i: _bf(bb[rs[i], ls[i]]) for i in n}
    v_f = {i: v[rs[i], ls[i]] for i in n}
    v_h = {i: _bf(v_f[i]) for i in n}
    a_kk = {i: jnp.where(strict, _dot_nt(xk_h[i], kb_h[i]), 0.0) for i in n}
    a_kb = {i: jnp.where(strict, _dot_nt(xk_h[i], bb_h[i]), 0.0) for i in n}
    a_rk = {i: jnp.where(incl, _dot_nt(xr_h[i], kb_h[i]), 0.0) for i in n}
    a_rb = {i: jnp.where(incl, _dot_nt(xr_h[i], bb_h[i]), 0.0) for i in n}
    rhs = {i: -(_dot_nt(xk_h[i], sb[i]) + _dot(_bf(a_kk[i]), v_h[i])) for i in n}
    u = dict(zip(n, _unit_lower_solve([a_kb[i] for i in n], [rhs[i] for i in n], c)))
    u_h = {i: _bf(u[i]) for i in n}
    y_h = {i: _dot_nt(xr_h[i], sb[i]) + _dot(_bf(a_rk[i]), v_h[i]) + _dot(_bf(a_rb[i]), u_h[i]) for i in n}
    vu = {i: _bf(jnp.concatenate([v_f[i], u[i]], axis=0)) for i in n}
    kbh = {i: _bf(jnp.concatenate([kh[rs[i], ls[i]], bh[rs[i], ls[i]]], axis=0)) for i in n}
    s_new = {i: s_old[i] * gam[ch[i][0] * c:ch[i][0] * c + 1, ls[i]] + _dot_tn(vu[i], kbh[i]) for i in n}
    for i in n:
        y_ref[rs[i], ls[i]] = y_h[i]
        s_ref[i] = s_new[i]


def _rwkv_kernel(p_ref, ps_ref, s0_ref, mu_ref, vec_ref, w2_ref, a2_ref, g2_ref, e_ref, lt_ref,
                 tok_ref, sout_ref, prev_ref, s_ref, y_ref, *, c, nb):
    ci = pl.program_id(1)
    tw = TOK_WIDTH
    lora0 = 3 * tw

    @pl.when(ci == 0)
    def _():
        for b in range(nb):
            s_ref[b * A_HEADS:(b + 1) * A_HEADS] = s0_ref[b]
        prev_ref[...] = ps_ref[:, :, MEM_WIDTH:]

    p = p_ref[:, MEM_WIDTH:].astype(F32)
    rows = lax.broadcasted_iota(jnp.int32, (nb * c, 1), 0)
    p_prev = pltpu.roll(p, 1, 0)
    for b in range(nb):
        p_prev = jnp.where(rows == b * c, prev_ref[b], p_prev)
        prev_ref[b] = p[b * c + c - 1:b * c + c, :]
    xm = p + mu_ref[...] * (p_prev - p)

    r = xm[:, 0:tw]
    k = xm[:, tw:2 * tw]
    v = xm[:, 2 * tw:3 * tw]
    xw = xm[:, lora0:lora0 + 128]
    xa = xm[:, lora0:lora0 + 256]
    xg = xm[:, lora0 + 128:lora0 + 512]
    w0, a0, k_k, k_a, r_k, ln_w, ln_b = (vec_ref[i:i + 1, :] for i in range(7))

    z = -(w0 + _dot(_bf(jnp.tanh(xw)), w2_ref[...]))
    softplus = jnp.maximum(z, 0.0) + jnp.log(1.0 + jnp.exp(-jnp.abs(z)))
    ell = -jnp.exp(-softplus - 0.5)
    a = _sigmoid(a0 + _dot(_bf(xa), a2_ref[...]))
    gate = _dot(_bf(_sigmoid(xg)), g2_ref[...])
    kkraw = k * k_k
    k2 = k * (1.0 + (a - 1.0) * k_a)
    e = e_ref[...]
    kap = kkraw * lax.rsqrt(jnp.maximum(_seg_sum(kkraw * kkraw, e, True), 1e-24))
    bet = kap * a

    ell_hi, ell_lo = _split(ell)
    gc = _dot(lt_ref[...], ell_hi) + _dot(lt_ref[...], ell_lo)
    glast = _bcast_rows(gc, c - 1, nb, c)
    egi = jnp.exp(-gc)
    el = jnp.exp(glast - gc)
    xk = kap * jnp.exp(gc - ell)
    xr = r * jnp.exp(gc)
    kb = k2 * egi
    bb = bet * egi
    kh = k2 * el
    bh = bet * el
    gam = jnp.exp(glast)
    ti = lax.broadcasted_iota(jnp.int32, (c, c), 0)
    si = lax.broadcasted_iota(jnp.int32, (c, c), 1)
    strict = si < ti
    incl = si <= ti

    ch = [(b, h) for b in range(nb) for h in range(A_HEADS)]
    for g0 in range(0, len(ch), A_CHAINS_PER_GROUP):
        _rwkv_chains(range(g0, min(g0 + A_CHAINS_PER_GROUP, len(ch))), ch, c, s_ref, y_ref,
                     xk, xr, kb, bb, v, kh, bh, gam, strict, incl)

    y = y_ref[...]
    inv_n = 1.0 / A_HEAD_DIM
    mean = _seg_sum(y, e, False) * inv_n
    d = y - mean
    var = _seg_sum(d * d, e, False) * inv_n
    yn = d * lax.rsqrt(var + GN_EPS) * ln_w + ln_b
    bonus = _seg_sum(r * k2 * r_k, e, False) * v
    tok_ref[...] = ((yn + bonus) * gate).astype(tok_ref.dtype)

    @pl.when(ci == pl.num_programs(1) - 1)
    def _():
        for b in range(nb):
            sout_ref[b] = s_ref[b * A_HEADS:(b + 1) * A_HEADS]


def _block_ltri(nb, c):
    t = jnp.arange(nb * c)
    return ((t[None, :] <= t[:, None]) & (t[None, :] // c == t[:, None] // c)).astype(BF16)


def rwkv_mix(p, row0, pshift, s0, mu, vecs, w2, a2, g2, n_batch, seq, c, nb):
    assert nb == 1 or (seq == c and c == 8)
    n_chunks = seq // c
    rows = nb * c
    t0 = row0 // rows
    lane = jnp.arange(LANES)
    e = (lane[:, None] // A_HEAD_DIM == lane[None, :] // A_HEAD_DIM).astype(BF16)
    const = lambda shape: pl.BlockSpec(shape, lambda b, ci: (0,) * len(shape))
    st_spec = pl.BlockSpec((nb, A_HEADS, A_HEAD_DIM, A_HEAD_DIM), lambda b, ci: (b, 0, 0, 0))
    return pl.pallas_call(
        functools.partial(_rwkv_kernel, c=c, nb=nb),
        grid=(n_batch // nb, n_chunks),
        in_specs=[pl.BlockSpec((rows, A_PROJ_COLS), lambda b, ci: (t0 + b * n_chunks + ci, 0)),
                  pl.BlockSpec((nb, 1, A_PROJ_COLS), lambda b, ci: (b, 0, 0)),
                  st_spec,
                  const((1, A_TOK_COLS)), const((8, TOK_WIDTH)),
                  const(w2.shape), const(a2.shape), const(g2.shape), const((LANES, LANES)), const((rows, rows))],
        out_specs=[pl.BlockSpec((rows, TOK_WIDTH), lambda b, ci: (b * n_chunks + ci, 0)), st_spec],
        out_shape=[jax.ShapeDtypeStruct((n_batch * seq, TOK_WIDTH), BF16),
                   jax.ShapeDtypeStruct((n_batch, A_HEADS, A_HEAD_DIM, A_HEAD_DIM), F32)],
        scratch_shapes=[pltpu.VMEM((nb, 1, A_TOK_COLS), F32),
                        pltpu.VMEM((nb * A_HEADS, A_HEAD_DIM, A_HEAD_DIM), F32),
                        pltpu.VMEM((rows, TOK_WIDTH), F32)],
        compiler_params=_cparams("parallel", "arbitrary"),
        name="rwkv_mix",
    )(p, pshift, s0, mu, vecs, w2, a2, g2, e, _block_ltri(nb, c))


def _hgrn_kernel(p_ref, s0_ref, lbp_ref, gn_ref, lt_ref, tok_ref, sout_ref, st_ref, *, c, nb, layer, col0,
                 single_chunk):
    ci = pl.program_id(1)
    tw = TOK_WIDTH
    hd = B_HEAD_DIM
    ch = [(b, h) for b in range(nb) for h in range(B_HEADS)]
    n = range(len(ch))

    if not single_chunk:
        @pl.when(ci == 0)
        def _():
            for i in n:
                st_ref[i] = s0_ref[ch[i][0], ch[i][1]].T

    lbp = lbp_ref[...]
    mx = jnp.max(lbp, axis=0, keepdims=True)
    ex = jnp.exp(lbp - mx)
    den = jnp.sum(ex, axis=0, keepdims=True)
    lb = jnp.zeros((1, tw), F32)
    for i in range(1, layer + 1):
        lb = lb + ex[i:i + 1, :] / den

    q = p_ref[:, col0:col0 + tw].astype(F32)
    f = p_ref[:, col0 + tw:col0 + 2 * tw].astype(F32)
    iv = p_ref[:, col0 + 2 * tw:col0 + 3 * tw].astype(F32)
    og = p_ref[:, col0 + 3 * tw:col0 + 4 * tw].astype(F32)
    fg = lb + (1.0 - lb) * _sigmoid(f)
    lf = jnp.log(fg)
    kk = 1.0 - fg
    qq = q * _sigmoid(q)
    lf_hi, lf_lo = _split(lf)
    bc = _dot(lt_ref[...], lf_hi) + _dot(lt_ref[...], lf_lo)
    blast = _bcast_rows(bc, c - 1, nb, c)
    mid = _bcast_rows(bc, (c - 1) // 2, nb, c)
    qe = qq * jnp.exp(bc - mid)
    ke = kk * jnp.exp(mid - bc)
    qs = qq * jnp.exp(bc)
    kl = kk * jnp.exp(blast - bc)
    gam = jnp.exp(blast)
    gate = og * _sigmoid(og)
    ti = lax.broadcasted_iota(jnp.int32, (c, c), 0)
    si = lax.broadcasted_iota(jnp.int32, (c, c), 1)
    incl = si <= ti
    gn = gn_ref[...]
    rs = [slice(b * c, (b + 1) * c) for b, _ in ch]
    ls = [slice(h * hd, (h + 1) * hd) for _, h in ch]
    v_h = [_bf(iv[rs[i], ls[i]]) for i in n]
    att = [jnp.where(incl, _dot_nt(_bf(qe[rs[i], ls[i]]), _bf(ke[rs[i], ls[i]])), 0.0) for i in n]
    if single_chunk:
        st = [s0_ref[ch[i][0], ch[i][1]] for i in n]
        o = [_dot(_bf(att[i]), v_h[i]) + _dot(_bf(qs[rs[i], ls[i]]), _bf(st[i])) for i in n]
        ones = jnp.ones((c, hd), BF16)
        lf_parts = [_split(lf[rs[i], ls[i]]) for i in n]
        gam_col = [jnp.exp(_dot_tn(hi, ones) + _dot_tn(lo, ones)) for hi, lo in lf_parts]
        st_new = [gam_col[i] * st[i] + _dot_tn(_bf(kl[rs[i], ls[i]]), v_h[i]) for i in n]
    else:
        st = [st_ref[i] for i in n]
        o = [_dot(_bf(att[i]), v_h[i]) + _dot_nt(_bf(qs[rs[i], ls[i]]), _bf(st[i])) for i in n]
        st_new = [st[i] * gam[ch[i][0] * c:ch[i][0] * c + 1, ls[i]] + _dot_tn(v_h[i], _bf(kl[rs[i], ls[i]]))
                  for i in n]
    on = [x * lax.rsqrt(jnp.mean(x * x, axis=-1, keepdims=True) + RMS_EPS) * gn for x in o]
    for i in n:
        tok_ref[rs[i], ls[i]] = (on[i] * gate[rs[i], ls[i]]).astype(tok_ref.dtype)
        if single_chunk:
            sout_ref[ch[i][0], ch[i][1]] = st_new[i]
        else:
            st_ref[i] = st_new[i]

    if not single_chunk:
        @pl.when(ci == pl.num_programs(1) - 1)
        def _():
            for i in n:
                sout_ref[ch[i][0], ch[i][1]] = st_ref[i].T


def hgrn_mix(p, row0, col0, s0, lbp, gn, n_batch, seq, c, nb, layer):
    assert nb == 1 or (seq == c and c == 8)
    cols = p.shape[1]
    n_chunks = seq // c
    rows = nb * c
    t0 = row0 // rows
    const = lambda shape: pl.BlockSpec(shape, lambda b, ci: (0,) * len(shape))
    st_spec = pl.BlockSpec((nb, B_HEADS, B_HEAD_DIM, B_HEAD_DIM), lambda b, ci: (b, 0, 0, 0))
    return pl.pallas_call(
        functools.partial(_hgrn_kernel, c=c, nb=nb, layer=layer, col0=col0, single_chunk=n_chunks == 1),
        grid=(n_batch // nb, n_chunks),
        in_specs=[pl.BlockSpec((rows, cols), lambda b, ci: (t0 + b * n_chunks + ci, 0)),
                  st_spec, const((DEPTH, TOK_WIDTH)), const((1, B_HEAD_DIM)), const((rows, rows))],
        out_specs=[pl.BlockSpec((rows, TOK_WIDTH), lambda b, ci: (b * n_chunks + ci, 0)), st_spec],
        out_shape=[jax.ShapeDtypeStruct((n_batch * seq, TOK_WIDTH), BF16),
                   jax.ShapeDtypeStruct((n_batch, B_HEADS, B_HEAD_DIM, B_HEAD_DIM), F32)],
        scratch_shapes=[pltpu.VMEM((nb * B_HEADS, B_HEAD_DIM, B_HEAD_DIM), F32)],
        compiler_params=_cparams("parallel", "arbitrary"),
        name="hgrn_mix",
    )(p, s0, lbp, gn.reshape(1, B_HEAD_DIM), _block_ltri(nb, c))


PROMPT_CHUNK = 64
A_CHAINS_PER_GROUP = 96
SAMPLE_SEQS_PER_STEP = 4
TM_DENSE = 1024


def kernel(x_prompt, x_sample, mem_prompt, cache_mem_k, cache_mem_v, state_rwkv, state_shift, state_hgrn, state_conv, norm_mix, norm_ffn, norm_final, mem_norm, w_mem_kv, a_w_in, a_mu, a_w0, a_w2, a_a0, a_a2, a_g2, a_k_k, a_k_a, a_r_k, a_ln_w, a_ln_b, a_w_out, b_w_in, b_lower_bounds, b_g_norm, b_w_out, ffn_w_up, ffn_conv_w, ffn_conv_b, ffn_w_down):
    bp, sp, d = x_prompt.shape
    bs, ss, _ = x_sample.shape
    mp, ms = bp * sp, bs * ss
    nbs = SAMPLE_SEQS_PER_STEP

    pad_cols = A_PROJ_COLS - a_w_in.shape[2]
    wa = _bf(jnp.pad(a_w_in[0], ((0, 0), (0, pad_cols))))
    mu = jnp.pad(a_mu[0], (0, pad_cols)).reshape(1, A_TOK_COLS)
    vecs = jnp.stack([a_w0[0], a_a0[0], a_k_k[0], a_k_a[0], a_r_k[0], a_ln_w[0], a_ln_b[0],
                      jnp.zeros((TOK_WIDTH,), F32)])
    w2 = _bf(jnp.pad(a_w2[0], ((0, 128 - A_LORA), (0, 0))))
    a2 = _bf(jnp.pad(a_a2[0], ((A_LORA, 256 - 2 * A_LORA), (0, 0))))
    g2 = _bf(jnp.pad(a_g2[0], ((2 * A_LORA - 128, 384 - (2 * A_LORA - 128) - A_GATE_RANK), (0, 0))))
    wb = _bf(b_w_in[0])
    w_out = [_bf(a_w_out[0]), _bf(b_w_out[0])]
    w_up = _bf(ffn_w_up)
    w_down = _bf(ffn_w_down)

    mem_rows = mem_prompt.reshape(bp * N_MEM, d)
    mem_kv = [norm_matmul([mem_rows], mem_norm[l], _bf(w_mem_kv[l]), bp * N_MEM, 512) for l in range(DEPTH)]
    mem_k_prompt = jnp.stack([kv[:, :MEM_WIDTH] for kv in mem_kv]).reshape(DEPTH, bp, N_MEM, MEM_HEADS, MEM_HEAD_DIM)
    mem_v_prompt = jnp.stack([kv[:, MEM_WIDTH:] for kv in mem_kv]).reshape(DEPTH, bp, N_MEM, MEM_HEADS, MEM_HEAD_DIM)

    hist = []
    for l in range(DEPTH):
        h = state_conv[l]
        p1 = jnp.pad(h[:, 1:2], ((0, 0), (0, ss - 1), (0, 0))).reshape(ms, D_FF)
        p2 = jnp.pad(h, ((0, 0), (0, ss - 2), (0, 0))).reshape(ms, D_FF)
        hist.append((p1, p2))

    def ffn(x, layer):
        u_p, cs_p = ffn_up_prompt(x, norm_ffn[layer], w_up, layer, ffn_conv_w[layer], ffn_conv_b[layer],
                                  bp, sp, 1024, 512)
        u_s, c6, c7 = ffn_up_sample(x, mp, norm_ffn[layer], w_up, layer, ffn_conv_w[layer], ffn_conv_b[layer],
                                    hist[layer][0], hist[layer][1], ss, 1024, 512)
        x_p = matmul([[u_p]], [(w_down, layer)], [x], D_MODEL, TM_DENSE, 512)
        x_s = matmul([[u_s]], [(w_down, layer)], [x], D_MODEL, TM_DENSE, 512, res_row0=mp)
        return [x_p, x_s], cs_p, jnp.stack([c6, c7], axis=1)

    def out_proj(tok, mem_o, w, res_segs):
        return matmul([tok, mem_o], [w[:TOK_WIDTH], w[TOK_WIDTH:]], res_segs, D_MODEL, TM_DENSE, 1024)

    x_segs = [x_prompt.reshape(mp, d), x_sample.reshape(ms, d)]
    p = norm_matmul(x_segs, norm_mix[0], wa, TM_DENSE, 512, BF16)
    pshift = matmul([[state_shift[0]]], [wa], None, A_PROJ_COLS, bs, 512).reshape(bs, 1, A_PROJ_COLS)
    tok_p, rwkv_p = rwkv_mix(p, 0, jnp.zeros((bp, 1, A_PROJ_COLS), F32),
                             jnp.zeros((bp, A_HEADS, A_HEAD_DIM, A_HEAD_DIM), F32),
                             mu, vecs, w2, a2, g2, bp, sp, PROMPT_CHUNK, 1)
    tok_s, rwkv_s = rwkv_mix(p, mp, pshift, state_rwkv[0], mu, vecs, w2, a2, g2, bs, ss, ss, nbs)
    mo_p = mem_attention(p, 0, 0, mem_kv[0], bp, sp, 512)
    mo_s = mem_attention_cache(p, mp, 0, cache_mem_k, cache_mem_v, 0, ss, 8)
    x = out_proj([tok_p, tok_s], [mo_p, mo_s], w_out[0], x_segs)
    x_segs, conv_p0, conv_s0 = ffn(x, 0)
    p = norm_matmul(x_segs, norm_mix[1], wb, TM_DENSE, 512, BF16)
    tok_p, hgrn_p = hgrn_mix(p, 0, MEM_WIDTH, jnp.zeros((bp, B_HEADS, B_HEAD_DIM, B_HEAD_DIM), F32),
                             b_lower_bounds, b_g_norm[0], bp, sp, PROMPT_CHUNK, 1, 1)
    tok_s, hgrn_s = hgrn_mix(p, mp, MEM_WIDTH, state_hgrn[0], b_lower_bounds, b_g_norm[0], bs, ss, ss, nbs, 1)
    mo_p = mem_attention(p, 0, 0, mem_kv[1], bp, sp, 512)
    mo_s = mem_attention_cache(p, mp, 0, cache_mem_k, cache_mem_v, 1, ss, 8)
    x = out_proj([tok_p, tok_s], [mo_p, mo_s], w_out[1], x_segs)
    x_segs, conv_p1, conv_s1 = ffn(x, 1)
    y_p = rmsnorm(x_segs[0], norm_final, 256)
    y_s = rmsnorm(x_segs[1], norm_final, 256)
    shift_p = rmsnorm(x_prompt[:, -1], norm_mix[0], bp)
    shift_s = rmsnorm(x_sample[:, -1], norm_mix[0], bs)

    return (y_p.reshape(bp, sp, d), y_s.reshape(bs, ss, d), mem_k_prompt, mem_v_prompt,
            rwkv_p[None], rwkv_s[None], shift_p[None], shift_s[None], hgrn_p[None], hgrn_s[None],
            jnp.stack([conv_p0, conv_p1]), jnp.stack([conv_s0, conv_s1]))
```

```python
import functools

import jax
import jax.numpy as jnp
from jax import lax
from jax.experimental import pallas as pl
from jax.experimental.pallas import tpu as pltpu

F32 = jnp.float32
BF16 = jnp.bfloat16

D_MODEL = 2048
DEPTH = 2
TOK_WIDTH = 1536
MEM_WIDTH = 512
MEM_HEADS = 4
MEM_HEAD_DIM = 128
N_MEM = 256
A_HEAD_DIM = 64
A_HEADS = 24
A_LORA = 96
A_GATE_RANK = 256
A_TOK_COLS = 5120
A_PROJ_COLS = MEM_WIDTH + A_TOK_COLS
B_HEADS = 12
B_HEAD_DIM = 128
D_FF = 5632
RMS_EPS = 1e-6
GN_EPS = 64e-5

LANES = 128
VMEM_LIMIT = 56 * 1024 * 1024


def _cparams(*sem):
    return pltpu.CompilerParams(dimension_semantics=sem, vmem_limit_bytes=VMEM_LIMIT)


def _dot(a, b):
    return jnp.dot(a, b, preferred_element_type=F32)


def _dot_nt(a, b):
    return lax.dot_general(a, b, (((1,), (1,)), ((), ())), preferred_element_type=F32)


def _dot_tn(a, b):
    return lax.dot_general(a, b, (((0,), (0,)), ((), ())), preferred_element_type=F32)


def _bf(x):
    return x.astype(BF16)


def _split(x):
    hi = x.astype(BF16)
    lo = (x - hi.astype(F32)).astype(BF16)
    return hi, lo


def _sigmoid(x):
    return 1.0 / (1.0 + jnp.exp(-x))


def _rms(x, g):
    ms = jnp.mean(x * x, axis=-1, keepdims=True)
    return x * lax.rsqrt(ms + RMS_EPS) * g


def _seg_bounds(segs, tm):
    bounds, off = [], 0
    for a in segs:
        bounds.append(off)
        off += a.shape[0] // tm
    return tuple(bounds), off


def _seg_row_specs(segs, tm, width, col_of_j, tile0=0):
    assert tile0 == 0 or len(segs) == 1
    specs, off = [], 0
    for a in segs:
        n = a.shape[0] // tm
        if col_of_j:
            imap = lambda i, j, off=off, n=n: (jnp.clip(i - off + tile0, 0, n - 1), j)
        else:
            imap = lambda i, j, off=off, n=n: (jnp.clip(i - off + tile0, 0, n - 1), 0)
        specs.append(pl.BlockSpec((tm, width), imap))
        off += n
    return specs


def _for_segment(i, bounds, fn):
    if len(bounds) == 1:
        fn(0)
        return
    for s in range(len(bounds)):
        cond = i >= bounds[s]
        if s + 1 < len(bounds):
            cond = jnp.logical_and(cond, i < bounds[s + 1])
        pl.when(cond)(functools.partial(fn, s))


def _w_spec(w, tn, col_block0=0):
    if isinstance(w, tuple):
        arr, layer = w
        return arr, pl.BlockSpec((None, arr.shape[1], tn), lambda i, j: (layer, 0, col_block0 + j))
    return w, pl.BlockSpec((w.shape[0], tn), lambda i, j: (0, col_block0 + j))


def _rmsnorm_kernel(x_ref, g_ref, o_ref):
    o_ref[...] = _rms(x_ref[...], g_ref[...])


def rmsnorm(x, g, tm, row0=0, rows=None):
    d = x.shape[1]
    rows = x.shape[0] if rows is None else rows
    t0 = row0 // tm
    return pl.pallas_call(
        _rmsnorm_kernel,
        grid=(rows // tm,),
        in_specs=[pl.BlockSpec((tm, d), lambda i: (t0 + i, 0)), pl.BlockSpec((1, d), lambda i: (0, 0))],
        out_specs=pl.BlockSpec((tm, d), lambda i: (i, 0)),
        out_shape=jax.ShapeDtypeStruct((rows, d), F32),
        compiler_params=_cparams("parallel"),
        name="rmsnorm",
    )(x, g.reshape(1, d))


NORM_ROWS = 256


def _norm_to_scratch(x_ref, g, hb_ref, tm, dst_off=0):
    for r in range(0, tm, NORM_ROWS):
        n = min(NORM_ROWS, tm - r)
        hb_ref[dst_off + r:dst_off + r + n, :] = _bf(_rms(x_ref[r:r + n, :], g))


def _norm_matmul_kernel(*refs, n_seg, bounds, tm):
    x_refs = refs[:n_seg]
    g_ref, w_ref, o_ref, hb_ref = refs[n_seg:]

    @pl.when(pl.program_id(1) == 0)
    def _():
        g = g_ref[...]
        _for_segment(pl.program_id(0), bounds, lambda s: _norm_to_scratch(x_refs[s], g, hb_ref, tm))

    o_ref[...] = _dot(hb_ref[...], w_ref[...]).astype(o_ref.dtype)


def norm_matmul(x_segs, g, w, tm, tn, out_dtype=F32):
    d = x_segs[0].shape[1]
    bounds, n_tiles = _seg_bounds(x_segs, tm)
    w_arr, w_spec = _w_spec(w, tn)
    n = w_arr.shape[-1]
    return pl.pallas_call(
        functools.partial(_norm_matmul_kernel, n_seg=len(x_segs), bounds=bounds, tm=tm),
        grid=(n_tiles, n // tn),
        in_specs=_seg_row_specs(x_segs, tm, d, False) + [pl.BlockSpec((1, d), lambda i, j: (0, 0)), w_spec],
        out_specs=pl.BlockSpec((tm, tn), lambda i, j: (i, j)),
        out_shape=jax.ShapeDtypeStruct((n_tiles * tm, n), out_dtype),
        scratch_shapes=[pltpu.VMEM((tm, d), BF16)],
        compiler_params=_cparams("parallel", "arbitrary"),
        name="norm_matmul",
    )(*x_segs, g.reshape(1, d), w_arr)


def _mm_kernel(*refs, a_counts, res_count, bounds):
    pos = 0
    a_refs = []
    for cnt in a_counts:
        a_refs.append(refs[pos:pos + cnt])
        pos += cnt
    w_refs = refs[pos:pos + len(a_counts)]
    pos += len(a_counts)
    res_refs = refs[pos:pos + res_count]
    o_ref = refs[-1]

    def compute(s):
        acc = None
        for k, segs in enumerate(a_refs):
            term = _dot(_bf(segs[min(s, len(segs) - 1)][...]), w_refs[k][...])
            acc = term if acc is None else acc + term
        if res_count:
            acc = res_refs[min(s, res_count - 1)][...] + acc
        o_ref[...] = acc

    _for_segment(pl.program_id(0), bounds, compute)


def matmul(a_list, w_list, res_segs, n_out, tm, tn, res_row0=0):
    longest = max(a_list + ([res_segs] if res_segs else []), key=len)
    bounds, n_tiles = _seg_bounds(longest, tm)
    in_specs, args = [], []
    for segs in a_list:
        assert len(segs) in (1, len(bounds))
        in_specs += _seg_row_specs(segs, tm, segs[0].shape[1], False)
        args += list(segs)
    for w in w_list:
        w_arr, w_spec = _w_spec(w, tn)
        in_specs.append(w_spec)
        args.append(w_arr)
    if res_segs:
        assert len(res_segs) in (1, len(bounds))
        in_specs += _seg_row_specs(res_segs, tm, tn, True, res_row0 // tm)
        args += list(res_segs)
    return pl.pallas_call(
        functools.partial(_mm_kernel, a_counts=tuple(len(s) for s in a_list),
                          res_count=len(res_segs) if res_segs else 0, bounds=bounds),
        grid=(n_tiles, n_out // tn),
        in_specs=in_specs,
        out_specs=pl.BlockSpec((tm, tn), lambda i, j: (i, j)),
        out_shape=jax.ShapeDtypeStruct((n_tiles * tm, n_out), F32),
        compiler_params=_cparams("parallel", "parallel"),
        name="matmul",
    )(*args)


def _attn_kernel(q_ref, k_ref, v_ref, o_ref, *, n_seq, tq, cache_layout):
    scale = MEM_HEAD_DIM ** -0.5
    pairs = [(s, h) for s in range(n_seq) for h in range(MEM_HEADS)]
    hsl = [slice(h * MEM_HEAD_DIM, (h + 1) * MEM_HEAD_DIM) for h in range(MEM_HEADS)]
    q = [_bf(q_ref[s * tq:(s + 1) * tq, hsl[h]]) for s, h in pairs]
    if cache_layout:
        k = [_bf(k_ref[0, s, pl.ds(h, N_MEM, stride=MEM_HEADS), :]) for s, h in pairs]
        v = [_bf(v_ref[0, s, pl.ds(h, N_MEM, stride=MEM_HEADS), :]) for s, h in pairs]
    else:
        k = [_bf(k_ref[s * N_MEM:(s + 1) * N_MEM, hsl[h]]) for s, h in pairs]
        v = [_bf(v_ref[s * N_MEM:(s + 1) * N_MEM, hsl[h]]) for s, h in pairs]
    sc = [_dot_nt(q[i], k[i]) * scale for i in range(len(pairs))]
    e = [jnp.exp(x - jnp.max(x, axis=-1, keepdims=True)) for x in sc]
    p = [_bf(x / jnp.sum(x, axis=-1, keepdims=True)) for x in e]
    o = [_dot(p[i], v[i]) for i in range(len(pairs))]
    rows = [jnp.concatenate(o[s * MEM_HEADS:(s + 1) * MEM_HEADS], axis=1) for s in range(n_seq)]
    o_ref[...] = _bf(jnp.concatenate(rows, axis=0) if n_seq > 1 else rows[0])


def mem_attention(p, row0, q_colblock, kv, n_batch, seq, tq):
    q_tiles = seq // tq
    t0 = row0 // tq
    return pl.pallas_call(
        functools.partial(_attn_kernel, n_seq=1, tq=tq, cache_layout=False),
        grid=(n_batch, q_tiles),
        in_specs=[pl.BlockSpec((tq, MEM_WIDTH), lambda b, t: (t0 + b * q_tiles + t, q_colblock)),
                  pl.BlockSpec((N_MEM, MEM_WIDTH), lambda b, t: (b, 0)),
                  pl.BlockSpec((N_MEM, MEM_WIDTH), lambda b, t: (b, 1))],
        out_specs=pl.BlockSpec((tq, MEM_WIDTH), lambda b, t: (b * q_tiles + t, 0)),
        out_shape=jax.ShapeDtypeStruct((n_batch * seq, MEM_WIDTH), BF16),
        compiler_params=_cparams("parallel", "parallel"),
        name="mem_attention",
    )(p, kv, kv)


def mem_attention_cache(p, row0, q_colblock, cache_k, cache_v, layer, seq, n_seq):
    depth, n_batch = cache_k.shape[:2]
    cache_k = cache_k.reshape(depth, n_batch, N_MEM * MEM_HEADS, MEM_HEAD_DIM)
    cache_v = cache_v.reshape(depth, n_batch, N_MEM * MEM_HEADS, MEM_HEAD_DIM)
    rows = n_seq * seq
    t0 = row0 // rows
    kv_spec = pl.BlockSpec((1, n_seq, N_MEM * MEM_HEADS, MEM_HEAD_DIM), lambda b: (layer, b, 0, 0))
    return pl.pallas_call(
        functools.partial(_attn_kernel, n_seq=n_seq, tq=seq, cache_layout=True),
        grid=(n_batch // n_seq,),
        in_specs=[pl.BlockSpec((rows, MEM_WIDTH), lambda b: (t0 + b, q_colblock)), kv_spec, kv_spec],
        out_specs=pl.BlockSpec((rows, MEM_WIDTH), lambda b: (b, 0)),
        out_shape=jax.ShapeDtypeStruct((n_batch * seq, MEM_WIDTH), BF16),
        compiler_params=_cparams("parallel"),
        name="mem_attention_cache",
    )(p, cache_k, cache_v)


FFN_HALO = 16


def _gelu_gate(c, v):
    return _bf(jax.nn.gelu(c) * v)


def _ffn_up_prompt_kernel(x_ref, xh_ref, g_ref, wa_ref, wv_ref, cw_ref, cb_ref, u_ref, cs_ref, hb_ref,
                          *, tm, tiles_per_seq):
    i = pl.program_id(0)

    @pl.when(pl.program_id(1) == 0)
    def _():
        g = g_ref[...]
        hb_ref[0:FFN_HALO, :] = _bf(_rms(xh_ref[...], g))
        _norm_to_scratch(x_ref, g, hb_ref, tm, dst_off=FFN_HALO)

    a_ext = _dot(hb_ref[...], wa_ref[...])
    v = _dot(hb_ref[FFN_HALO:, :], wv_ref[...])
    rows = lax.broadcasted_iota(jnp.int32, (FFN_HALO + tm, 1), 0)
    n_zero = jnp.where((i % tiles_per_seq) == 0, FFN_HALO, 0)
    a_ext = jnp.where(rows < n_zero, 0.0, a_ext)
    a0 = a_ext[FFN_HALO:]
    a1 = pltpu.roll(a_ext, 1, 0)[FFN_HALO:]
    a2 = pltpu.roll(a_ext, 2, 0)[FFN_HALO:]
    c = cb_ref[...] + a2 * cw_ref[0:1, :] + a1 * cw_ref[1:2, :] + a0 * cw_ref[2:3, :]
    u_ref[...] = _gelu_gate(c, v)
    cs_ref[0] = a0[tm - 8:tm][6:8]


def ffn_up_prompt(x, g, w_up, layer, cw, cb, n_batch, seq, tm, tn):
    d = x.shape[1]
    m = n_batch * seq
    nf = D_FF // tn
    tiles_per_seq = seq // tm
    halo_blocks = tm // FFN_HALO
    w_arr, wa_spec = _w_spec((w_up, layer), tn)
    _, wv_spec = _w_spec((w_up, layer), tn, nf)
    u, cs = pl.pallas_call(
        functools.partial(_ffn_up_prompt_kernel, tm=tm, tiles_per_seq=tiles_per_seq),
        grid=(m // tm, nf),
        in_specs=[pl.BlockSpec((tm, d), lambda i, j: (i, 0)),
                  pl.BlockSpec((FFN_HALO, d), lambda i, j: (jnp.maximum(i * halo_blocks - 1, 0), 0)),
                  pl.BlockSpec((1, d), lambda i, j: (0, 0)),
                  wa_spec, wv_spec,
                  pl.BlockSpec((3, tn), lambda i, j: (0, j)),
                  pl.BlockSpec((1, tn), lambda i, j: (0, j))],
        out_specs=[pl.BlockSpec((tm, tn), lambda i, j: (i, j)),
                   pl.BlockSpec((1, 2, tn), lambda i, j: (i, 0, j))],
        out_shape=[jax.ShapeDtypeStruct((m, D_FF), BF16),
                   jax.ShapeDtypeStruct((m // tm, 2, D_FF), F32)],
        scratch_shapes=[pltpu.VMEM((FFN_HALO + tm, d), BF16)],
        compiler_params=_cparams("parallel", "arbitrary"),
        name="ffn_up_prompt",
    )(x, x, g.reshape(1, d), w_arr, w_arr, cw, cb.reshape(1, D_FF))
    return u, cs.reshape(n_batch, tiles_per_seq, 2, D_FF)[:, -1]


def _ffn_up_sample_kernel(x_ref, g_ref, wa_ref, wv_ref, cw_ref, cb_ref, h0_ref, h1_ref,
                          u_ref, c6_ref, c7_ref, hb_ref, a_ref, a1_ref, a2_ref, *, tm, seq):
    @pl.when(pl.program_id(1) == 0)
    def _():
        _norm_to_scratch(x_ref, g_ref[...], hb_ref, tm)

    a = _dot(hb_ref[...], wa_ref[...])
    v = _dot(hb_ref[...], wv_ref[...])
    r1 = pltpu.roll(a, 1, 0)
    r2 = pltpu.roll(a, 2, 0)
    nseq = tm // seq
    firsts = pl.ds(0, nseq, stride=seq)
    seconds = pl.ds(1, nseq, stride=seq)
    for q in range(a.shape[1] // LANES):
        sl = slice(q * LANES, (q + 1) * LANES)
        a1_ref[q] = r1[:, sl]
        a1_ref[q, firsts, :] = h1_ref[:, sl]
        a2_ref[q] = r2[:, sl]
        a2_ref[q, firsts, :] = h0_ref[:, sl]
        a2_ref[q, seconds, :] = h1_ref[:, sl]
        a_ref[q] = a[:, sl]
        c6_ref[:, sl] = a_ref[q, pl.ds(seq - 2, nseq, stride=seq), :]
        c7_ref[:, sl] = a_ref[q, pl.ds(seq - 1, nseq, stride=seq), :]
    n_q = a.shape[1] // LANES
    a1 = jnp.concatenate([a1_ref[q] for q in range(n_q)], axis=1)
    a2 = jnp.concatenate([a2_ref[q] for q in range(n_q)], axis=1)
    c = cb_ref[...] + a2 * cw_ref[0:1, :] + a1 * cw_ref[1:2, :] + a * cw_ref[2:3, :]
    u_ref[...] = _gelu_gate(c, v)


def ffn_up_sample(x, row0, g, w_up, layer, cw, cb, hist, seq, tm, tn):
    d = x.shape[1]
    m = hist.shape[0] * seq
    nf = D_FF // tn
    nb = tm // seq
    t0 = row0 // tm
    w_arr, wa_spec = _w_spec((w_up, layer), tn)
    _, wv_spec = _w_spec((w_up, layer), tn, nf)
    rows_scratch = pltpu.VMEM((tn // LANES, tm, LANES), F32)
    return pl.pallas_call(
        functools.partial(_ffn_up_sample_kernel, tm=tm, seq=seq),
        grid=(m // tm, nf),
        in_specs=[pl.BlockSpec((tm, d), lambda i, j: (t0 + i, 0)),
                  pl.BlockSpec((1, d), lambda i, j: (0, 0)),
                  wa_spec, wv_spec,
                  pl.BlockSpec((3, tn), lambda i, j: (0, j)),
                  pl.BlockSpec((1, tn), lambda i, j: (0, j)),
                  pl.BlockSpec((nb, tn), lambda i, j: (i, j)),
                  pl.BlockSpec((nb, tn), lambda i, j: (i, nf + j))],
        out_specs=[pl.BlockSpec((tm, tn), lambda i, j: (i, j)),
                   pl.BlockSpec((nb, tn), lambda i, j: (i, j)),
                   pl.BlockSpec((nb, tn), lambda i, j: (i, j))],
        out_shape=[jax.ShapeDtypeStruct((m, D_FF), BF16),
                   jax.ShapeDtypeStruct((m // seq, D_FF), F32),
                   jax.ShapeDtypeStruct((m // seq, D_FF), F32)],
        scratch_shapes=[pltpu.VMEM((tm, d), BF16), rows_scratch, rows_scratch, rows_scratch],
        compiler_params=_cparams("parallel", "arbitrary"),
        name="ffn_up_sample",
    )(x, g.reshape(1, d), w_arr, w_arr, cw, cb.reshape(1, D_FF), hist, hist)


def _seg_sum(x, e, two_pass):
    parts = []
    for j in range(x.shape[1] // LANES):
        xj = x[:, j * LANES:(j + 1) * LANES]
        if two_pass:
            hi, lo = _split(xj)
            parts.append(_dot(hi, e) + _dot(lo, e))
        else:
            parts.append(_dot(_bf(xj), e))
    return jnp.concatenate(parts, axis=1)


def _bcast_rows(x, idx, nb, c):
    parts = [jnp.broadcast_to(x[b * c + idx:b * c + idx + 1, :], (c, x.shape[1])) for b in range(nb)]
    return parts[0] if nb == 1 else jnp.concatenate(parts, axis=0)


def _unit_lower_solve(a_list, rhs_list, c):
    mm = lambda x, y: _dot(_bf(x), _bf(y))
    n = range(len(a_list))
    rows = lax.broadcasted_iota(jnp.int32, (c, c), 0)
    cols = lax.broadcasted_iota(jnp.int32, (c, c), 1)
    eye = (rows == cols).astype(F32)
    blk = min(c, 16)
    if c > blk:
        assert c // blk <= 4
        same = (rows // blk) == (cols // blk)
        ad = [jnp.where(same, a, 0.0) for a in a_list]
        ao = [a_list[i] - ad[i] for i in n]
    else:
        ad = a_list
    t = [eye - ad[i] for i in n]
    pw = ad
    span = 2
    while span < blk:
        pw = [mm(pw[i], pw[i]) for i in n]
        t = [t[i] + mm(t[i], pw[i]) for i in n]
        span *= 2
    x = [mm(t[i], rhs_list[i]) for i in n]
    if c > blk:
        nn = [mm(t[i], ao[i]) for i in n]
        n2 = [mm(nn[i], nn[i]) for i in n]
        x = [x[i] + mm(n2[i], x[i]) for i in n]
        x = [x[i] - mm(nn[i], x[i]) for i in n]
    return x


def _rwkv_chains(n, ch, c, s_ref, y_ref, xk, xr, kb, bb, v, kh, bh, gam, strict, incl):
    rs = {i: slice(ch[i][0] * c, (ch[i][0] + 1) * c) for i in n}
    ls = {i: slice(ch[i][1] * A_HEAD_DIM, (ch[i][1] + 1) * A_HEAD_DIM) for i in n}
    s_old = {i: s_ref[i] for i in n}
    sb = {i: _bf(s_old[i]) for i in n}
    xk_h = {i: _bf(xk[rs[i], ls[i]]) for i in n}
    xr_h = {i: _bf(xr[rs[i], ls[i]]) for i in n}
    kb_h = {i: _bf(kb[rs[i], ls[i]]) for i in n}
    bb_h = {i: _bf(bb[rs[i], ls[i]]) for i in n}
    v_f = {i: v[rs[i], ls[i]] for i in n}
    v_h = {i: _bf(v_f[i]) for i in n}
    a_kk = {i: jnp.where(strict, _dot_nt(xk_h[i], kb_h[i]), 0.0) for i in n}
    a_kb = {i: jnp.where(strict, _dot_nt(xk_h[i], bb_h[i]), 0.0) for i in n}
    a_rk = {i: jnp.where(incl, _dot_nt(xr_h[i], kb_h[i]), 0.0) for i in n}
    a_rb = {i: jnp.where(incl, _dot_nt(xr_h[i], bb_h[i]), 0.0) for i in n}
    rhs = {i: -(_dot_nt(xk_h[i], sb[i]) + _dot(_bf(a_kk[i]), v_h[i])) for i in n}
    u = dict(zip(n, _unit_lower_solve([a_kb[i] for i in n], [rhs[i] for i in n], c)))
    u_h = {i: _bf(u[i]) for i in n}
    y_h = {i: _dot_nt(xr_h[i], sb[i]) + _dot(_bf(a_rk[i]), v_h[i]) + _dot(_bf(a_rb[i]), u_h[i]) for i in n}
    vu = {i: _bf(jnp.concatenate([v_f[i], u[i]], axis=0)) for i in n}
    kbh = {i: _bf(jnp.concatenate([kh[rs[i], ls[i]], bh[rs[i], ls[i]]], axis=0)) for i in n}
    s_new = {i: s_old[i] * gam[ch[i][0] * c:ch[i][0] * c + 1, ls[i]] + _dot_tn(vu[i], kbh[i]) for i in n}
    for i in n:
        y_ref[rs[i], ls[i]] = y_h[i]
        s_ref[i] = s_new[i]


def _rwkv_kernel(p_ref, ps_ref, s0_ref, mu_ref, vec_ref, w2_ref, a2_ref, g2_ref, e_ref, lt_ref,
                 tok_ref, sout_ref, prev_ref, s_ref, y_ref, *, c, nb):
    ci = pl.program_id(1)
    tw = TOK_WIDTH
    lora0 = 3 * tw

    @pl.when(ci == 0)
    def _():
        for b in range(nb):
            s_ref[b * A_HEADS:(b + 1) * A_HEADS] = s0_ref[b]
        prev_ref[...] = ps_ref[:, :, MEM_WIDTH:]

    p = p_ref[:, MEM_WIDTH:].astype(F32)
    rows = lax.broadcasted_iota(jnp.int32, (nb * c, 1), 0)
    p_prev = pltpu.roll(p, 1, 0)
    for b in range(nb):
        p_prev = jnp.where(rows == b * c, prev_ref[b], p_prev)
        prev_ref[b] = p[b * c + c - 1:b * c + c, :]
    xm = p + mu_ref[...] * (p_prev - p)

    r = xm[:, 0:tw]
    k = xm[:, tw:2 * tw]
    v = xm[:, 2 * tw:3 * tw]
    xw = xm[:, lora0:lora0 + 128]
    xa = xm[:, lora0:lora0 + 256]
    xg = xm[:, lora0 + 128:lora0 + 512]
    w0, a0, k_k, k_a, r_k, ln_w, ln_b = (vec_ref[i:i + 1, :] for i in range(7))

    z = -(w0 + _dot(_bf(jnp.tanh(xw)), w2_ref[...]))
    softplus = jnp.maximum(z, 0.0) + jnp.log(1.0 + jnp.exp(-jnp.abs(z)))
    ell = -jnp.exp(-softplus - 0.5)
    a = _sigmoid(a0 + _dot(_bf(xa), a2_ref[...]))
    gate = _dot(_bf(_sigmoid(xg)), g2_ref[...])
    kkraw = k * k_k
    k2 = k * (1.0 + (a - 1.0) * k_a)
    e = e_ref[...]
    kap = kkraw * lax.rsqrt(jnp.maximum(_seg_sum(kkraw * kkraw, e, True), 1e-24))
    bet = kap * a

    ell_hi, ell_lo = _split(ell)
    gc = _dot(lt_ref[...], ell_hi) + _dot(lt_ref[...], ell_lo)
    glast = _bcast_rows(gc, c - 1, nb, c)
    egi = jnp.exp(-gc)
    el = jnp.exp(glast - gc)
    xk = kap * jnp.exp(gc - ell)
    xr = r * jnp.exp(gc)
    kb = k2 * egi
    bb = bet * egi
    kh = k2 * el
    bh = bet * el
    gam = jnp.exp(glast)
    ti = lax.broadcasted_iota(jnp.int32, (c, c), 0)
    si = lax.broadcasted_iota(jnp.int32, (c, c), 1)
    strict = si < ti
    incl = si <= ti

    ch = [(b, h) for b in range(nb) for h in range(A_HEADS)]
    for g0 in range(0, len(ch), A_CHAINS_PER_GROUP):
        _rwkv_chains(range(g0, min(g0 + A_CHAINS_PER_GROUP, len(ch))), ch, c, s_ref, y_ref,
                     xk, xr, kb, bb, v, kh, bh, gam, strict, incl)

    y = y_ref[...]
    inv_n = 1.0 / A_HEAD_DIM
    mean = _seg_sum(y, e, False) * inv_n
    d = y - mean
    var = _seg_sum(d * d, e, False) * inv_n
    yn = d * lax.rsqrt(var + GN_EPS) * ln_w + ln_b
    bonus = _seg_sum(r * k2 * r_k, e, False) * v
    tok_ref[...] = ((yn + bonus) * gate).astype(tok_ref.dtype)

    @pl.when(ci == pl.num_programs(1) - 1)
    def _():
        for b in range(nb):
            sout_ref[b] = s_ref[b * A_HEADS:(b + 1) * A_HEADS]


def _block_ltri(nb, c):
    t = jnp.arange(nb * c)
    return ((t[None, :] <= t[:, None]) & (t[None, :] // c == t[:, None] // c)).astype(BF16)


def rwkv_mix(p, row0, pshift, s0, mu, vecs, w2, a2, g2, n_batch, seq, c, nb):
    assert nb == 1 or (seq == c and c == 8)
    n_chunks = seq // c
    rows = nb * c
    t0 = row0 // rows
    lane = jnp.arange(LANES)
    e = (lane[:, None] // A_HEAD_DIM == lane[None, :] // A_HEAD_DIM).astype(BF16)
    const = lambda shape: pl.BlockSpec(shape, lambda b, ci: (0,) * len(shape))
    st_spec = pl.BlockSpec((nb, A_HEADS, A_HEAD_DIM, A_HEAD_DIM), lambda b, ci: (b, 0, 0, 0))
    return pl.pallas_call(
        functools.partial(_rwkv_kernel, c=c, nb=nb),
        grid=(n_batch // nb, n_chunks),
        in_specs=[pl.BlockSpec((rows, A_PROJ_COLS), lambda b, ci: (t0 + b * n_chunks + ci, 0)),
                  pl.BlockSpec((nb, 1, A_PROJ_COLS), lambda b, ci: (b, 0, 0)),
                  st_spec,
                  const((1, A_TOK_COLS)), const((8, TOK_WIDTH)),
                  const(w2.shape), const(a2.shape), const(g2.shape), const((LANES, LANES)), const((rows, rows))],
        out_specs=[pl.BlockSpec((rows, TOK_WIDTH), lambda b, ci: (b * n_chunks + ci, 0)), st_spec],
        out_shape=[jax.ShapeDtypeStruct((n_batch * seq, TOK_WIDTH), BF16),
                   jax.ShapeDtypeStruct((n_batch, A_HEADS, A_HEAD_DIM, A_HEAD_DIM), F32)],
        scratch_shapes=[pltpu.VMEM((nb, 1, A_TOK_COLS), F32),
                        pltpu.VMEM((nb * A_HEADS, A_HEAD_DIM, A_HEAD_DIM), F32),
                        pltpu.VMEM((rows, TOK_WIDTH), F32)],
        compiler_params=_cparams("parallel", "arbitrary"),
        name="rwkv_mix",
    )(p, pshift, s0, mu, vecs, w2, a2, g2, e, _block_ltri(nb, c))


def _hgrn_kernel(p_ref, s0_ref, lbp_ref, gn_ref, lt_ref, tok_ref, sout_ref, st_ref, *, c, nb, layer, col0,
                 single_chunk):
    ci = pl.program_id(1)
    tw = TOK_WIDTH
    hd = B_HEAD_DIM
    ch = [(b, h) for b in range(nb) for h in range(B_HEADS)]
    n = range(len(ch))

    if not single_chunk:
        @pl.when(ci == 0)
        def _():
            for i in n:
                st_ref[i] = s0_ref[ch[i][0], ch[i][1]].T

    lbp = lbp_ref[...]
    mx = jnp.max(lbp, axis=0, keepdims=True)
    ex = jnp.exp(lbp - mx)
    den = jnp.sum(ex, axis=0, keepdims=True)
    lb = jnp.zeros((1, tw), F32)
    for i in range(1, layer + 1):
        lb = lb + ex[i:i + 1, :] / den

    q = p_ref[:, col0:col0 + tw].astype(F32)
    f = p_ref[:, col0 + tw:col0 + 2 * tw].astype(F32)
    iv = p_ref[:, col0 + 2 * tw:col0 + 3 * tw].astype(F32)
    og = p_ref[:, col0 + 3 * tw:col0 + 4 * tw].astype(F32)
    fg = lb + (1.0 - lb) * _sigmoid(f)
    lf = jnp.log(fg)
    kk = 1.0 - fg
    qq = q * _sigmoid(q)
    lf_hi, lf_lo = _split(lf)
    bc = _dot(lt_ref[...], lf_hi) + _dot(lt_ref[...], lf_lo)
    blast = _bcast_rows(bc, c - 1, nb, c)
    mid = _bcast_rows(bc, (c - 1) // 2, nb, c)
    qe = qq * jnp.exp(bc - mid)
    ke = kk * jnp.exp(mid - bc)
    qs = qq * jnp.exp(bc)
    kl = kk * jnp.exp(blast - bc)
    gam = jnp.exp(blast)
    gate = og * _sigmoid(og)
    ti = lax.broadcasted_iota(jnp.int32, (c, c), 0)
    si = lax.broadcasted_iota(jnp.int32, (c, c), 1)
    incl = si <= ti
    gn = gn_ref[...]
    rs = [slice(b * c, (b + 1) * c) for b, _ in ch]
    ls = [slice(h * hd, (h + 1) * hd) for _, h in ch]
    v_h = [_bf(iv[rs[i], ls[i]]) for i in n]
    att = [jnp.where(incl, _dot_nt(_bf(qe[rs[i], ls[i]]), _bf(ke[rs[i], ls[i]])), 0.0) for i in n]
    if single_chunk:
        st = [s0_ref[ch[i][0], ch[i][1]] for i in n]
        o = [_dot(_bf(att[i]), v_h[i]) + _dot(_bf(qs[rs[i], ls[i]]), _bf(st[i])) for i in n]
        ones = jnp.ones((c, hd), BF16)
        lf_parts = [_split(lf[rs[i], ls[i]]) for i in n]
        gam_col = [jnp.exp(_dot_tn(hi, ones) + _dot_tn(lo, ones)) for hi, lo in lf_parts]
        st_new = [gam_col[i] * st[i] + _dot_tn(_bf(kl[rs[i], ls[i]]), v_h[i]) for i in n]
    else:
        st = [st_ref[i] for i in n]
        o = [_dot(_bf(att[i]), v_h[i]) + _dot_nt(_bf(qs[rs[i], ls[i]]), _bf(st[i])) for i in n]
        st_new = [st[i] * gam[ch[i][0] * c:ch[i][0] * c + 1, ls[i]] + _dot_tn(v_h[i], _bf(kl[rs[i], ls[i]]))
                  for i in n]
    on = [x * lax.rsqrt(jnp.mean(x * x, axis=-1, keepdims=True) + RMS_EPS) * gn for x in o]
    for i in n:
        tok_ref[rs[i], ls[i]] = (on[i] * gate[rs[i], ls[i]]).astype(tok_ref.dtype)
        if single_chunk:
            sout_ref[ch[i][0], ch[i][1]] = st_new[i]
        else:
            st_ref[i] = st_new[i]

    if not single_chunk:
        @pl.when(ci == pl.num_programs(1) - 1)
        def _():
            for i in n:
                sout_ref[ch[i][0], ch[i][1]] = st_ref[i].T


def hgrn_mix(p, row0, col0, s0, lbp, gn, n_batch, seq, c, nb, layer):
    assert nb == 1 or (seq == c and c == 8)
    cols = p.shape[1]
    n_chunks = seq // c
    rows = nb * c
    t0 = row0 // rows
    const = lambda shape: pl.BlockSpec(shape, lambda b, ci: (0,) * len(shape))
    st_spec = pl.BlockSpec((nb, B_HEADS, B_HEAD_DIM, B_HEAD_DIM), lambda b, ci: (b, 0, 0, 0))
    return pl.pallas_call(
        functools.partial(_hgrn_kernel, c=c, nb=nb, layer=layer, col0=col0, single_chunk=n_chunks == 1),
        grid=(n_batch // nb, n_chunks),
        in_specs=[pl.BlockSpec((rows, cols), lambda b, ci: (t0 + b * n_chunks + ci, 0)),
                  st_spec, const((DEPTH, TOK_WIDTH)), const((1, B_HEAD_DIM)), const((rows, rows))],
        out_specs=[pl.BlockSpec((rows, TOK_WIDTH), lambda b, ci: (b * n_chunks + ci, 0)), st_spec],
        out_shape=[jax.ShapeDtypeStruct((n_batch * seq, TOK_WIDTH), BF16),
                   jax.ShapeDtypeStruct((n_batch, B_HEADS, B_HEAD_DIM, B_HEAD_DIM), F32)],
        scratch_shapes=[pltpu.VMEM((nb * B_HEADS, B_HEAD_DIM, B_HEAD_DIM), F32)],
        compiler_params=_cparams("parallel", "arbitrary"),
        name="hgrn_mix",
    )(p, s0, lbp, gn.reshape(1, B_HEAD_DIM), _block_ltri(nb, c))


PROMPT_CHUNK = 64
A_CHAINS_PER_GROUP = 96
SAMPLE_SEQS_PER_STEP = 4
TM_DENSE = 1024
TM_IN_PROJ = 512


def kernel(x_prompt, x_sample, mem_prompt, cache_mem_k, cache_mem_v, state_rwkv, state_shift, state_hgrn, state_conv, norm_mix, norm_ffn, norm_final, mem_norm, w_mem_kv, a_w_in, a_mu, a_w0, a_w2, a_a0, a_a2, a_g2, a_k_k, a_k_a, a_r_k, a_ln_w, a_ln_b, a_w_out, b_w_in, b_lower_bounds, b_g_norm, b_w_out, ffn_w_up, ffn_conv_w, ffn_conv_b, ffn_w_down):
    bp, sp, d = x_prompt.shape
    bs, ss, _ = x_sample.shape
    mp, ms = bp * sp, bs * ss
    nbs = SAMPLE_SEQS_PER_STEP

    pad_cols = A_PROJ_COLS - a_w_in.shape[2]
    wa = _bf(jnp.pad(a_w_in[0], ((0, 0), (0, pad_cols))))
    mu = jnp.pad(a_mu[0], (0, pad_cols)).reshape(1, A_TOK_COLS)
    vecs = jnp.stack([a_w0[0], a_a0[0], a_k_k[0], a_k_a[0], a_r_k[0], a_ln_w[0], a_ln_b[0],
                      jnp.zeros((TOK_WIDTH,), F32)])
    w2 = _bf(jnp.pad(a_w2[0], ((0, 128 - A_LORA), (0, 0))))
    a2 = _bf(jnp.pad(a_a2[0], ((A_LORA, 256 - 2 * A_LORA), (0, 0))))
    g2 = _bf(jnp.pad(a_g2[0], ((2 * A_LORA - 128, 384 - (2 * A_LORA - 128) - A_GATE_RANK), (0, 0))))
    wb = _bf(b_w_in[0])
    w_out = [_bf(a_w_out[0]), _bf(b_w_out[0])]
    w_up = _bf(ffn_w_up)
    w_down = _bf(ffn_w_down)

    mem_rows = mem_prompt.reshape(bp * N_MEM, d)
    mem_kv = [norm_matmul([mem_rows], mem_norm[l], _bf(w_mem_kv[l]), bp * N_MEM, 512) for l in range(DEPTH)]
    mem_k_prompt = jnp.stack([kv[:, :MEM_WIDTH] for kv in mem_kv]).reshape(DEPTH, bp, N_MEM, MEM_HEADS, MEM_HEAD_DIM)
    mem_v_prompt = jnp.stack([kv[:, MEM_WIDTH:] for kv in mem_kv]).reshape(DEPTH, bp, N_MEM, MEM_HEADS, MEM_HEAD_DIM)

    hist = [state_conv[l].reshape(bs, 2 * D_FF) for l in range(DEPTH)]

    def ffn(x, layer):
        u_p, cs_p = ffn_up_prompt(x, norm_ffn[layer], w_up, layer, ffn_conv_w[layer], ffn_conv_b[layer],
                                  bp, sp, 1024, 512)
        u_s, c6, c7 = ffn_up_sample(x, mp, norm_ffn[layer], w_up, layer, ffn_conv_w[layer], ffn_conv_b[layer],
                                    hist[layer], ss, 1024, 512)
        x_p = matmul([[u_p]], [(w_down, layer)], [x], D_MODEL, TM_DENSE, 512)
        x_s = matmul([[u_s]], [(w_down, layer)], [x], D_MODEL, TM_DENSE, 512, res_row0=mp)
        return [x_p, x_s], cs_p, jnp.stack([c6, c7], axis=1)

    def out_proj(tok, mem_o, w, res_segs):
        return matmul([tok, mem_o], [w[:TOK_WIDTH], w[TOK_WIDTH:]], res_segs, D_MODEL, TM_DENSE, 1024)

    x_segs = [x_prompt.reshape(mp, d), x_sample.reshape(ms, d)]
    p = norm_matmul(x_segs, norm_mix[0], wa, TM_IN_PROJ, A_PROJ_COLS // 2, BF16)
    pshift = matmul([[state_shift[0]]], [wa], None, A_PROJ_COLS, bs, 512).reshape(bs, 1, A_PROJ_COLS)
    tok_p, rwkv_p = rwkv_mix(p, 0, jnp.zeros((bp, 1, A_PROJ_COLS), F32),
                             jnp.zeros((bp, A_HEADS, A_HEAD_DIM, A_HEAD_DIM), F32),
                             mu, vecs, w2, a2, g2, bp, sp, PROMPT_CHUNK, 1)
    tok_s, rwkv_s = rwkv_mix(p, mp, pshift, state_rwkv[0], mu, vecs, w2, a2, g2, bs, ss, ss, nbs)
    mo_p = mem_attention(p, 0, 0, mem_kv[0], bp, sp, 512)
    mo_s = mem_attention_cache(p, mp, 0, cache_mem_k, cache_mem_v, 0, ss, 8)
    x = out_proj([tok_p, tok_s], [mo_p, mo_s], w_out[0], x_segs)
    x_segs, conv_p0, conv_s0 = ffn(x, 0)
    p = norm_matmul(x_segs, norm_mix[1], wb, TM_IN_PROJ, wb.shape[1] // 2, BF16)
    tok_p, hgrn_p = hgrn_mix(p, 0, MEM_WIDTH, jnp.zeros((bp, B_HEADS, B_HEAD_DIM, B_HEAD_DIM), F32),
                             b_lower_bounds, b_g_norm[0], bp, sp, PROMPT_CHUNK, 1, 1)
    tok_s, hgrn_s = hgrn_mix(p, mp, MEM_WIDTH, state_hgrn[0], b_lower_bounds, b_g_norm[0], bs, ss, ss, nbs, 1)
    mo_p = mem_attention(p, 0, 0, mem_kv[1], bp, sp, 512)
    mo_s = mem_attention_cache(p, mp, 0, cache_mem_k, cache_mem_v, 1, ss, 8)
    x = out_proj([tok_p, tok_s], [mo_p, mo_s], w_out[1], x_segs)
    x_segs, conv_p1, conv_s1 = ffn(x, 1)
    y_p = rmsnorm(x_segs[0], norm_final, 256)
    y_s = rmsnorm(x_segs[1], norm_final, 256)
    shift_p = rmsnorm(x_prompt[:, -1], norm_mix[0], bp)
    shift_s = rmsnorm(x_sample[:, -1], norm_mix[0], bs)

    return (y_p.reshape(bp, sp, d), y_s.reshape(bs, ss, d), mem_k_prompt, mem_v_prompt,
            rwkv_p[None], rwkv_s[None], shift_p[None], shift_s[None], hgrn_p[None], hgrn_s[None],
            jnp.stack([conv_p0, conv_p1]), jnp.stack([conv_s0, conv_s1]))
```

```python
import functools

import jax
import jax.numpy as jnp
from jax import lax
from jax.experimental import pallas as pl
from jax.experimental.pallas import tpu as pltpu

F32 = jnp.float32
BF16 = jnp.bfloat16

D_MODEL = 2048
DEPTH = 2
TOK_WIDTH = 1536
MEM_WIDTH = 512
MEM_HEADS = 4
MEM_HEAD_DIM = 128
N_MEM = 256
A_HEAD_DIM = 64
A_HEADS = 24
A_LORA = 96
A_GATE_RANK = 256
A_TOK_COLS = 5120
A_PROJ_COLS = MEM_WIDTH + A_TOK_COLS
B_HEADS = 12
B_HEAD_DIM = 128
D_FF = 5632
RMS_EPS = 1e-6
GN_EPS = 64e-5

LANES = 128
VMEM_LIMIT = 56 * 1024 * 1024


def _cparams(*sem):
    return pltpu.CompilerParams(dimension_semantics=sem, vmem_limit_bytes=VMEM_LIMIT)


def _dot(a, b):
    return jnp.dot(a, b, preferred_element_type=F32)


def _dot_nt(a, b):
    return lax.dot_general(a, b, (((1,), (1,)), ((), ())), preferred_element_type=F32)


def _dot_tn(a, b):
    return lax.dot_general(a, b, (((0,), (0,)), ((), ())), preferred_element_type=F32)


def _bf(x):
    return x.astype(BF16)


def _split(x):
    hi = x.astype(BF16)
    lo = (x - hi.astype(F32)).astype(BF16)
    return hi, lo


def _sigmoid(x):
    return 0.5 * (jnp.tanh(0.5 * x) + 1.0)


def _rms(x, g):
    ms = jnp.mean(x * x, axis=-1, keepdims=True)
    return x * lax.rsqrt(ms + RMS_EPS) * g


def _seg_bounds(segs, tm):
    bounds, off = [], 0
    for a in segs:
        bounds.append(off)
        off += a.shape[0] // tm
    return tuple(bounds), off


def _seg_row_specs(segs, tm, width, col_of_j, tile0=0):
    assert tile0 == 0 or len(segs) == 1
    specs, off = [], 0
    for a in segs:
        n = a.shape[0] // tm
        if col_of_j:
            imap = lambda i, j, off=off, n=n: (jnp.clip(i - off + tile0, 0, n - 1), j)
        else:
            imap = lambda i, j, off=off, n=n: (jnp.clip(i - off + tile0, 0, n - 1), 0)
        specs.append(pl.BlockSpec((tm, width), imap))
        off += n
    return specs


def _for_segment(i, bounds, fn):
    if len(bounds) == 1:
        fn(0)
        return
    for s in range(len(bounds)):
        cond = i >= bounds[s]
        if s + 1 < len(bounds):
            cond = jnp.logical_and(cond, i < bounds[s + 1])
        pl.when(cond)(functools.partial(fn, s))


def _w_spec(w, tn, col_block0=0):
    if isinstance(w, tuple):
        arr, layer = w
        return arr, pl.BlockSpec((None, arr.shape[1], tn), lambda i, j: (layer, 0, col_block0 + j))
    return w, pl.BlockSpec((w.shape[0], tn), lambda i, j: (0, col_block0 + j))


def _rmsnorm_kernel(x_ref, g_ref, o_ref):
    o_ref[...] = _rms(x_ref[...], g_ref[...])


def rmsnorm(x, g, tm, row0=0, rows=None):
    d = x.shape[1]
    rows = x.shape[0] if rows is None else rows
    t0 = row0 // tm
    return pl.pallas_call(
        _rmsnorm_kernel,
        grid=(rows // tm,),
        in_specs=[pl.BlockSpec((tm, d), lambda i: (t0 + i, 0)), pl.BlockSpec((1, d), lambda i: (0, 0))],
        out_specs=pl.BlockSpec((tm, d), lambda i: (i, 0)),
        out_shape=jax.ShapeDtypeStruct((rows, d), F32),
        compiler_params=_cparams("parallel"),
        name="rmsnorm",
    )(x, g.reshape(1, d))


NORM_ROWS = 256


def _norm_to_scratch(x_ref, g, hb_ref, tm, dst_off=0):
    for r in range(0, tm, NORM_ROWS):
        n = min(NORM_ROWS, tm - r)
        hb_ref[dst_off + r:dst_off + r + n, :] = _bf(_rms(x_ref[r:r + n, :], g))


def _norm_matmul_kernel(*refs, n_seg, bounds, tm):
    x_refs = refs[:n_seg]
    g_ref, w_ref, o_ref, hb_ref = refs[n_seg:]

    @pl.when(pl.program_id(1) == 0)
    def _():
        g = g_ref[...]
        _for_segment(pl.program_id(0), bounds, lambda s: _norm_to_scratch(x_refs[s], g, hb_ref, tm))

    o_ref[...] = _dot(hb_ref[...], w_ref[...]).astype(o_ref.dtype)


def norm_matmul(x_segs, g, w, tm, tn, out_dtype=F32):
    d = x_segs[0].shape[1]
    bounds, n_tiles = _seg_bounds(x_segs, tm)
    w_arr, w_spec = _w_spec(w, tn)
    n = w_arr.shape[-1]
    return pl.pallas_call(
        functools.partial(_norm_matmul_kernel, n_seg=len(x_segs), bounds=bounds, tm=tm),
        grid=(n_tiles, n // tn),
        in_specs=_seg_row_specs(x_segs, tm, d, False) + [pl.BlockSpec((1, d), lambda i, j: (0, 0)), w_spec],
        out_specs=pl.BlockSpec((tm, tn), lambda i, j: (i, j)),
        out_shape=jax.ShapeDtypeStruct((n_tiles * tm, n), out_dtype),
        scratch_shapes=[pltpu.VMEM((tm, d), BF16)],
        compiler_params=_cparams("parallel", "arbitrary"),
        name="norm_matmul",
    )(*x_segs, g.reshape(1, d), w_arr)


def _mm_kernel(*refs, a_counts, res_count, bounds):
    pos = 0
    a_refs = []
    for cnt in a_counts:
        a_refs.append(refs[pos:pos + cnt])
        pos += cnt
    w_refs = refs[pos:pos + len(a_counts)]
    pos += len(a_counts)
    res_refs = refs[pos:pos + res_count]
    o_ref = refs[-1]

    def compute(s):
        acc = None
        for k, segs in enumerate(a_refs):
            term = _dot(_bf(segs[min(s, len(segs) - 1)][...]), w_refs[k][...])
            acc = term if acc is None else acc + term
        if res_count:
            acc = res_refs[min(s, res_count - 1)][...] + acc
        o_ref[...] = acc

    _for_segment(pl.program_id(0), bounds, compute)


def matmul(a_list, w_list, res_segs, n_out, tm, tn, res_row0=0):
    longest = max(a_list + ([res_segs] if res_segs else []), key=len)
    bounds, n_tiles = _seg_bounds(longest, tm)
    in_specs, args = [], []
    for segs in a_list:
        assert len(segs) in (1, len(bounds))
        in_specs += _seg_row_specs(segs, tm, segs[0].shape[1], False)
        args += list(segs)
    for w in w_list:
        w_arr, w_spec = _w_spec(w, tn)
        in_specs.append(w_spec)
        args.append(w_arr)
    if res_segs:
        assert len(res_segs) in (1, len(bounds))
        in_specs += _seg_row_specs(res_segs, tm, tn, True, res_row0 // tm)
        args += list(res_segs)
    return pl.pallas_call(
        functools.partial(_mm_kernel, a_counts=tuple(len(s) for s in a_list),
                          res_count=len(res_segs) if res_segs else 0, bounds=bounds),
        grid=(n_tiles, n_out // tn),
        in_specs=in_specs,
        out_specs=pl.BlockSpec((tm, tn), lambda i, j: (i, j)),
        out_shape=jax.ShapeDtypeStruct((n_tiles * tm, n_out), F32),
        compiler_params=_cparams("parallel", "parallel"),
        name="matmul",
    )(*args)


def _attn_kernel(q_ref, k_ref, v_ref, o_ref, *, n_seq, tq, cache_layout):
    scale = MEM_HEAD_DIM ** -0.5
    pairs = [(s, h) for s in range(n_seq) for h in range(MEM_HEADS)]
    hsl = [slice(h * MEM_HEAD_DIM, (h + 1) * MEM_HEAD_DIM) for h in range(MEM_HEADS)]
    q = [_bf(q_ref[s * tq:(s + 1) * tq, hsl[h]]) for s, h in pairs]
    if cache_layout:
        k = [_bf(k_ref[0, s, pl.ds(h, N_MEM, stride=MEM_HEADS), :]) for s, h in pairs]
        v = [_bf(v_ref[0, s, pl.ds(h, N_MEM, stride=MEM_HEADS), :]) for s, h in pairs]
    else:
        k = [_bf(k_ref[s * N_MEM:(s + 1) * N_MEM, hsl[h]]) for s, h in pairs]
        v = [_bf(v_ref[s * N_MEM:(s + 1) * N_MEM, hsl[h]]) for s, h in pairs]
    sc = [_dot_nt(q[i], k[i]) * scale for i in range(len(pairs))]
    e = [jnp.exp(x - jnp.max(x, axis=-1, keepdims=True)) for x in sc]
    p = [_bf(x / jnp.sum(x, axis=-1, keepdims=True)) for x in e]
    o = [_dot(p[i], v[i]) for i in range(len(pairs))]
    rows = [jnp.concatenate(o[s * MEM_HEADS:(s + 1) * MEM_HEADS], axis=1) for s in range(n_seq)]
    o_ref[...] = _bf(jnp.concatenate(rows, axis=0) if n_seq > 1 else rows[0])


def mem_attention(p, row0, q_colblock, kv, n_batch, seq, tq):
    q_tiles = seq // tq
    t0 = row0 // tq
    return pl.pallas_call(
        functools.partial(_attn_kernel, n_seq=1, tq=tq, cache_layout=False),
        grid=(n_batch, q_tiles),
        in_specs=[pl.BlockSpec((tq, MEM_WIDTH), lambda b, t: (t0 + b * q_tiles + t, q_colblock)),
                  pl.BlockSpec((N_MEM, MEM_WIDTH), lambda b, t: (b, 0)),
                  pl.BlockSpec((N_MEM, MEM_WIDTH), lambda b, t: (b, 1))],
        out_specs=pl.BlockSpec((tq, MEM_WIDTH), lambda b, t: (b * q_tiles + t, 0)),
        out_shape=jax.ShapeDtypeStruct((n_batch * seq, MEM_WIDTH), BF16),
        compiler_params=_cparams("parallel", "parallel"),
        name="mem_attention",
    )(p, kv, kv)


def mem_attention_cache(p, row0, q_colblock, cache_k, cache_v, layer, seq, n_seq):
    depth, n_batch = cache_k.shape[:2]
    cache_k = cache_k.reshape(depth, n_batch, N_MEM * MEM_HEADS, MEM_HEAD_DIM)
    cache_v = cache_v.reshape(depth, n_batch, N_MEM * MEM_HEADS, MEM_HEAD_DIM)
    rows = n_seq * seq
    t0 = row0 // rows
    kv_spec = pl.BlockSpec((1, n_seq, N_MEM * MEM_HEADS, MEM_HEAD_DIM), lambda b: (layer, b, 0, 0))
    return pl.pallas_call(
        functools.partial(_attn_kernel, n_seq=n_seq, tq=seq, cache_layout=True),
        grid=(n_batch // n_seq,),
        in_specs=[pl.BlockSpec((rows, MEM_WIDTH), lambda b: (t0 + b, q_colblock)), kv_spec, kv_spec],
        out_specs=pl.BlockSpec((rows, MEM_WIDTH), lambda b: (b, 0)),
        out_shape=jax.ShapeDtypeStruct((n_batch * seq, MEM_WIDTH), BF16),
        compiler_params=_cparams("parallel"),
        name="mem_attention_cache",
    )(p, cache_k, cache_v)


FFN_HALO = 16


def _gelu_gate(c, v):
    return _bf(jax.nn.gelu(c) * v)


def _ffn_up_prompt_kernel(x_ref, xh_ref, g_ref, wa_ref, wv_ref, cw_ref, cb_ref, u_ref, cs_ref, hb_ref,
                          *, tm, tiles_per_seq):
    i = pl.program_id(0)

    @pl.when(pl.program_id(1) == 0)
    def _():
        g = g_ref[...]
        hb_ref[0:FFN_HALO, :] = _bf(_rms(xh_ref[...], g))
        _norm_to_scratch(x_ref, g, hb_ref, tm, dst_off=FFN_HALO)

    a_ext = _dot(hb_ref[...], wa_ref[...])
    v = _dot(hb_ref[FFN_HALO:, :], wv_ref[...])
    rows = lax.broadcasted_iota(jnp.int32, (FFN_HALO + tm, 1), 0)
    n_zero = jnp.where((i % tiles_per_seq) == 0, FFN_HALO, 0)
    a_ext = jnp.where(rows < n_zero, 0.0, a_ext)
    a0 = a_ext[FFN_HALO:]
    a1 = pltpu.roll(a_ext, 1, 0)[FFN_HALO:]
    a2 = pltpu.roll(a_ext, 2, 0)[FFN_HALO:]
    c = cb_ref[...] + a2 * cw_ref[0:1, :] + a1 * cw_ref[1:2, :] + a0 * cw_ref[2:3, :]
    u_ref[...] = _gelu_gate(c, v)
    cs_ref[0] = a0[tm - 8:tm][6:8]


def ffn_up_prompt(x, g, w_up, layer, cw, cb, n_batch, seq, tm, tn):
    d = x.shape[1]
    m = n_batch * seq
    nf = D_FF // tn
    tiles_per_seq = seq // tm
    halo_blocks = tm // FFN_HALO
    w_arr, wa_spec = _w_spec((w_up, layer), tn)
    _, wv_spec = _w_spec((w_up, layer), tn, nf)
    u, cs = pl.pallas_call(
        functools.partial(_ffn_up_prompt_kernel, tm=tm, tiles_per_seq=tiles_per_seq),
        grid=(m // tm, nf),
        in_specs=[pl.BlockSpec((tm, d), lambda i, j: (i, 0)),
                  pl.BlockSpec((FFN_HALO, d), lambda i, j: (jnp.maximum(i * halo_blocks - 1, 0), 0)),
                  pl.BlockSpec((1, d), lambda i, j: (0, 0)),
                  wa_spec, wv_spec,
                  pl.BlockSpec((3, tn), lambda i, j: (0, j)),
                  pl.BlockSpec((1, tn), lambda i, j: (0, j))],
        out_specs=[pl.BlockSpec((tm, tn), lambda i, j: (i, j)),
                   pl.BlockSpec((1, 2, tn), lambda i, j: (i, 0, j))],
        out_shape=[jax.ShapeDtypeStruct((m, D_FF), BF16),
                   jax.ShapeDtypeStruct((m // tm, 2, D_FF), F32)],
        scratch_shapes=[pltpu.VMEM((FFN_HALO + tm, d), BF16)],
        compiler_params=_cparams("parallel", "arbitrary"),
        name="ffn_up_prompt",
    )(x, x, g.reshape(1, d), w_arr, w_arr, cw, cb.reshape(1, D_FF))
    return u, cs.reshape(n_batch, tiles_per_seq, 2, D_FF)[:, -1]


def _ffn_up_sample_kernel(x_ref, g_ref, wa_ref, wv_ref, cw_ref, cb_ref, h0_ref, h1_ref,
                          u_ref, c6_ref, c7_ref, hb_ref, a_ref, a1_ref, a2_ref, *, tm, seq):
    @pl.when(pl.program_id(1) == 0)
    def _():
        _norm_to_scratch(x_ref, g_ref[...], hb_ref, tm)

    a = _dot(hb_ref[...], wa_ref[...])
    v = _dot(hb_ref[...], wv_ref[...])
    r1 = pltpu.roll(a, 1, 0)
    r2 = pltpu.roll(a, 2, 0)
    nseq = tm // seq
    firsts = pl.ds(0, nseq, stride=seq)
    seconds = pl.ds(1, nseq, stride=seq)
    for q in range(a.shape[1] // LANES):
        sl = slice(q * LANES, (q + 1) * LANES)
        a1_ref[q] = r1[:, sl]
        a1_ref[q, firsts, :] = h1_ref[:, sl]
        a2_ref[q] = r2[:, sl]
        a2_ref[q, firsts, :] = h0_ref[:, sl]
        a2_ref[q, seconds, :] = h1_ref[:, sl]
        a_ref[q] = a[:, sl]
        c6_ref[:, sl] = a_ref[q, pl.ds(seq - 2, nseq, stride=seq), :]
        c7_ref[:, sl] = a_ref[q, pl.ds(seq - 1, nseq, stride=seq), :]
    n_q = a.shape[1] // LANES
    a1 = jnp.concatenate([a1_ref[q] for q in range(n_q)], axis=1)
    a2 = jnp.concatenate([a2_ref[q] for q in range(n_q)], axis=1)
    c = cb_ref[...] + a2 * cw_ref[0:1, :] + a1 * cw_ref[1:2, :] + a * cw_ref[2:3, :]
    u_ref[...] = _gelu_gate(c, v)


def ffn_up_sample(x, row0, g, w_up, layer, cw, cb, hist, seq, tm, tn):
    d = x.shape[1]
    m = hist.shape[0] * seq
    nf = D_FF // tn
    nb = tm // seq
    t0 = row0 // tm
    w_arr, wa_spec = _w_spec((w_up, layer), tn)
    _, wv_spec = _w_spec((w_up, layer), tn, nf)
    rows_scratch = pltpu.VMEM((tn // LANES, tm, LANES), F32)
    return pl.pallas_call(
        functools.partial(_ffn_up_sample_kernel, tm=tm, seq=seq),
        grid=(m // tm, nf),
        in_specs=[pl.BlockSpec((tm, d), lambda i, j: (t0 + i, 0)),
                  pl.BlockSpec((1, d), lambda i, j: (0, 0)),
                  wa_spec, wv_spec,
                  pl.BlockSpec((3, tn), lambda i, j: (0, j)),
                  pl.BlockSpec((1, tn), lambda i, j: (0, j)),
                  pl.BlockSpec((nb, tn), lambda i, j: (i, j)),
                  pl.BlockSpec((nb, tn), lambda i, j: (i, nf + j))],
        out_specs=[pl.BlockSpec((tm, tn), lambda i, j: (i, j)),
                   pl.BlockSpec((nb, tn), lambda i, j: (i, j)),
                   pl.BlockSpec((nb, tn), lambda i, j: (i, j))],
        out_shape=[jax.ShapeDtypeStruct((m, D_FF), BF16),
                   jax.ShapeDtypeStruct((m // seq, D_FF), F32),
                   jax.ShapeDtypeStruct((m // seq, D_FF), F32)],
        scratch_shapes=[pltpu.VMEM((tm, d), BF16), rows_scratch, rows_scratch, rows_scratch],
        compiler_params=_cparams("parallel", "arbitrary"),
        name="ffn_up_sample",
    )(x, g.reshape(1, d), w_arr, w_arr, cw, cb.reshape(1, D_FF), hist, hist)


def _seg_sum(x, e, two_pass):
    parts = []
    for j in range(x.shape[1] // LANES):
        xj = x[:, j * LANES:(j + 1) * LANES]
        if two_pass:
            hi, lo = _split(xj)
            parts.append(_dot(hi, e) + _dot(lo, e))
        else:
            parts.append(_dot(_bf(xj), e))
    return jnp.concatenate(parts, axis=1)


def _bcast_rows(x, idx, nb, c):
    parts = [jnp.broadcast_to(x[b * c + idx:b * c + idx + 1, :], (c, x.shape[1])) for b in range(nb)]
    return parts[0] if nb == 1 else jnp.concatenate(parts, axis=0)


def _unit_lower_solve(a_list, rhs_list, c):
    mm = lambda x, y: _dot(_bf(x), _bf(y))
    n = range(len(a_list))
    rows = lax.broadcasted_iota(jnp.int32, (c, c), 0)
    cols = lax.broadcasted_iota(jnp.int32, (c, c), 1)
    eye = (rows == cols).astype(F32)
    blk = min(c, 16)
    if c > blk:
        assert c // blk <= 4
        same = (rows // blk) == (cols // blk)
        ad = [jnp.where(same, a, 0.0) for a in a_list]
        ao = [a_list[i] - ad[i] for i in n]
    else:
        ad = a_list
    t = [eye - ad[i] for i in n]
    pw = ad
    span = 2
    while span < blk:
        pw = [mm(pw[i], pw[i]) for i in n]
        t = [t[i] + mm(t[i], pw[i]) for i in n]
        span *= 2
    x = [mm(t[i], rhs_list[i]) for i in n]
    if c > blk:
        nn = [mm(t[i], ao[i]) for i in n]
        n2 = [mm(nn[i], nn[i]) for i in n]
        x = [x[i] + mm(n2[i], x[i]) for i in n]
        x = [x[i] - mm(nn[i], x[i]) for i in n]
    return x


def _rwkv_chains(n, ch, c, s_old, xk, xr, kb, bb, v, kh, bh, gam, strict, incl):
    rs = {i: slice(ch[i][0] * c, (ch[i][0] + 1) * c) for i in n}
    ls = {i: slice(ch[i][1] * A_HEAD_DIM, (ch[i][1] + 1) * A_HEAD_DIM) for i in n}
    sb = {i: _bf(s_old[i]) for i in n}
    xk_h = {i: _bf(xk[rs[i], ls[i]]) for i in n}
    xr_h = {i: _bf(xr[rs[i], ls[i]]) for i in n}
    kb_h = {i: _bf(kb[rs[i], ls[i]]) for i in n}
    bb_h = {i: _bf(bb[rs[i], ls[i]]) for i in n}
    v_f = {i: v[rs[i], ls[i]] for i in n}
    v_h = {i: _bf(v_f[i]) for i in n}
    a_kk = {i: jnp.where(strict, _dot_nt(xk_h[i], kb_h[i]), 0.0) for i in n}
    a_kb = {i: jnp.where(strict, _dot_nt(xk_h[i], bb_h[i]), 0.0) for i in n}
    a_rk = {i: jnp.where(incl, _dot_nt(xr_h[i], kb_h[i]), 0.0) for i in n}
    a_rb = {i: jnp.where(incl, _dot_nt(xr_h[i], bb_h[i]), 0.0) for i in n}
    rhs = {i: -(_dot_nt(xk_h[i], sb[i]) + _dot(_bf(a_kk[i]), v_h[i])) for i in n}
    u = dict(zip(n, _unit_lower_solve([a_kb[i] for i in n], [rhs[i] for i in n], c)))
    u_h = {i: _bf(u[i]) for i in n}
    y_h = {i: _dot_nt(xr_h[i], sb[i]) + _dot(_bf(a_rk[i]), v_h[i]) + _dot(_bf(a_rb[i]), u_h[i]) for i in n}
    vu = {i: _bf(jnp.concatenate([v_f[i], u[i]], axis=0)) for i in n}
    kbh = {i: _bf(jnp.concatenate([kh[rs[i], ls[i]], bh[rs[i], ls[i]]], axis=0)) for i in n}
    s_new = {i: s_old[i] * gam[ch[i][0] * c:ch[i][0] * c + 1, ls[i]] + _dot_tn(vu[i], kbh[i]) for i in n}
    return y_h, s_new


def _rwkv_kernel(p_ref, ps_ref, s0_ref, mu_ref, vec_ref, w2_ref, a2_ref, g2_ref, e_ref, lt_ref,
                 tok_ref, sout_ref, prev_ref, s_ref, y_ref, *, c, nb, ns):
    ci = pl.program_id(1)
    tw = TOK_WIDTH
    lora0 = 3 * tw
    groups = nb * ns

    @pl.when(ci == 0)
    def _():
        for b in range(nb):
            s_ref[b * A_HEADS:(b + 1) * A_HEADS] = s0_ref[b]
        prev_ref[...] = ps_ref[:, :, MEM_WIDTH:]

    p = p_ref[:, MEM_WIDTH:].astype(F32)
    rows = lax.broadcasted_iota(jnp.int32, (groups * c, 1), 0)
    p_prev = pltpu.roll(p, 1, 0)
    for b in range(nb):
        p_prev = jnp.where(rows == b * ns * c, prev_ref[b], p_prev)
        prev_ref[b] = p[(b + 1) * ns * c - 1:(b + 1) * ns * c, :]
    xm = p + mu_ref[...] * (p_prev - p)

    r = xm[:, 0:tw]
    k = xm[:, tw:2 * tw]
    v = xm[:, 2 * tw:3 * tw]
    xw = xm[:, lora0:lora0 + 128]
    xa = xm[:, lora0:lora0 + 256]
    xg = xm[:, lora0 + 128:lora0 + 512]
    w0, a0, k_k, k_a, r_k, ln_w, ln_b = (vec_ref[i:i + 1, :] for i in range(7))

    z = -(w0 + _dot(_bf(jnp.tanh(xw)), w2_ref[...]))
    softplus = jnp.maximum(z, 0.0) + jnp.log(1.0 + jnp.exp(-jnp.abs(z)))
    ell = -jnp.exp(-softplus - 0.5)
    a = _sigmoid(a0 + _dot(_bf(xa), a2_ref[...]))
    gate = _dot(_bf(_sigmoid(xg)), g2_ref[...])
    kkraw = k * k_k
    k2 = k * (1.0 + (a - 1.0) * k_a)
    e = e_ref[...]
    kap = kkraw * lax.rsqrt(jnp.maximum(_seg_sum(kkraw * kkraw, e, True), 1e-24))
    bet = kap * a

    ell_hi, ell_lo = _split(ell)
    gc = _dot(lt_ref[...], ell_hi) + _dot(lt_ref[...], ell_lo)
    glast = _bcast_rows(gc, c - 1, groups, c)
    egi = jnp.exp(-gc)
    el = jnp.exp(glast - gc)
    xk = kap * jnp.exp(gc - ell)
    xr = r * jnp.exp(gc)
    kb = k2 * egi
    bb = bet * egi
    kh = k2 * el
    bh = bet * el
    gam = jnp.exp(glast)
    ti = lax.broadcasted_iota(jnp.int32, (c, c), 0)
    si = lax.broadcasted_iota(jnp.int32, (c, c), 1)
    strict = si < ti
    incl = si <= ti

    n_chains = nb * A_HEADS
    state = {i: s_ref[i] for i in range(n_chains)}
    y_h = {}
    for s in range(ns):
        ch = [(b * ns + s, h) for b in range(nb) for h in range(A_HEADS)]
        ys, state = _rwkv_chains(range(n_chains), ch, c, state, xk, xr, kb, bb, v, kh, bh, gam, strict, incl)
        for i in range(n_chains):
            y_h[ch[i]] = ys[i]
    for (grp, h), val in y_h.items():
        y_ref[grp * c:(grp + 1) * c, h * A_HEAD_DIM:(h + 1) * A_HEAD_DIM] = val
    for i in range(n_chains):
        s_ref[i] = state[i]

    y = y_ref[...]
    inv_n = 1.0 / A_HEAD_DIM
    mean = _seg_sum(y, e, False) * inv_n
    d = y - mean
    var = _seg_sum(d * d, e, False) * inv_n
    yn = d * lax.rsqrt(var + GN_EPS) * ln_w + ln_b
    bonus = _seg_sum(r * k2 * r_k, e, False) * v
    tok_ref[...] = ((yn + bonus) * gate).astype(tok_ref.dtype)

    @pl.when(ci == pl.num_programs(1) - 1)
    def _():
        for b in range(nb):
            sout_ref[b] = s_ref[b * A_HEADS:(b + 1) * A_HEADS]


def _block_ltri(nb, c):
    t = jnp.arange(nb * c)
    return ((t[None, :] <= t[:, None]) & (t[None, :] // c == t[:, None] // c)).astype(BF16)


def rwkv_mix(p, row0, pshift, s0, mu, vecs, w2, a2, g2, n_batch, seq, c, nb, ns):
    assert (nb == 1 and seq % (ns * c) == 0) or (ns == 1 and seq == c and c == 8)
    n_chunks = seq // (ns * c)
    rows = nb * ns * c
    t0 = row0 // rows
    lane = jnp.arange(LANES)
    e = (lane[:, None] // A_HEAD_DIM == lane[None, :] // A_HEAD_DIM).astype(BF16)
    const = lambda shape: pl.BlockSpec(shape, lambda b, ci: (0,) * len(shape))
    st_spec = pl.BlockSpec((nb, A_HEADS, A_HEAD_DIM, A_HEAD_DIM), lambda b, ci: (b, 0, 0, 0))
    return pl.pallas_call(
        functools.partial(_rwkv_kernel, c=c, nb=nb, ns=ns),
        grid=(n_batch // nb, n_chunks),
        in_specs=[pl.BlockSpec((rows, A_PROJ_COLS), lambda b, ci: (t0 + b * n_chunks + ci, 0)),
                  pl.BlockSpec((nb, 1, A_PROJ_COLS), lambda b, ci: (b, 0, 0)),
                  st_spec,
                  const((1, A_TOK_COLS)), const((8, TOK_WIDTH)),
                  const(w2.shape), const(a2.shape), const(g2.shape), const((LANES, LANES)), const((rows, rows))],
        out_specs=[pl.BlockSpec((rows, TOK_WIDTH), lambda b, ci: (b * n_chunks + ci, 0)), st_spec],
        out_shape=[jax.ShapeDtypeStruct((n_batch * seq, TOK_WIDTH), BF16),
                   jax.ShapeDtypeStruct((n_batch, A_HEADS, A_HEAD_DIM, A_HEAD_DIM), F32)],
        scratch_shapes=[pltpu.VMEM((nb, 1, A_TOK_COLS), F32),
                        pltpu.VMEM((nb * A_HEADS, A_HEAD_DIM, A_HEAD_DIM), F32),
                        pltpu.VMEM((rows, TOK_WIDTH), F32)],
        compiler_params=_cparams("parallel", "arbitrary"),
        name="rwkv_mix",
    )(p, pshift, s0, mu, vecs, w2, a2, g2, e, _block_ltri(nb * ns, c))


def _hgrn_kernel(p_ref, s0_ref, lbp_ref, gn_ref, lt_ref, tok_ref, sout_ref, st_ref, *, c, nb, ns, layer, col0,
                 single_chunk):
    ci = pl.program_id(1)
    tw = TOK_WIDTH
    hd = B_HEAD_DIM
    ch = [(b, h) for b in range(nb) for h in range(B_HEADS)]
    n = range(len(ch))

    if not single_chunk:
        @pl.when(ci == 0)
        def _():
            for i in n:
                st_ref[i] = s0_ref[ch[i][0], ch[i][1]].T

    lbp = lbp_ref[...]
    mx = jnp.max(lbp, axis=0, keepdims=True)
    ex = jnp.exp(lbp - mx)
    den = jnp.sum(ex, axis=0, keepdims=True)
    lb = jnp.zeros((1, tw), F32)
    for i in range(1, layer + 1):
        lb = lb + ex[i:i + 1, :] / den

    q = p_ref[:, col0:col0 + tw].astype(F32)
    f = p_ref[:, col0 + tw:col0 + 2 * tw].astype(F32)
    iv = p_ref[:, col0 + 2 * tw:col0 + 3 * tw].astype(F32)
    og = p_ref[:, col0 + 3 * tw:col0 + 4 * tw].astype(F32)
    fg = lb + (1.0 - lb) * _sigmoid(f)
    lf = jnp.log(fg)
    kk = 1.0 - fg
    qq = q * _sigmoid(q)
    lf_hi, lf_lo = _split(lf)
    bc = _dot(lt_ref[...], lf_hi) + _dot(lt_ref[...], lf_lo)
    blast = _bcast_rows(bc, c - 1, nb * ns, c)
    mid = _bcast_rows(bc, (c - 1) // 2, nb * ns, c)
    qe = qq * jnp.exp(bc - mid)
    ke = kk * jnp.exp(mid - bc)
    qs = qq * jnp.exp(bc)
    kl = kk * jnp.exp(blast - bc)
    gam = jnp.exp(blast)
    gate = og * _sigmoid(og)
    ti = lax.broadcasted_iota(jnp.int32, (c, c), 0)
    si = lax.broadcasted_iota(jnp.int32, (c, c), 1)
    incl = si <= ti
    gn = gn_ref[...]
    ls = [slice(h * hd, (h + 1) * hd) for _, h in ch]
    st = [s0_ref[ch[i][0], ch[i][1]] if single_chunk else st_ref[i] for i in n]
    out = []
    for s in range(ns):
        grp = [ch[i][0] * ns + s for i in n]
        rs = [slice(g * c, (g + 1) * c) for g in grp]
        v_h = [_bf(iv[rs[i], ls[i]]) for i in n]
        att = [jnp.where(incl, _dot_nt(_bf(qe[rs[i], ls[i]]), _bf(ke[rs[i], ls[i]])), 0.0) for i in n]
        if single_chunk:
            o = [_dot(_bf(att[i]), v_h[i]) + _dot(_bf(qs[rs[i], ls[i]]), _bf(st[i])) for i in n]
            ones = jnp.ones((c, hd), BF16)
            lf_parts = [_split(lf[rs[i], ls[i]]) for i in n]
            gam_col = [jnp.exp(_dot_tn(hi, ones) + _dot_tn(lo, ones)) for hi, lo in lf_parts]
            st = [gam_col[i] * st[i] + _dot_tn(_bf(kl[rs[i], ls[i]]), v_h[i]) for i in n]
        else:
            o = [_dot(_bf(att[i]), v_h[i]) + _dot_nt(_bf(qs[rs[i], ls[i]]), _bf(st[i])) for i in n]
            st = [st[i] * gam[grp[i] * c:grp[i] * c + 1, ls[i]] + _dot_tn(v_h[i], _bf(kl[rs[i], ls[i]]))
                  for i in n]
        on = [x * lax.rsqrt(jnp.mean(x * x, axis=-1, keepdims=True) + RMS_EPS) * gn for x in o]
        out += [(rs[i], ls[i], on[i]) for i in n]
    for r_sl, l_sl, val in out:
        tok_ref[r_sl, l_sl] = (val * gate[r_sl, l_sl]).astype(tok_ref.dtype)
    for i in n:
        if single_chunk:
            sout_ref[ch[i][0], ch[i][1]] = st[i]
        else:
            st_ref[i] = st[i]

    if not single_chunk:
        @pl.when(ci == pl.num_programs(1) - 1)
        def _():
            for i in n:
                sout_ref[ch[i][0], ch[i][1]] = st_ref[i].T


def hgrn_mix(p, row0, col0, s0, lbp, gn, n_batch, seq, c, nb, ns, layer):
    assert (nb == 1 and seq % (ns * c) == 0) or (ns == 1 and seq == c and c == 8)
    cols = p.shape[1]
    n_chunks = seq // (ns * c)
    rows = nb * ns * c
    t0 = row0 // rows
    const = lambda shape: pl.BlockSpec(shape, lambda b, ci: (0,) * len(shape))
    st_spec = pl.BlockSpec((nb, B_HEADS, B_HEAD_DIM, B_HEAD_DIM), lambda b, ci: (b, 0, 0, 0))
    return pl.pallas_call(
        functools.partial(_hgrn_kernel, c=c, nb=nb, ns=ns, layer=layer, col0=col0, single_chunk=seq == c),
        grid=(n_batch // nb, n_chunks),
        in_specs=[pl.BlockSpec((rows, cols), lambda b, ci: (t0 + b * n_chunks + ci, 0)),
                  st_spec, const((DEPTH, TOK_WIDTH)), const((1, B_HEAD_DIM)), const((rows, rows))],
        out_specs=[pl.BlockSpec((rows, TOK_WIDTH), lambda b, ci: (b * n_chunks + ci, 0)), st_spec],
        out_shape=[jax.ShapeDtypeStruct((n_batch * seq, TOK_WIDTH), BF16),
                   jax.ShapeDtypeStruct((n_batch, B_HEADS, B_HEAD_DIM, B_HEAD_DIM), F32)],
        scratch_shapes=[pltpu.VMEM((nb * B_HEADS, B_HEAD_DIM, B_HEAD_DIM), F32)],
        compiler_params=_cparams("parallel", "arbitrary"),
        name="hgrn_mix",
    )(p, s0, lbp, gn.reshape(1, B_HEAD_DIM), _block_ltri(nb * ns, c))


PROMPT_CHUNK = 64
A_CHUNKS_PER_STEP = 2
B_CHUNKS_PER_STEP = 4
SAMPLE_SEQS_PER_STEP = 4
TM_DENSE = 1024
TM_IN_PROJ = 512


def kernel(x_prompt, x_sample, mem_prompt, cache_mem_k, cache_mem_v, state_rwkv, state_shift, state_hgrn, state_conv, norm_mix, norm_ffn, norm_final, mem_norm, w_mem_kv, a_w_in, a_mu, a_w0, a_w2, a_a0, a_a2, a_g2, a_k_k, a_k_a, a_r_k, a_ln_w, a_ln_b, a_w_out, b_w_in, b_lower_bounds, b_g_norm, b_w_out, ffn_w_up, ffn_conv_w, ffn_conv_b, ffn_w_down):
    bp, sp, d = x_prompt.shape
    bs, ss, _ = x_sample.shape
    mp, ms = bp * sp, bs * ss
    nbs = SAMPLE_SEQS_PER_STEP

    pad_cols = A_PROJ_COLS - a_w_in.shape[2]
    wa = _bf(jnp.pad(a_w_in[0], ((0, 0), (0, pad_cols))))
    mu = jnp.pad(a_mu[0], (0, pad_cols)).reshape(1, A_TOK_COLS)
    vecs = jnp.stack([a_w0[0], a_a0[0], a_k_k[0], a_k_a[0], a_r_k[0], a_ln_w[0], a_ln_b[0],
                      jnp.zeros((TOK_WIDTH,), F32)])
    w2 = _bf(jnp.pad(a_w2[0], ((0, 128 - A_LORA), (0, 0))))
    a2 = _bf(jnp.pad(a_a2[0], ((A_LORA, 256 - 2 * A_LORA), (0, 0))))
    g2 = _bf(jnp.pad(a_g2[0], ((2 * A_LORA - 128, 384 - (2 * A_LORA - 128) - A_GATE_RANK), (0, 0))))
    wb = _bf(b_w_in[0])
    w_out = [_bf(a_w_out[0]), _bf(b_w_out[0])]
    w_up = _bf(ffn_w_up)
    w_down = _bf(ffn_w_down)

    mem_rows = mem_prompt.reshape(bp * N_MEM, d)
    mem_kv = [norm_matmul([mem_rows], mem_norm[l], _bf(w_mem_kv[l]), bp * N_MEM, 512) for l in range(DEPTH)]
    mem_k_prompt = jnp.stack([kv[:, :MEM_WIDTH] for kv in mem_kv]).reshape(DEPTH, bp, N_MEM, MEM_HEADS, MEM_HEAD_DIM)
    mem_v_prompt = jnp.stack([kv[:, MEM_WIDTH:] for kv in mem_kv]).reshape(DEPTH, bp, N_MEM, MEM_HEADS, MEM_HEAD_DIM)

    hist = [state_conv[l].reshape(bs, 2 * D_FF) for l in range(DEPTH)]

    def ffn(x, layer):
        u_p, cs_p = ffn_up_prompt(x, norm_ffn[layer], w_up, layer, ffn_conv_w[layer], ffn_conv_b[layer],
                                  bp, sp, 1024, 512)
        u_s, c6, c7 = ffn_up_sample(x, mp, norm_ffn[layer], w_up, layer, ffn_conv_w[layer], ffn_conv_b[layer],
                                    hist[layer], ss, 1024, 512)
        x_p = matmul([[u_p]], [(w_down, layer)], [x], D_MODEL, TM_DENSE, 512)
        x_s = matmul([[u_s]], [(w_down, layer)], [x], D_MODEL, TM_DENSE, 512, res_row0=mp)
        return [x_p, x_s], cs_p, jnp.stack([c6, c7], axis=1)

    def out_proj(tok, mem_o, w, res_segs):
        return matmul([tok, mem_o], [w[:TOK_WIDTH], w[TOK_WIDTH:]], res_segs, D_MODEL, TM_DENSE, 1024)

    x_segs = [x_prompt.reshape(mp, d), x_sample.reshape(ms, d)]
    p = norm_matmul(x_segs, norm_mix[0], wa, TM_IN_PROJ, A_PROJ_COLS // 2, BF16)
    pshift = matmul([[state_shift[0]]], [wa], None, A_PROJ_COLS, bs, 512).reshape(bs, 1, A_PROJ_COLS)
    tok_p, rwkv_p = rwkv_mix(p, 0, jnp.zeros((bp, 1, A_PROJ_COLS), F32),
                             jnp.zeros((bp, A_HEADS, A_HEAD_DIM, A_HEAD_DIM), F32),
                             mu, vecs, w2, a2, g2, bp, sp, PROMPT_CHUNK, 1, A_CHUNKS_PER_STEP)
    tok_s, rwkv_s = rwkv_mix(p, mp, pshift, state_rwkv[0], mu, vecs, w2, a2, g2, bs, ss, ss, nbs, 1)
    mo_p = mem_attention(p, 0, 0, mem_kv[0], bp, sp, 512)
    mo_s = mem_attention_cache(p, mp, 0, cache_mem_k, cache_mem_v, 0, ss, 8)
    x = out_proj([tok_p, tok_s], [mo_p, mo_s], w_out[0], x_segs)
    x_segs, conv_p0, conv_s0 = ffn(x, 0)
    p = norm_matmul(x_segs, norm_mix[1], wb, TM_IN_PROJ, wb.shape[1] // 2, BF16)
    tok_p, hgrn_p = hgrn_mix(p, 0, MEM_WIDTH, jnp.zeros((bp, B_HEADS, B_HEAD_DIM, B_HEAD_DIM), F32),
                             b_lower_bounds, b_g_norm[0], bp, sp, PROMPT_CHUNK, 1, B_CHUNKS_PER_STEP, 1)
    tok_s, hgrn_s = hgrn_mix(p, mp, MEM_WIDTH, state_hgrn[0], b_lower_bounds, b_g_norm[0], bs, ss, ss, nbs, 1, 1)
    mo_p = mem_attention(p, 0, 0, mem_kv[1], bp, sp, 512)
    mo_s = mem_attention_cache(p, mp, 0, cache_mem_k, cache_mem_v, 1, ss, 8)
    x = out_proj([tok_p, tok_s], [mo_p, mo_s], w_out[1], x_segs)
    x_segs, conv_p1, conv_s1 = ffn(x, 1)
    y_p = rmsnorm(x_segs[0], norm_final, 256)
    y_s = rmsnorm(x_segs[1], norm_final, 256)
    shift_p = rmsnorm(x_prompt[:, -1], norm_mix[0], bp)
    shift_s = rmsnorm(x_sample[:, -1], norm_mix[0], bs)

    return (y_p.reshape(bp, sp, d), y_s.reshape(bs, ss, d), mem_k_prompt, mem_v_prompt,
            rwkv_p[None], rwkv_s[None], shift_p[None], shift_s[None], hgrn_p[None], hgrn_s[None],
            jnp.stack([conv_p0, conv_p1]), jnp.stack([conv_s0, conv_s1]))
```

```python
import functools

import jax
import jax.numpy as jnp
from jax import lax
from jax.experimental import pallas as pl
from jax.experimental.pallas import tpu as pltpu

F32 = jnp.float32
BF16 = jnp.bfloat16

D_MODEL = 2048
DEPTH = 2
TOK_WIDTH = 1536
MEM_WIDTH = 512
MEM_HEADS = 4
MEM_HEAD_DIM = 128
N_MEM = 256
A_HEAD_DIM = 64
A_HEADS = 24
A_LORA = 96
A_GATE_RANK = 256
A_TOK_COLS = 5120
A_PROJ_COLS = MEM_WIDTH + A_TOK_COLS
B_HEADS = 12
B_HEAD_DIM = 128
D_FF = 5632
RMS_EPS = 1e-6
GN_EPS = 64e-5

LANES = 128
VMEM_LIMIT = 56 * 1024 * 1024


def _cparams(*sem):
    return pltpu.CompilerParams(dimension_semantics=sem, vmem_limit_bytes=VMEM_LIMIT)


def _dot(a, b):
    return jnp.dot(a, b, preferred_element_type=F32)


def _dot_nt(a, b):
    return lax.dot_general(a, b, (((1,), (1,)), ((), ())), preferred_element_type=F32)


def _dot_tn(a, b):
    return lax.dot_general(a, b, (((0,), (0,)), ((), ())), preferred_element_type=F32)


def _bf(x):
    return x.astype(BF16)


def _split(x):
    hi = x.astype(BF16)
    lo = (x - hi.astype(F32)).astype(BF16)
    return hi, lo


def _sigmoid(x):
    return 0.5 * (jnp.tanh(0.5 * x) + 1.0)


def _rms(x, g):
    ms = jnp.mean(x * x, axis=-1, keepdims=True)
    return x * lax.rsqrt(ms + RMS_EPS) * g


def _seg_bounds(segs, tm):
    bounds, off = [], 0
    for a in segs:
        bounds.append(off)
        off += a.shape[0] // tm
    return tuple(bounds), off


def _seg_row_specs(segs, tm, width, col_of_j, tile0=0):
    assert tile0 == 0 or len(segs) == 1
    specs, off = [], 0
    for a in segs:
        n = a.shape[0] // tm
        if col_of_j:
            imap = lambda i, j, off=off, n=n: (jnp.clip(i - off + tile0, 0, n - 1), j)
        else:
            imap = lambda i, j, off=off, n=n: (jnp.clip(i - off + tile0, 0, n - 1), 0)
        specs.append(pl.BlockSpec((tm, width), imap))
        off += n
    return specs


def _for_segment(i, bounds, fn):
    if len(bounds) == 1:
        fn(0)
        return
    for s in range(len(bounds)):
        cond = i >= bounds[s]
        if s + 1 < len(bounds):
            cond = jnp.logical_and(cond, i < bounds[s + 1])
        pl.when(cond)(functools.partial(fn, s))


def _w_spec(w, tn, col_block0=0):
    if isinstance(w, tuple):
        arr, layer = w
        return arr, pl.BlockSpec((None, arr.shape[1], tn), lambda i, j: (layer, 0, col_block0 + j))
    return w, pl.BlockSpec((w.shape[0], tn), lambda i, j: (0, col_block0 + j))


def _rmsnorm_kernel(x_ref, g_ref, o_ref):
    o_ref[...] = _rms(x_ref[...], g_ref[...])


def rmsnorm(x, g, tm, row0=0, rows=None):
    d = x.shape[1]
    rows = x.shape[0] if rows is None else rows
    t0 = row0 // tm
    return pl.pallas_call(
        _rmsnorm_kernel,
        grid=(rows // tm,),
        in_specs=[pl.BlockSpec((tm, d), lambda i: (t0 + i, 0)), pl.BlockSpec((1, d), lambda i: (0, 0))],
        out_specs=pl.BlockSpec((tm, d), lambda i: (i, 0)),
        out_shape=jax.ShapeDtypeStruct((rows, d), F32),
        compiler_params=_cparams("parallel"),
        name="rmsnorm",
    )(x, g.reshape(1, d))


NORM_ROWS = 256


def _norm_to_scratch(x_ref, g, hb_ref, tm, dst_off=0):
    for r in range(0, tm, NORM_ROWS):
        n = min(NORM_ROWS, tm - r)
        hb_ref[dst_off + r:dst_off + r + n, :] = _bf(_rms(x_ref[r:r + n, :], g))


def _norm_matmul_kernel(*refs, n_seg, bounds, tm):
    x_refs = refs[:n_seg]
    g_ref, w_ref, o_ref, hb_ref = refs[n_seg:]

    @pl.when(pl.program_id(1) == 0)
    def _():
        g = g_ref[...]
        _for_segment(pl.program_id(0), bounds, lambda s: _norm_to_scratch(x_refs[s], g, hb_ref, tm))

    o_ref[...] = _dot(hb_ref[...], w_ref[...]).astype(o_ref.dtype)


def norm_matmul(x_segs, g, w, tm, tn, out_dtype=F32):
    d = x_segs[0].shape[1]
    bounds, n_tiles = _seg_bounds(x_segs, tm)
    w_arr, w_spec = _w_spec(w, tn)
    n = w_arr.shape[-1]
    return pl.pallas_call(
        functools.partial(_norm_matmul_kernel, n_seg=len(x_segs), bounds=bounds, tm=tm),
        grid=(n_tiles, n // tn),
        in_specs=_seg_row_specs(x_segs, tm, d, False) + [pl.BlockSpec((1, d), lambda i, j: (0, 0)), w_spec],
        out_specs=pl.BlockSpec((tm, tn), lambda i, j: (i, j)),
        out_shape=jax.ShapeDtypeStruct((n_tiles * tm, n), out_dtype),
        scratch_shapes=[pltpu.VMEM((tm, d), BF16)],
        compiler_params=_cparams("parallel", "arbitrary"),
        name="norm_matmul",
    )(*x_segs, g.reshape(1, d), w_arr)


def _mm_kernel(*refs, a_counts, res_count, bounds, row_axis):
    pos = 0
    a_refs = []
    for cnt in a_counts:
        a_refs.append(refs[pos:pos + cnt])
        pos += cnt
    w_refs = refs[pos:pos + len(a_counts)]
    pos += len(a_counts)
    res_refs = refs[pos:pos + res_count]
    o_ref = refs[-1]

    def compute(s):
        acc = None
        for k, segs in enumerate(a_refs):
            term = _dot(_bf(segs[min(s, len(segs) - 1)][...]), w_refs[k][...])
            acc = term if acc is None else acc + term
        if res_count:
            acc = res_refs[min(s, res_count - 1)][...] + acc
        o_ref[...] = acc

    _for_segment(pl.program_id(row_axis), bounds, compute)


def matmul(a_list, w_list, res_segs, n_out, tm, tn, res_row0=0, cols_outer=False):
    longest = max(a_list + ([res_segs] if res_segs else []), key=len)
    bounds, n_tiles = _seg_bounds(longest, tm)
    in_specs, args = [], []
    for segs in a_list:
        assert len(segs) in (1, len(bounds))
        in_specs += _seg_row_specs(segs, tm, segs[0].shape[1], False)
        args += list(segs)
    for w in w_list:
        w_arr, w_spec = _w_spec(w, tn)
        in_specs.append(w_spec)
        args.append(w_arr)
    if res_segs:
        assert len(res_segs) in (1, len(bounds))
        in_specs += _seg_row_specs(res_segs, tm, tn, True, res_row0 // tm)
        args += list(res_segs)
    out_spec = pl.BlockSpec((tm, tn), lambda i, j: (i, j))
    grid = (n_tiles, n_out // tn)
    if cols_outer:
        swap = lambda spec: pl.BlockSpec(spec.block_shape, lambda j, i, f=spec.index_map: f(i, j))
        in_specs, out_spec, grid = [swap(sp) for sp in in_specs], swap(out_spec), grid[::-1]
    return pl.pallas_call(
        functools.partial(_mm_kernel, a_counts=tuple(len(s) for s in a_list),
                          res_count=len(res_segs) if res_segs else 0, bounds=bounds,
                          row_axis=1 if cols_outer else 0),
        grid=grid,
        in_specs=in_specs,
        out_specs=out_spec,
        out_shape=jax.ShapeDtypeStruct((n_tiles * tm, n_out), F32),
        compiler_params=_cparams("parallel", "parallel"),
        name="matmul",
    )(*args)


def _attn_kernel(q_ref, k_ref, v_ref, o_ref, *, n_seq, tq, cache_layout):
    scale = MEM_HEAD_DIM ** -0.5
    pairs = [(s, h) for s in range(n_seq) for h in range(MEM_HEADS)]
    hsl = [slice(h * MEM_HEAD_DIM, (h + 1) * MEM_HEAD_DIM) for h in range(MEM_HEADS)]
    q = [_bf(q_ref[s * tq:(s + 1) * tq, hsl[h]]) for s, h in pairs]
    if cache_layout:
        k = [_bf(k_ref[0, s, pl.ds(h, N_MEM, stride=MEM_HEADS), :]) for s, h in pairs]
        v = [_bf(v_ref[0, s, pl.ds(h, N_MEM, stride=MEM_HEADS), :]) for s, h in pairs]
    else:
        k = [_bf(k_ref[s * N_MEM:(s + 1) * N_MEM, hsl[h]]) for s, h in pairs]
        v = [_bf(v_ref[s * N_MEM:(s + 1) * N_MEM, hsl[h]]) for s, h in pairs]
    sc = [_dot_nt(q[i], k[i]) * scale for i in range(len(pairs))]
    e = [jnp.exp(x - jnp.max(x, axis=-1, keepdims=True)) for x in sc]
    p = [_bf(x / jnp.sum(x, axis=-1, keepdims=True)) for x in e]
    o = [_dot(p[i], v[i]) for i in range(len(pairs))]
    rows = [jnp.concatenate(o[s * MEM_HEADS:(s + 1) * MEM_HEADS], axis=1) for s in range(n_seq)]
    o_ref[...] = _bf(jnp.concatenate(rows, axis=0) if n_seq > 1 else rows[0])


def mem_attention(p, row0, q_colblock, kv, n_batch, seq, tq):
    q_tiles = seq // tq
    t0 = row0 // tq
    return pl.pallas_call(
        functools.partial(_attn_kernel, n_seq=1, tq=tq, cache_layout=False),
        grid=(n_batch, q_tiles),
        in_specs=[pl.BlockSpec((tq, MEM_WIDTH), lambda b, t: (t0 + b * q_tiles + t, q_colblock)),
                  pl.BlockSpec((N_MEM, MEM_WIDTH), lambda b, t: (b, 0)),
                  pl.BlockSpec((N_MEM, MEM_WIDTH), lambda b, t: (b, 1))],
        out_specs=pl.BlockSpec((tq, MEM_WIDTH), lambda b, t: (b * q_tiles + t, 0)),
        out_shape=jax.ShapeDtypeStruct((n_batch * seq, MEM_WIDTH), BF16),
        compiler_params=_cparams("parallel", "parallel"),
        name="mem_attention",
    )(p, kv, kv)


def mem_attention_cache(p, row0, q_colblock, cache_k, cache_v, layer, seq, n_seq):
    depth, n_batch = cache_k.shape[:2]
    cache_k = cache_k.reshape(depth, n_batch, N_MEM * MEM_HEADS, MEM_HEAD_DIM)
    cache_v = cache_v.reshape(depth, n_batch, N_MEM * MEM_HEADS, MEM_HEAD_DIM)
    rows = n_seq * seq
    t0 = row0 // rows
    kv_spec = pl.BlockSpec((1, n_seq, N_MEM * MEM_HEADS, MEM_HEAD_DIM), lambda b: (layer, b, 0, 0))
    return pl.pallas_call(
        functools.partial(_attn_kernel, n_seq=n_seq, tq=seq, cache_layout=True),
        grid=(n_batch // n_seq,),
        in_specs=[pl.BlockSpec((rows, MEM_WIDTH), lambda b: (t0 + b, q_colblock)), kv_spec, kv_spec],
        out_specs=pl.BlockSpec((rows, MEM_WIDTH), lambda b: (b, 0)),
        out_shape=jax.ShapeDtypeStruct((n_batch * seq, MEM_WIDTH), BF16),
        compiler_params=_cparams("parallel"),
        name="mem_attention_cache",
    )(p, cache_k, cache_v)


FFN_HALO = 16


def _gelu_gate(c, v):
    return _bf(jax.nn.gelu(c) * v)


def _ffn_up_prompt_kernel(x_ref, xh_ref, g_ref, wa_ref, wv_ref, cw_ref, cb_ref, u_ref, cs_ref, hb_ref,
                          *, tm, tiles_per_seq):
    i = pl.program_id(0)

    @pl.when(pl.program_id(1) == 0)
    def _():
        g = g_ref[...]
        hb_ref[0:FFN_HALO, :] = _bf(_rms(xh_ref[...], g))
        _norm_to_scratch(x_ref, g, hb_ref, tm, dst_off=FFN_HALO)

    a_ext = _dot(hb_ref[...], wa_ref[...])
    v = _dot(hb_ref[FFN_HALO:, :], wv_ref[...])
    rows = lax.broadcasted_iota(jnp.int32, (FFN_HALO + tm, 1), 0)
    n_zero = jnp.where((i % tiles_per_seq) == 0, FFN_HALO, 0)
    a_ext = jnp.where(rows < n_zero, 0.0, a_ext)
    a0 = a_ext[FFN_HALO:]
    a1 = pltpu.roll(a_ext, 1, 0)[FFN_HALO:]
    a2 = pltpu.roll(a_ext, 2, 0)[FFN_HALO:]
    c = cb_ref[...] + a2 * cw_ref[0:1, :] + a1 * cw_ref[1:2, :] + a0 * cw_ref[2:3, :]
    u_ref[...] = _gelu_gate(c, v)
    cs_ref[0] = a0[tm - 8:tm][6:8]


def ffn_up_prompt(x, g, w_up, layer, cw, cb, n_batch, seq, tm, tn):
    d = x.shape[1]
    m = n_batch * seq
    nf = D_FF // tn
    tiles_per_seq = seq // tm
    halo_blocks = tm // FFN_HALO
    w_arr, wa_spec = _w_spec((w_up, layer), tn)
    _, wv_spec = _w_spec((w_up, layer), tn, nf)
    u, cs = pl.pallas_call(
        functools.partial(_ffn_up_prompt_kernel, tm=tm, tiles_per_seq=tiles_per_seq),
        grid=(m // tm, nf),
        in_specs=[pl.BlockSpec((tm, d), lambda i, j: (i, 0)),
                  pl.BlockSpec((FFN_HALO, d), lambda i, j: (jnp.maximum(i * halo_blocks - 1, 0), 0)),
                  pl.BlockSpec((1, d), lambda i, j: (0, 0)),
                  wa_spec, wv_spec,
                  pl.BlockSpec((3, tn), lambda i, j: (0, j)),
                  pl.BlockSpec((1, tn), lambda i, j: (0, j))],
        out_specs=[pl.BlockSpec((tm, tn), lambda i, j: (i, j)),
                   pl.BlockSpec((1, 2, tn), lambda i, j: (i, 0, j))],
        out_shape=[jax.ShapeDtypeStruct((m, D_FF), BF16),
                   jax.ShapeDtypeStruct((m // tm, 2, D_FF), F32)],
        scratch_shapes=[pltpu.VMEM((FFN_HALO + tm, d), BF16)],
        compiler_params=_cparams("parallel", "arbitrary"),
        name="ffn_up_prompt",
    )(x, x, g.reshape(1, d), w_arr, w_arr, cw, cb.reshape(1, D_FF))
    return u, cs.reshape(n_batch, tiles_per_seq, 2, D_FF)[:, -1]


def _ffn_up_sample_kernel(x_ref, g_ref, wa_ref, wv_ref, cw_ref, cb_ref, h0_ref, h1_ref,
                          u_ref, c6_ref, c7_ref, hb_ref, a_ref, a1_ref, a2_ref, *, tm, seq):
    @pl.when(pl.program_id(1) == 0)
    def _():
        _norm_to_scratch(x_ref, g_ref[...], hb_ref, tm)

    a = _dot(hb_ref[...], wa_ref[...])
    v = _dot(hb_ref[...], wv_ref[...])
    r1 = pltpu.roll(a, 1, 0)
    r2 = pltpu.roll(a, 2, 0)
    nseq = tm // seq
    firsts = pl.ds(0, nseq, stride=seq)
    seconds = pl.ds(1, nseq, stride=seq)
    for q in range(a.shape[1] // LANES):
        sl = slice(q * LANES, (q + 1) * LANES)
        a1_ref[q] = r1[:, sl]
        a1_ref[q, firsts, :] = h1_ref[:, sl]
        a2_ref[q] = r2[:, sl]
        a2_ref[q, firsts, :] = h0_ref[:, sl]
        a2_ref[q, seconds, :] = h1_ref[:, sl]
        a_ref[q] = a[:, sl]
        c6_ref[:, sl] = a_ref[q, pl.ds(seq - 2, nseq, stride=seq), :]
        c7_ref[:, sl] = a_ref[q, pl.ds(seq - 1, nseq, stride=seq), :]
    n_q = a.shape[1] // LANES
    a1 = jnp.concatenate([a1_ref[q] for q in range(n_q)], axis=1)
    a2 = jnp.concatenate([a2_ref[q] for q in range(n_q)], axis=1)
    c = cb_ref[...] + a2 * cw_ref[0:1, :] + a1 * cw_ref[1:2, :] + a * cw_ref[2:3, :]
    u_ref[...] = _gelu_gate(c, v)


def ffn_up_sample(x, row0, g, w_up, layer, cw, cb, hist, seq, tm, tn):
    d = x.shape[1]
    m = hist.shape[0] * seq
    nf = D_FF // tn
    nb = tm // seq
    t0 = row0 // tm
    w_arr, wa_spec = _w_spec((w_up, layer), tn)
    _, wv_spec = _w_spec((w_up, layer), tn, nf)
    rows_scratch = pltpu.VMEM((tn // LANES, tm, LANES), F32)
    return pl.pallas_call(
        functools.partial(_ffn_up_sample_kernel, tm=tm, seq=seq),
        grid=(m // tm, nf),
        in_specs=[pl.BlockSpec((tm, d), lambda i, j: (t0 + i, 0)),
                  pl.BlockSpec((1, d), lambda i, j: (0, 0)),
                  wa_spec, wv_spec,
                  pl.BlockSpec((3, tn), lambda i, j: (0, j)),
                  pl.BlockSpec((1, tn), lambda i, j: (0, j)),
                  pl.BlockSpec((nb, tn), lambda i, j: (i, j)),
                  pl.BlockSpec((nb, tn), lambda i, j: (i, nf + j))],
        out_specs=[pl.BlockSpec((tm, tn), lambda i, j: (i, j)),
                   pl.BlockSpec((nb, tn), lambda i, j: (i, j)),
                   pl.BlockSpec((nb, tn), lambda i, j: (i, j))],
        out_shape=[jax.ShapeDtypeStruct((m, D_FF), BF16),
                   jax.ShapeDtypeStruct((m // seq, D_FF), F32),
                   jax.ShapeDtypeStruct((m // seq, D_FF), F32)],
        scratch_shapes=[pltpu.VMEM((tm, d), BF16), rows_scratch, rows_scratch, rows_scratch],
        compiler_params=_cparams("parallel", "arbitrary"),
        name="ffn_up_sample",
    )(x, g.reshape(1, d), w_arr, w_arr, cw, cb.reshape(1, D_FF), hist, hist)


def _seg_sum(x, e, two_pass):
    parts = []
    for j in range(x.shape[1] // LANES):
        xj = x[:, j * LANES:(j + 1) * LANES]
        if two_pass:
            hi, lo = _split(xj)
            parts.append(_dot(hi, e) + _dot(lo, e))
        else:
            parts.append(_dot(_bf(xj), e))
    return jnp.concatenate(parts, axis=1)


def _bcast_rows(x, idx, nb, c):
    parts = [jnp.broadcast_to(x[b * c + idx:b * c + idx + 1, :], (c, x.shape[1])) for b in range(nb)]
    return parts[0] if nb == 1 else jnp.concatenate(parts, axis=0)


def _unit_lower_solve(a_list, rhs_list, c):
    mm = lambda x, y: _dot(_bf(x), _bf(y))
    n = range(len(a_list))
    rows = lax.broadcasted_iota(jnp.int32, (c, c), 0)
    cols = lax.broadcasted_iota(jnp.int32, (c, c), 1)
    eye = (rows == cols).astype(F32)
    blk = min(c, 16)
    if c > blk:
        assert c // blk <= 4
        same = (rows // blk) == (cols // blk)
        ad = [jnp.where(same, a, 0.0) for a in a_list]
        ao = [a_list[i] - ad[i] for i in n]
    else:
        ad = a_list
    t = [eye - ad[i] for i in n]
    pw = ad
    span = 2
    while span < blk:
        pw = [mm(pw[i], pw[i]) for i in n]
        t = [t[i] + mm(t[i], pw[i]) for i in n]
        span *= 2
    x = [mm(t[i], rhs_list[i]) for i in n]
    if c > blk:
        nn = [mm(t[i], ao[i]) for i in n]
        n2 = [mm(nn[i], nn[i]) for i in n]
        x = [x[i] + mm(n2[i], x[i]) for i in n]
        x = [x[i] - mm(nn[i], x[i]) for i in n]
    return x


def _rwkv_chains(n, ch, c, s_old, xk, xr, kb, bb, v, kh, bh, gam, strict, incl):
    rs = {i: slice(ch[i][0] * c, (ch[i][0] + 1) * c) for i in n}
    ls = {i: slice(ch[i][1] * A_HEAD_DIM, (ch[i][1] + 1) * A_HEAD_DIM) for i in n}
    sb = {i: _bf(s_old[i]) for i in n}
    xk_h = {i: _bf(xk[rs[i], ls[i]]) for i in n}
    xr_h = {i: _bf(xr[rs[i], ls[i]]) for i in n}
    kb_h = {i: _bf(kb[rs[i], ls[i]]) for i in n}
    bb_h = {i: _bf(bb[rs[i], ls[i]]) for i in n}
    v_f = {i: v[rs[i], ls[i]] for i in n}
    v_h = {i: _bf(v_f[i]) for i in n}
    a_kk = {i: jnp.where(strict, _dot_nt(xk_h[i], kb_h[i]), 0.0) for i in n}
    a_kb = {i: jnp.where(strict, _dot_nt(xk_h[i], bb_h[i]), 0.0) for i in n}
    a_rk = {i: jnp.where(incl, _dot_nt(xr_h[i], kb_h[i]), 0.0) for i in n}
    a_rb = {i: jnp.where(incl, _dot_nt(xr_h[i], bb_h[i]), 0.0) for i in n}
    rhs = {i: -(_dot_nt(xk_h[i], sb[i]) + _dot(_bf(a_kk[i]), v_h[i])) for i in n}
    u = dict(zip(n, _unit_lower_solve([a_kb[i] for i in n], [rhs[i] for i in n], c)))
    u_h = {i: _bf(u[i]) for i in n}
    y_h = {i: _dot_nt(xr_h[i], sb[i]) + _dot(_bf(a_rk[i]), v_h[i]) + _dot(_bf(a_rb[i]), u_h[i]) for i in n}
    vu = {i: _bf(jnp.concatenate([v_f[i], u[i]], axis=0)) for i in n}
    kbh = {i: _bf(jnp.concatenate([kh[rs[i], ls[i]], bh[rs[i], ls[i]]], axis=0)) for i in n}
    s_new = {i: s_old[i] * gam[ch[i][0] * c:ch[i][0] * c + 1, ls[i]] + _dot_tn(vu[i], kbh[i]) for i in n}
    return y_h, s_new


def _rwkv_kernel(p_ref, ps_ref, s0_ref, mu_ref, vec_ref, w2_ref, a2_ref, g2_ref, e_ref, lt_ref,
                 tok_ref, sout_ref, prev_ref, s_ref, y_ref, *, c, nb, ns):
    ci = pl.program_id(1)
    tw = TOK_WIDTH
    lora0 = 3 * tw
    groups = nb * ns

    @pl.when(ci == 0)
    def _():
        for b in range(nb):
            s_ref[b * A_HEADS:(b + 1) * A_HEADS] = s0_ref[b]
        prev_ref[...] = ps_ref[:, :, MEM_WIDTH:]

    p = p_ref[:, MEM_WIDTH:].astype(F32)
    rows = lax.broadcasted_iota(jnp.int32, (groups * c, 1), 0)
    p_prev = pltpu.roll(p, 1, 0)
    for b in range(nb):
        p_prev = jnp.where(rows == b * ns * c, prev_ref[b], p_prev)
        prev_ref[b] = p[(b + 1) * ns * c - 1:(b + 1) * ns * c, :]
    xm = p + mu_ref[...] * (p_prev - p)

    r = xm[:, 0:tw]
    k = xm[:, tw:2 * tw]
    v = xm[:, 2 * tw:3 * tw]
    xw = xm[:, lora0:lora0 + 128]
    xa = xm[:, lora0:lora0 + 256]
    xg = xm[:, lora0 + 128:lora0 + 512]
    w0, a0, k_k, k_a, r_k, ln_w, ln_b = (vec_ref[i:i + 1, :] for i in range(7))

    z = -(w0 + _dot(_bf(jnp.tanh(xw)), w2_ref[...]))
    softplus = jnp.maximum(z, 0.0) + jnp.log(1.0 + jnp.exp(-jnp.abs(z)))
    ell = -jnp.exp(-softplus - 0.5)
    a = _sigmoid(a0 + _dot(_bf(xa), a2_ref[...]))
    gate = _dot(_bf(_sigmoid(xg)), g2_ref[...])
    kkraw = k * k_k
    k2 = k * (1.0 + (a - 1.0) * k_a)
    e = e_ref[...]
    kap = kkraw * lax.rsqrt(jnp.maximum(_seg_sum(kkraw * kkraw, e, True), 1e-24))
    bet = kap * a

    ell_hi, ell_lo = _split(ell)
    gc = _dot(lt_ref[...], ell_hi) + _dot(lt_ref[...], ell_lo)
    glast = _bcast_rows(gc, c - 1, groups, c)
    egi = jnp.exp(-gc)
    el = jnp.exp(glast - gc)
    xk = kap * jnp.exp(gc - ell)
    xr = r * jnp.exp(gc)
    kb = k2 * egi
    bb = bet * egi
    kh = k2 * el
    bh = bet * el
    gam = jnp.exp(glast)
    ti = lax.broadcasted_iota(jnp.int32, (c, c), 0)
    si = lax.broadcasted_iota(jnp.int32, (c, c), 1)
    strict = si < ti
    incl = si <= ti

    n_chains = nb * A_HEADS
    state = {i: s_ref[i] for i in range(n_chains)}
    y_h = {}
    for s in range(ns):
        ch = [(b * ns + s, h) for b in range(nb) for h in range(A_HEADS)]
        ys, state = _rwkv_chains(range(n_chains), ch, c, state, xk, xr, kb, bb, v, kh, bh, gam, strict, incl)
        for i in range(n_chains):
            y_h[ch[i]] = ys[i]
    for (grp, h), val in y_h.items():
        y_ref[grp * c:(grp + 1) * c, h * A_HEAD_DIM:(h + 1) * A_HEAD_DIM] = val
    for i in range(n_chains):
        s_ref[i] = state[i]

    y = y_ref[...]
    inv_n = 1.0 / A_HEAD_DIM
    mean = _seg_sum(y, e, False) * inv_n
    d = y - mean
    var = _seg_sum(d * d, e, False) * inv_n
    yn = d * lax.rsqrt(var + GN_EPS) * ln_w + ln_b
    bonus = _seg_sum(r * k2 * r_k, e, False) * v
    tok_ref[...] = ((yn + bonus) * gate).astype(tok_ref.dtype)

    @pl.when(ci == pl.num_programs(1) - 1)
    def _():
        for b in range(nb):
            sout_ref[b] = s_ref[b * A_HEADS:(b + 1) * A_HEADS]


def _block_ltri(nb, c):
    t = jnp.arange(nb * c)
    return ((t[None, :] <= t[:, None]) & (t[None, :] // c == t[:, None] // c)).astype(BF16)


def rwkv_mix(p, row0, pshift, s0, mu, vecs, w2, a2, g2, n_batch, seq, c, nb, ns):
    assert (nb == 1 and seq % (ns * c) == 0) or (ns == 1 and seq == c and c == 8)
    n_chunks = seq // (ns * c)
    rows = nb * ns * c
    t0 = row0 // rows
    lane = jnp.arange(LANES)
    e = (lane[:, None] // A_HEAD_DIM == lane[None, :] // A_HEAD_DIM).astype(BF16)
    const = lambda shape: pl.BlockSpec(shape, lambda b, ci: (0,) * len(shape))
    st_spec = pl.BlockSpec((nb, A_HEADS, A_HEAD_DIM, A_HEAD_DIM), lambda b, ci: (b, 0, 0, 0))
    return pl.pallas_call(
        functools.partial(_rwkv_kernel, c=c, nb=nb, ns=ns),
        grid=(n_batch // nb, n_chunks),
        in_specs=[pl.BlockSpec((rows, A_PROJ_COLS), lambda b, ci: (t0 + b * n_chunks + ci, 0)),
                  pl.BlockSpec((nb, 1, A_PROJ_COLS), lambda b, ci: (b, 0, 0)),
                  st_spec,
                  const((1, A_TOK_COLS)), const((8, TOK_WIDTH)),
                  const(w2.shape), const(a2.shape), const(g2.shape), const((LANES, LANES)), const((rows, rows))],
        out_specs=[pl.BlockSpec((rows, TOK_WIDTH), lambda b, ci: (b * n_chunks + ci, 0)), st_spec],
        out_shape=[jax.ShapeDtypeStruct((n_batch * seq, TOK_WIDTH), BF16),
                   jax.ShapeDtypeStruct((n_batch, A_HEADS, A_HEAD_DIM, A_HEAD_DIM), F32)],
        scratch_shapes=[pltpu.VMEM((nb, 1, A_TOK_COLS), F32),
                        pltpu.VMEM((nb * A_HEADS, A_HEAD_DIM, A_HEAD_DIM), F32),
                        pltpu.VMEM((rows, TOK_WIDTH), F32)],
        compiler_params=_cparams("parallel", "arbitrary"),
        name="rwkv_mix",
    )(p, pshift, s0, mu, vecs, w2, a2, g2, e, _block_ltri(nb * ns, c))


def _hgrn_kernel(p_ref, s0_ref, lbp_ref, gn_ref, lt_ref, tok_ref, sout_ref, st_ref, *, c, nb, ns, layer, col0,
                 single_chunk):
    ci = pl.program_id(1)
    tw = TOK_WIDTH
    hd = B_HEAD_DIM
    ch = [(b, h) for b in range(nb) for h in range(B_HEADS)]
    n = range(len(ch))

    if not single_chunk:
        @pl.when(ci == 0)
        def _():
            for i in n:
                st_ref[i] = s0_ref[ch[i][0], ch[i][1]].T

    lbp = lbp_ref[...]
    mx = jnp.max(lbp, axis=0, keepdims=True)
    ex = jnp.exp(lbp - mx)
    den = jnp.sum(ex, axis=0, keepdims=True)
    lb = jnp.zeros((1, tw), F32)
    for i in range(1, layer + 1):
        lb = lb + ex[i:i + 1, :] / den

    q = p_ref[:, col0:col0 + tw].astype(F32)
    f = p_ref[:, col0 + tw:col0 + 2 * tw].astype(F32)
    iv = p_ref[:, col0 + 2 * tw:col0 + 3 * tw].astype(F32)
    og = p_ref[:, col0 + 3 * tw:col0 + 4 * tw].astype(F32)
    fg = lb + (1.0 - lb) * _sigmoid(f)
    lf = jnp.log(fg)
    kk = 1.0 - fg
    qq = q * _sigmoid(q)
    lf_hi, lf_lo = _split(lf)
    bc = _dot(lt_ref[...], lf_hi) + _dot(lt_ref[...], lf_lo)
    blast = _bcast_rows(bc, c - 1, nb * ns, c)
    mid = _bcast_rows(bc, (c - 1) // 2, nb * ns, c)
    qe = qq * jnp.exp(bc - mid)
    ke = kk * jnp.exp(mid - bc)
    qs = qq * jnp.exp(bc)
    kl = kk * jnp.exp(blast - bc)
    gam = jnp.exp(blast)
    gate = og * _sigmoid(og)
    ti = lax.broadcasted_iota(jnp.int32, (c, c), 0)
    si = lax.broadcasted_iota(jnp.int32, (c, c), 1)
    incl = si <= ti
    gn = gn_ref[...]
    ls = [slice(h * hd, (h + 1) * hd) for _, h in ch]
    st = [s0_ref[ch[i][0], ch[i][1]] if single_chunk else st_ref[i] for i in n]
    out = []
    for s in range(ns):
        grp = [ch[i][0] * ns + s for i in n]
        rs = [slice(g * c, (g + 1) * c) for g in grp]
        v_h = [_bf(iv[rs[i], ls[i]]) for i in n]
        att = [jnp.where(incl, _dot_nt(_bf(qe[rs[i], ls[i]]), _bf(ke[rs[i], ls[i]])), 0.0) for i in n]
        if single_chunk:
            o = [_dot(_bf(att[i]), v_h[i]) + _dot(_bf(qs[rs[i], ls[i]]), _bf(st[i])) for i in n]
            ones = jnp.ones((c, hd), BF16)
            lf_parts = [_split(lf[rs[i], ls[i]]) for i in n]
            gam_col = [jnp.exp(_dot_tn(hi, ones) + _dot_tn(lo, ones)) for hi, lo in lf_parts]
            st = [gam_col[i] * st[i] + _dot_tn(_bf(kl[rs[i], ls[i]]), v_h[i]) for i in n]
        else:
            o = [_dot(_bf(att[i]), v_h[i]) + _dot_nt(_bf(qs[rs[i], ls[i]]), _bf(st[i])) for i in n]
            st = [st[i] * gam[grp[i] * c:grp[i] * c + 1, ls[i]] + _dot_tn(v_h[i], _bf(kl[rs[i], ls[i]]))
                  for i in n]
        on = [x * lax.rsqrt(jnp.mean(x * x, axis=-1, keepdims=True) + RMS_EPS) * gn for x in o]
        out += [(rs[i], ls[i], on[i]) for i in n]
    for r_sl, l_sl, val in out:
        tok_ref[r_sl, l_sl] = (val * gate[r_sl, l_sl]).astype(tok_ref.dtype)
    for i in n:
        if single_chunk:
            sout_ref[ch[i][0], ch[i][1]] = st[i]
        else:
            st_ref[i] = st[i]

    if not single_chunk:
        @pl.when(ci == pl.num_programs(1) - 1)
        def _():
            for i in n:
                sout_ref[ch[i][0], ch[i][1]] = st_ref[i].T


def hgrn_mix(p, row0, col0, s0, lbp, gn, n_batch, seq, c, nb, ns, layer):
    assert (nb == 1 and seq % (ns * c) == 0) or (ns == 1 and seq == c and c == 8)
    cols = p.shape[1]
    n_chunks = seq // (ns * c)
    rows = nb * ns * c
    t0 = row0 // rows
    const = lambda shape: pl.BlockSpec(shape, lambda b, ci: (0,) * len(shape))
    st_spec = pl.BlockSpec((nb, B_HEADS, B_HEAD_DIM, B_HEAD_DIM), lambda b, ci: (b, 0, 0, 0))
    return pl.pallas_call(
        functools.partial(_hgrn_kernel, c=c, nb=nb, ns=ns, layer=layer, col0=col0, single_chunk=seq == c),
        grid=(n_batch // nb, n_chunks),
        in_specs=[pl.BlockSpec((rows, cols), lambda b, ci: (t0 + b * n_chunks + ci, 0)),
                  st_spec, const((DEPTH, TOK_WIDTH)), const((1, B_HEAD_DIM)), const((rows, rows))],
        out_specs=[pl.BlockSpec((rows, TOK_WIDTH), lambda b, ci: (b * n_chunks + ci, 0)), st_spec],
        out_shape=[jax.ShapeDtypeStruct((n_batch * seq, TOK_WIDTH), BF16),
                   jax.ShapeDtypeStruct((n_batch, B_HEADS, B_HEAD_DIM, B_HEAD_DIM), F32)],
        scratch_shapes=[pltpu.VMEM((nb * B_HEADS, B_HEAD_DIM, B_HEAD_DIM), F32)],
        compiler_params=_cparams("parallel", "arbitrary"),
        name="hgrn_mix",
    )(p, s0, lbp, gn.reshape(1, B_HEAD_DIM), _block_ltri(nb * ns, c))


PROMPT_CHUNK = 64
A_CHUNKS_PER_STEP = 2
B_CHUNKS_PER_STEP = 4
SAMPLE_SEQS_PER_STEP = 8
TM_DENSE = 1024
TM_IN_PROJ = 512


def kernel(x_prompt, x_sample, mem_prompt, cache_mem_k, cache_mem_v, state_rwkv, state_shift, state_hgrn, state_conv, norm_mix, norm_ffn, norm_final, mem_norm, w_mem_kv, a_w_in, a_mu, a_w0, a_w2, a_a0, a_a2, a_g2, a_k_k, a_k_a, a_r_k, a_ln_w, a_ln_b, a_w_out, b_w_in, b_lower_bounds, b_g_norm, b_w_out, ffn_w_up, ffn_conv_w, ffn_conv_b, ffn_w_down):
    bp, sp, d = x_prompt.shape
    bs, ss, _ = x_sample.shape
    mp, ms = bp * sp, bs * ss
    nbs = SAMPLE_SEQS_PER_STEP

    pad_cols = A_PROJ_COLS - a_w_in.shape[2]
    wa = _bf(jnp.pad(a_w_in[0], ((0, 0), (0, pad_cols))))
    mu = jnp.pad(a_mu[0], (0, pad_cols)).reshape(1, A_TOK_COLS)
    vecs = jnp.stack([a_w0[0], a_a0[0], a_k_k[0], a_k_a[0], a_r_k[0], a_ln_w[0], a_ln_b[0],
                      jnp.zeros((TOK_WIDTH,), F32)])
    w2 = _bf(jnp.pad(a_w2[0], ((0, 128 - A_LORA), (0, 0))))
    a2 = _bf(jnp.pad(a_a2[0], ((A_LORA, 256 - 2 * A_LORA), (0, 0))))
    g2 = _bf(jnp.pad(a_g2[0], ((2 * A_LORA - 128, 384 - (2 * A_LORA - 128) - A_GATE_RANK), (0, 0))))
    wb = _bf(b_w_in[0])
    w_out = [_bf(a_w_out[0]), _bf(b_w_out[0])]
    w_up = _bf(ffn_w_up)
    w_down = _bf(ffn_w_down)

    mem_rows = mem_prompt.reshape(bp * N_MEM, d)
    mem_kv = [norm_matmul([mem_rows], mem_norm[l], _bf(w_mem_kv[l]), bp * N_MEM, 512) for l in range(DEPTH)]
    mem_k_prompt = jnp.stack([kv[:, :MEM_WIDTH] for kv in mem_kv]).reshape(DEPTH, bp, N_MEM, MEM_HEADS, MEM_HEAD_DIM)
    mem_v_prompt = jnp.stack([kv[:, MEM_WIDTH:] for kv in mem_kv]).reshape(DEPTH, bp, N_MEM, MEM_HEADS, MEM_HEAD_DIM)

    hist = [state_conv[l].reshape(bs, 2 * D_FF) for l in range(DEPTH)]

    def ffn(x, layer):
        u_p, cs_p = ffn_up_prompt(x, norm_ffn[layer], w_up, layer, ffn_conv_w[layer], ffn_conv_b[layer],
                                  bp, sp, 1024, 512)
        u_s, c6, c7 = ffn_up_sample(x, mp, norm_ffn[layer], w_up, layer, ffn_conv_w[layer], ffn_conv_b[layer],
                                    hist[layer], ss, 1024, 512)
        x_p = matmul([[u_p]], [(w_down, layer)], [x], D_MODEL, TM_DENSE, 512)
        x_s = matmul([[u_s]], [(w_down, layer)], [x], D_MODEL, TM_DENSE, 512, res_row0=mp)
        return [x_p, x_s], cs_p, jnp.stack([c6, c7], axis=1)

    def out_proj(tok, mem_o, w, res_segs):
        return matmul([tok, mem_o], [w[:TOK_WIDTH], w[TOK_WIDTH:]], res_segs, D_MODEL, TM_DENSE, 1024,
                      cols_outer=True)

    x_segs = [x_prompt.reshape(mp, d), x_sample.reshape(ms, d)]
    p = norm_matmul(x_segs, norm_mix[0], wa, TM_IN_PROJ, A_PROJ_COLS // 2, BF16)
    pshift = matmul([[state_shift[0]]], [wa], None, A_PROJ_COLS, bs, 512).reshape(bs, 1, A_PROJ_COLS)
    tok_p, rwkv_p = rwkv_mix(p, 0, jnp.zeros((bp, 1, A_PROJ_COLS), F32),
                             jnp.zeros((bp, A_HEADS, A_HEAD_DIM, A_HEAD_DIM), F32),
                             mu, vecs, w2, a2, g2, bp, sp, PROMPT_CHUNK, 1, A_CHUNKS_PER_STEP)
    tok_s, rwkv_s = rwkv_mix(p, mp, pshift, state_rwkv[0], mu, vecs, w2, a2, g2, bs, ss, ss, nbs, 1)
    mo_p = mem_attention(p, 0, 0, mem_kv[0], bp, sp, 512)
    mo_s = mem_attention_cache(p, mp, 0, cache_mem_k, cache_mem_v, 0, ss, 8)
    x = out_proj([tok_p, tok_s], [mo_p, mo_s], w_out[0], x_segs)
    x_segs, conv_p0, conv_s0 = ffn(x, 0)
    p = norm_matmul(x_segs, norm_mix[1], wb, TM_IN_PROJ, wb.shape[1] // 2, BF16)
    tok_p, hgrn_p = hgrn_mix(p, 0, MEM_WIDTH, jnp.zeros((bp, B_HEADS, B_HEAD_DIM, B_HEAD_DIM), F32),
                             b_lower_bounds, b_g_norm[0], bp, sp, PROMPT_CHUNK, 1, B_CHUNKS_PER_STEP, 1)
    tok_s, hgrn_s = hgrn_mix(p, mp, MEM_WIDTH, state_hgrn[0], b_lower_bounds, b_g_norm[0], bs, ss, ss, nbs, 1, 1)
    mo_p = mem_attention(p, 0, 0, mem_kv[1], bp, sp, 512)
    mo_s = mem_attention_cache(p, mp, 0, cache_mem_k, cache_mem_v, 1, ss, 8)
    x = out_proj([tok_p, tok_s], [mo_p, mo_s], w_out[1], x_segs)
    x_segs, conv_p1, conv_s1 = ffn(x, 1)
    y_p = rmsnorm(x_segs[0], norm_final, 256)
    y_s = rmsnorm(x_segs[1], norm_final, 256)
    shift_p = rmsnorm(x_prompt[:, -1], norm_mix[0], bp)
    shift_s = rmsnorm(x_sample[:, -1], norm_mix[0], bs)

    return (y_p.reshape(bp, sp, d), y_s.reshape(bs, ss, d), mem_k_prompt, mem_v_prompt,
            rwkv_p[None], rwkv_s[None], shift_p[None], shift_s[None], hgrn_p[None], hgrn_s[None],
            jnp.stack([conv_p0, conv_p1]), jnp.stack([conv_s0, conv_s1]))
```

```python
import functools

import jax
import jax.numpy as jnp
from jax import lax
from jax.experimental import pallas as pl
from jax.experimental.pallas import tpu as pltpu

F32 = jnp.float32
BF16 = jnp.bfloat16

D_MODEL = 2048
DEPTH = 2
TOK_WIDTH = 1536
MEM_WIDTH = 512
MEM_HEADS = 4
MEM_HEAD_DIM = 128
N_MEM = 256
A_HEAD_DIM = 64
A_HEADS = 24
A_LORA = 96
A_GATE_RANK = 256
A_TOK_COLS = 5120
A_PROJ_COLS = MEM_WIDTH + A_TOK_COLS
B_HEADS = 12
B_HEAD_DIM = 128
D_FF = 5632
RMS_EPS = 1e-6
GN_EPS = 64e-5

LANES = 128
VMEM_LIMIT = 56 * 1024 * 1024


def _cparams(*sem):
    return pltpu.CompilerParams(dimension_semantics=sem, vmem_limit_bytes=VMEM_LIMIT)


def _dot(a, b):
    return jnp.dot(a, b, preferred_element_type=F32)


def _dot_nt(a, b):
    return lax.dot_general(a, b, (((1,), (1,)), ((), ())), preferred_element_type=F32)


def _dot_tn(a, b):
    return lax.dot_general(a, b, (((0,), (0,)), ((), ())), preferred_element_type=F32)


def _bf(x):
    return x.astype(BF16)


def _split(x):
    hi = x.astype(BF16)
    lo = (x - hi.astype(F32)).astype(BF16)
    return hi, lo


def _sigmoid(x):
    return 0.5 * (jnp.tanh(0.5 * x) + 1.0)


def _rms(x, g):
    ms = jnp.mean(x * x, axis=-1, keepdims=True)
    return x * lax.rsqrt(ms + RMS_EPS) * g


def _seg_bounds(segs, tm):
    bounds, off = [], 0
    for a in segs:
        bounds.append(off)
        off += a.shape[0] // tm
    return tuple(bounds), off


def _seg_row_specs(segs, tm, width, col_of_j, tile0=0):
    assert tile0 == 0 or len(segs) == 1
    specs, off = [], 0
    for a in segs:
        n = a.shape[0] // tm
        if col_of_j:
            imap = lambda i, j, off=off, n=n: (jnp.clip(i - off + tile0, 0, n - 1), j)
        else:
            imap = lambda i, j, off=off, n=n: (jnp.clip(i - off + tile0, 0, n - 1), 0)
        specs.append(pl.BlockSpec((tm, width), imap))
        off += n
    return specs


def _for_segment(i, bounds, fn):
    if len(bounds) == 1:
        fn(0)
        return
    for s in range(len(bounds)):
        cond = i >= bounds[s]
        if s + 1 < len(bounds):
            cond = jnp.logical_and(cond, i < bounds[s + 1])
        pl.when(cond)(functools.partial(fn, s))


def _w_spec(w, tn, col_block0=0):
    if isinstance(w, tuple):
        arr, layer = w
        return arr, pl.BlockSpec((None, arr.shape[1], tn), lambda i, j: (layer, 0, col_block0 + j))
    return w, pl.BlockSpec((w.shape[0], tn), lambda i, j: (0, col_block0 + j))


def _cast_kernel(x_ref, o_ref):
    o_ref[...] = x_ref[...].astype(o_ref.dtype)


def cast_bf16(w, tm):
    n_l, m, n = w.shape
    spec = pl.BlockSpec((1, tm, n), lambda l, i: (l, i, 0))
    return pl.pallas_call(
        _cast_kernel,
        grid=(n_l, m // tm),
        in_specs=[spec],
        out_specs=spec,
        out_shape=jax.ShapeDtypeStruct(w.shape, BF16),
        compiler_params=_cparams("parallel", "parallel"),
        name="cast_bf16",
    )(w)


def _rmsnorm_kernel(x_ref, g_ref, o_ref):
    o_ref[...] = _rms(x_ref[...], g_ref[...])


def rmsnorm(x, g, tm, row0=0, rows=None):
    d = x.shape[1]
    rows = x.shape[0] if rows is None else rows
    t0 = row0 // tm
    return pl.pallas_call(
        _rmsnorm_kernel,
        grid=(rows // tm,),
        in_specs=[pl.BlockSpec((tm, d), lambda i: (t0 + i, 0)), pl.BlockSpec((1, d), lambda i: (0, 0))],
        out_specs=pl.BlockSpec((tm, d), lambda i: (i, 0)),
        out_shape=jax.ShapeDtypeStruct((rows, d), F32),
        compiler_params=_cparams("parallel"),
        name="rmsnorm",
    )(x, g.reshape(1, d))


NORM_ROWS = 256


def _norm_to_scratch(x_ref, g, hb_ref, tm, dst_off=0):
    for r in range(0, tm, NORM_ROWS):
        n = min(NORM_ROWS, tm - r)
        hb_ref[dst_off + r:dst_off + r + n, :] = _bf(_rms(x_ref[r:r + n, :], g))


def _norm_matmul_kernel(*refs, n_seg, bounds, tm):
    x_refs = refs[:n_seg]
    g_ref, w_ref, o_ref, hb_ref = refs[n_seg:]

    @pl.when(pl.program_id(1) == 0)
    def _():
        g = g_ref[...]
        _for_segment(pl.program_id(0), bounds, lambda s: _norm_to_scratch(x_refs[s], g, hb_ref, tm))

    o_ref[...] = _dot(hb_ref[...], w_ref[...]).astype(o_ref.dtype)


def norm_matmul(x_segs, g, w, tm, tn, out_dtype=F32):
    d = x_segs[0].shape[1]
    bounds, n_tiles = _seg_bounds(x_segs, tm)
    w_arr, w_spec = _w_spec(w, tn)
    n = w_arr.shape[-1]
    return pl.pallas_call(
        functools.partial(_norm_matmul_kernel, n_seg=len(x_segs), bounds=bounds, tm=tm),
        grid=(n_tiles, n // tn),
        in_specs=_seg_row_specs(x_segs, tm, d, False) + [pl.BlockSpec((1, d), lambda i, j: (0, 0)), w_spec],
        out_specs=pl.BlockSpec((tm, tn), lambda i, j: (i, j)),
        out_shape=jax.ShapeDtypeStruct((n_tiles * tm, n), out_dtype),
        scratch_shapes=[pltpu.VMEM((tm, d), BF16)],
        compiler_params=_cparams("parallel", "arbitrary"),
        name="norm_matmul",
    )(*x_segs, g.reshape(1, d), w_arr)


def _mm_kernel(*refs, a_counts, res_count, bounds, row_axis):
    pos = 0
    a_refs = []
    for cnt in a_counts:
        a_refs.append(refs[pos:pos + cnt])
        pos += cnt
    w_refs = refs[pos:pos + len(a_counts)]
    pos += len(a_counts)
    res_refs = refs[pos:pos + res_count]
    o_ref = refs[-1]

    def compute(s):
        acc = None
        for k, segs in enumerate(a_refs):
            term = _dot(_bf(segs[min(s, len(segs) - 1)][...]), w_refs[k][...])
            acc = term if acc is None else acc + term
        if res_count:
            acc = res_refs[min(s, res_count - 1)][...] + acc
        o_ref[...] = acc

    _for_segment(pl.program_id(row_axis), bounds, compute)


def matmul(a_list, w_list, res_segs, n_out, tm, tn, res_row0=0, cols_outer=False):
    longest = max(a_list + ([res_segs] if res_segs else []), key=len)
    bounds, n_tiles = _seg_bounds(longest, tm)
    in_specs, args = [], []
    for segs in a_list:
        assert len(segs) in (1, len(bounds))
        in_specs += _seg_row_specs(segs, tm, segs[0].shape[1], False)
        args += list(segs)
    for w in w_list:
        w_arr, w_spec = _w_spec(w, tn)
        in_specs.append(w_spec)
        args.append(w_arr)
    if res_segs:
        assert len(res_segs) in (1, len(bounds))
        in_specs += _seg_row_specs(res_segs, tm, tn, True, res_row0 // tm)
        args += list(res_segs)
    out_spec = pl.BlockSpec((tm, tn), lambda i, j: (i, j))
    grid = (n_tiles, n_out // tn)
    if cols_outer:
        swap = lambda spec: pl.BlockSpec(spec.block_shape, lambda j, i, f=spec.index_map: f(i, j))
        in_specs, out_spec, grid = [swap(sp) for sp in in_specs], swap(out_spec), grid[::-1]
    return pl.pallas_call(
        functools.partial(_mm_kernel, a_counts=tuple(len(s) for s in a_list),
                          res_count=len(res_segs) if res_segs else 0, bounds=bounds,
                          row_axis=1 if cols_outer else 0),
        grid=grid,
        in_specs=in_specs,
        out_specs=out_spec,
        out_shape=jax.ShapeDtypeStruct((n_tiles * tm, n_out), F32),
        compiler_params=_cparams("parallel", "parallel"),
        name="matmul",
    )(*args)


def _attn_kernel(q_ref, k_ref, v_ref, o_ref, *, n_seq, tq, cache_layout):
    scale = MEM_HEAD_DIM ** -0.5
    pairs = [(s, h) for s in range(n_seq) for h in range(MEM_HEADS)]
    hsl = [slice(h * MEM_HEAD_DIM, (h + 1) * MEM_HEAD_DIM) for h in range(MEM_HEADS)]
    q = [_bf(q_ref[s * tq:(s + 1) * tq, hsl[h]]) for s, h in pairs]
    if cache_layout:
        k = [_bf(k_ref[0, s, pl.ds(h, N_MEM, stride=MEM_HEADS), :]) for s, h in pairs]
        v = [_bf(v_ref[0, s, pl.ds(h, N_MEM, stride=MEM_HEADS), :]) for s, h in pairs]
    else:
        k = [_bf(k_ref[s * N_MEM:(s + 1) * N_MEM, hsl[h]]) for s, h in pairs]
        v = [_bf(v_ref[s * N_MEM:(s + 1) * N_MEM, hsl[h]]) for s, h in pairs]
    sc = [_dot_nt(q[i], k[i]) * scale for i in range(len(pairs))]
    e = [jnp.exp(x - jnp.max(x, axis=-1, keepdims=True)) for x in sc]
    p = [_bf(x / jnp.sum(x, axis=-1, keepdims=True)) for x in e]
    o = [_dot(p[i], v[i]) for i in range(len(pairs))]
    rows = [jnp.concatenate(o[s * MEM_HEADS:(s + 1) * MEM_HEADS], axis=1) for s in range(n_seq)]
    o_ref[...] = _bf(jnp.concatenate(rows, axis=0) if n_seq > 1 else rows[0])


def mem_attention(p, row0, q_colblock, kv, n_batch, seq, tq):
    q_tiles = seq // tq
    t0 = row0 // tq
    return pl.pallas_call(
        functools.partial(_attn_kernel, n_seq=1, tq=tq, cache_layout=False),
        grid=(n_batch, q_tiles),
        in_specs=[pl.BlockSpec((tq, MEM_WIDTH), lambda b, t: (t0 + b * q_tiles + t, q_colblock)),
                  pl.BlockSpec((N_MEM, MEM_WIDTH), lambda b, t: (b, 0)),
                  pl.BlockSpec((N_MEM, MEM_WIDTH), lambda b, t: (b, 1))],
        out_specs=pl.BlockSpec((tq, MEM_WIDTH), lambda b, t: (b * q_tiles + t, 0)),
        out_shape=jax.ShapeDtypeStruct((n_batch * seq, MEM_WIDTH), BF16),
        compiler_params=_cparams("parallel", "parallel"),
        name="mem_attention",
    )(p, kv, kv)


def mem_attention_cache(p, row0, q_colblock, cache_k, cache_v, layer, seq, n_seq):
    depth, n_batch = cache_k.shape[:2]
    cache_k = cache_k.reshape(depth, n_batch, N_MEM * MEM_HEADS, MEM_HEAD_DIM)
    cache_v = cache_v.reshape(depth, n_batch, N_MEM * MEM_HEADS, MEM_HEAD_DIM)
    rows = n_seq * seq
    t0 = row0 // rows
    kv_spec = pl.BlockSpec((1, n_seq, N_MEM * MEM_HEADS, MEM_HEAD_DIM), lambda b: (layer, b, 0, 0))
    return pl.pallas_call(
        functools.partial(_attn_kernel, n_seq=n_seq, tq=seq, cache_layout=True),
        grid=(n_batch // n_seq,),
        in_specs=[pl.BlockSpec((rows, MEM_WIDTH), lambda b: (t0 + b, q_colblock)), kv_spec, kv_spec],
        out_specs=pl.BlockSpec((rows, MEM_WIDTH), lambda b: (b, 0)),
        out_shape=jax.ShapeDtypeStruct((n_batch * seq, MEM_WIDTH), BF16),
        compiler_params=_cparams("parallel"),
        name="mem_attention_cache",
    )(p, cache_k, cache_v)


FFN_HALO = 16


def _gelu_gate(c, v):
    return _bf(jax.nn.gelu(c) * v)


def _ffn_up_prompt_kernel(x_ref, xh_ref, g_ref, wa_ref, wv_ref, cw_ref, cb_ref, u_ref, cs_ref, hb_ref,
                          *, tm, tiles_per_seq):
    i = pl.program_id(0)

    @pl.when(pl.program_id(1) == 0)
    def _():
        g = g_ref[...]
        hb_ref[0:FFN_HALO, :] = _bf(_rms(xh_ref[...], g))
        _norm_to_scratch(x_ref, g, hb_ref, tm, dst_off=FFN_HALO)

    a_ext = _dot(hb_ref[...], wa_ref[...])
    v = _dot(hb_ref[FFN_HALO:, :], wv_ref[...])
    rows = lax.broadcasted_iota(jnp.int32, (FFN_HALO + tm, 1), 0)
    n_zero = jnp.where((i % tiles_per_seq) == 0, FFN_HALO, 0)
    a_ext = jnp.where(rows < n_zero, 0.0, a_ext)
    a0 = a_ext[FFN_HALO:]
    a1 = pltpu.roll(a_ext, 1, 0)[FFN_HALO:]
    a2 = pltpu.roll(a_ext, 2, 0)[FFN_HALO:]
    c = cb_ref[...] + a2 * cw_ref[0:1, :] + a1 * cw_ref[1:2, :] + a0 * cw_ref[2:3, :]
    u_ref[...] = _gelu_gate(c, v)
    cs_ref[0] = a0[tm - 8:tm][6:8]


def ffn_up_prompt(x, g, w_up, layer, cw, cb, n_batch, seq, tm, tn):
    d = x.shape[1]
    m = n_batch * seq
    nf = D_FF // tn
    tiles_per_seq = seq // tm
    halo_blocks = tm // FFN_HALO
    w_arr, wa_spec = _w_spec((w_up, layer), tn)
    _, wv_spec = _w_spec((w_up, layer), tn, nf)
    u, cs = pl.pallas_call(
        functools.partial(_ffn_up_prompt_kernel, tm=tm, tiles_per_seq=tiles_per_seq),
        grid=(m // tm, nf),
        in_specs=[pl.BlockSpec((tm, d), lambda i, j: (i, 0)),
                  pl.BlockSpec((FFN_HALO, d), lambda i, j: (jnp.maximum(i * halo_blocks - 1, 0), 0)),
                  pl.BlockSpec((1, d), lambda i, j: (0, 0)),
                  wa_spec, wv_spec,
                  pl.BlockSpec((3, tn), lambda i, j: (0, j)),
                  pl.BlockSpec((1, tn), lambda i, j: (0, j))],
        out_specs=[pl.BlockSpec((tm, tn), lambda i, j: (i, j)),
                   pl.BlockSpec((1, 2, tn), lambda i, j: (i, 0, j))],
        out_shape=[jax.ShapeDtypeStruct((m, D_FF), BF16),
                   jax.ShapeDtypeStruct((m // tm, 2, D_FF), F32)],
        scratch_shapes=[pltpu.VMEM((FFN_HALO + tm, d), BF16)],
        compiler_params=_cparams("parallel", "arbitrary"),
        name="ffn_up_prompt",
    )(x, x, g.reshape(1, d), w_arr, w_arr, cw, cb.reshape(1, D_FF))
    return u, cs.reshape(n_batch, tiles_per_seq, 2, D_FF)[:, -1]


def _ffn_up_sample_kernel(x_ref, g_ref, wa_ref, wv_ref, cw_ref, cb_ref, h0_ref, h1_ref,
                          u_ref, c6_ref, c7_ref, hb_ref, a_ref, a1_ref, a2_ref, *, tm, seq):
    @pl.when(pl.program_id(1) == 0)
    def _():
        _norm_to_scratch(x_ref, g_ref[...], hb_ref, tm)

    a = _dot(hb_ref[...], wa_ref[...])
    v = _dot(hb_ref[...], wv_ref[...])
    r1 = pltpu.roll(a, 1, 0)
    r2 = pltpu.roll(a, 2, 0)
    nseq = tm // seq
    firsts = pl.ds(0, nseq, stride=seq)
    seconds = pl.ds(1, nseq, stride=seq)
    for q in range(a.shape[1] // LANES):
        sl = slice(q * LANES, (q + 1) * LANES)
        a1_ref[q] = r1[:, sl]
        a1_ref[q, firsts, :] = h1_ref[:, sl]
        a2_ref[q] = r2[:, sl]
        a2_ref[q, firsts, :] = h0_ref[:, sl]
        a2_ref[q, seconds, :] = h1_ref[:, sl]
        a_ref[q] = a[:, sl]
        c6_ref[:, sl] = a_ref[q, pl.ds(seq - 2, nseq, stride=seq), :]
        c7_ref[:, sl] = a_ref[q, pl.ds(seq - 1, nseq, stride=seq), :]
    n_q = a.shape[1] // LANES
    a1 = jnp.concatenate([a1_ref[q] for q in range(n_q)], axis=1)
    a2 = jnp.concatenate([a2_ref[q] for q in range(n_q)], axis=1)
    c = cb_ref[...] + a2 * cw_ref[0:1, :] + a1 * cw_ref[1:2, :] + a * cw_ref[2:3, :]
    u_ref[...] = _gelu_gate(c, v)


def ffn_up_sample(x, row0, g, w_up, layer, cw, cb, hist, seq, tm, tn):
    d = x.shape[1]
    m = hist.shape[0] * seq
    nf = D_FF // tn
    nb = tm // seq
    t0 = row0 // tm
    w_arr, wa_spec = _w_spec((w_up, layer), tn)
    _, wv_spec = _w_spec((w_up, layer), tn, nf)
    rows_scratch = pltpu.VMEM((tn // LANES, tm, LANES), F32)
    return pl.pallas_call(
        functools.partial(_ffn_up_sample_kernel, tm=tm, seq=seq),
        grid=(m // tm, nf),
        in_specs=[pl.BlockSpec((tm, d), lambda i, j: (t0 + i, 0)),
                  pl.BlockSpec((1, d), lambda i, j: (0, 0)),
                  wa_spec, wv_spec,
                  pl.BlockSpec((3, tn), lambda i, j: (0, j)),
                  pl.BlockSpec((1, tn), lambda i, j: (0, j)),
                  pl.BlockSpec((nb, tn), lambda i, j: (i, j)),
                  pl.BlockSpec((nb, tn), lambda i, j: (i, nf + j))],
        out_specs=[pl.BlockSpec((tm, tn), lambda i, j: (i, j)),
                   pl.BlockSpec((nb, tn), lambda i, j: (i, j)),
                   pl.BlockSpec((nb, tn), lambda i, j: (i, j))],
        out_shape=[jax.ShapeDtypeStruct((m, D_FF), BF16),
                   jax.ShapeDtypeStruct((m // seq, D_FF), F32),
                   jax.ShapeDtypeStruct((m // seq, D_FF), F32)],
        scratch_shapes=[pltpu.VMEM((tm, d), BF16), rows_scratch, rows_scratch, rows_scratch],
        compiler_params=_cparams("parallel", "arbitrary"),
        name="ffn_up_sample",
    )(x, g.reshape(1, d), w_arr, w_arr, cw, cb.reshape(1, D_FF), hist, hist)


def _seg_sum(x, e, two_pass):
    parts = []
    for j in range(x.shape[1] // LANES):
        xj = x[:, j * LANES:(j + 1) * LANES]
        if two_pass:
            hi, lo = _split(xj)
            parts.append(_dot(hi, e) + _dot(lo, e))
        else:
            parts.append(_dot(_bf(xj), e))
    return jnp.concatenate(parts, axis=1)


def _bcast_rows(x, idx, nb, c):
    parts = [jnp.broadcast_to(x[b * c + idx:b * c + idx + 1, :], (c, x.shape[1])) for b in range(nb)]
    return parts[0] if nb == 1 else jnp.concatenate(parts, axis=0)


def _unit_lower_solve(a_list, rhs_list, c):
    mm = lambda x, y: _dot(_bf(x), _bf(y))
    n = range(len(a_list))
    rows = lax.broadcasted_iota(jnp.int32, (c, c), 0)
    cols = lax.broadcasted_iota(jnp.int32, (c, c), 1)
    eye = (rows == cols).astype(F32)
    blk = min(c, 16)
    if c > blk:
        assert c // blk <= 4
        same = (rows // blk) == (cols // blk)
        ad = [jnp.where(same, a, 0.0) for a in a_list]
        ao = [a_list[i] - ad[i] for i in n]
    else:
        ad = a_list
    t = [eye - ad[i] for i in n]
    pw = ad
    span = 2
    while span < blk:
        pw = [mm(pw[i], pw[i]) for i in n]
        t = [t[i] + mm(t[i], pw[i]) for i in n]
        span *= 2
    x = [mm(t[i], rhs_list[i]) for i in n]
    if c > blk:
        nn = [mm(t[i], ao[i]) for i in n]
        n2 = [mm(nn[i], nn[i]) for i in n]
        x = [x[i] + mm(n2[i], x[i]) for i in n]
        x = [x[i] - mm(nn[i], x[i]) for i in n]
    return x


def _rwkv_chains(n, ch, c, s_old, xk, xr, kb, bb, v, kh, bh, gam, strict, incl):
    rs = {i: slice(ch[i][0] * c, (ch[i][0] + 1) * c) for i in n}
    ls = {i: slice(ch[i][1] * A_HEAD_DIM, (ch[i][1] + 1) * A_HEAD_DIM) for i in n}
    sb = {i: _bf(s_old[i]) for i in n}
    xk_h = {i: _bf(xk[rs[i], ls[i]]) for i in n}
    xr_h = {i: _bf(xr[rs[i], ls[i]]) for i in n}
    kb_h = {i: _bf(kb[rs[i], ls[i]]) for i in n}
    bb_h = {i: _bf(bb[rs[i], ls[i]]) for i in n}
    v_f = {i: v[rs[i], ls[i]] for i in n}
    v_h = {i: _bf(v_f[i]) for i in n}
    a_kk = {i: jnp.where(strict, _dot_nt(xk_h[i], kb_h[i]), 0.0) for i in n}
    a_kb = {i: jnp.where(strict, _dot_nt(xk_h[i], bb_h[i]), 0.0) for i in n}
    a_rk = {i: jnp.where(incl, _dot_nt(xr_h[i], kb_h[i]), 0.0) for i in n}
    a_rb = {i: jnp.where(incl, _dot_nt(xr_h[i], bb_h[i]), 0.0) for i in n}
    rhs = {i: -(_dot_nt(xk_h[i], sb[i]) + _dot(_bf(a_kk[i]), v_h[i])) for i in n}
    u = dict(zip(n, _unit_lower_solve([a_kb[i] for i in n], [rhs[i] for i in n], c)))
    u_h = {i: _bf(u[i]) for i in n}
    y_h = {i: _dot_nt(xr_h[i], sb[i]) + _dot(_bf(a_rk[i]), v_h[i]) + _dot(_bf(a_rb[i]), u_h[i]) for i in n}
    vu = {i: _bf(jnp.concatenate([v_f[i], u[i]], axis=0)) for i in n}
    kbh = {i: _bf(jnp.concatenate([kh[rs[i], ls[i]], bh[rs[i], ls[i]]], axis=0)) for i in n}
    s_new = {i: s_old[i] * gam[ch[i][0] * c:ch[i][0] * c + 1, ls[i]] + _dot_tn(vu[i], kbh[i]) for i in n}
    return y_h, s_new


def _rwkv_kernel(p_ref, ps_ref, s0_ref, mu_ref, vec_ref, w2_ref, a2_ref, g2_ref, e_ref, lt_ref,
                 tok_ref, sout_ref, prev_ref, s_ref, y_ref, *, c, nb, ns):
    ci = pl.program_id(1)
    tw = TOK_WIDTH
    lora0 = 3 * tw
    groups = nb * ns

    @pl.when(ci == 0)
    def _():
        for b in range(nb):
            s_ref[b * A_HEADS:(b + 1) * A_HEADS] = s0_ref[b]
        prev_ref[...] = ps_ref[:, :, MEM_WIDTH:]

    p = p_ref[:, MEM_WIDTH:].astype(F32)
    rows = lax.broadcasted_iota(jnp.int32, (groups * c, 1), 0)
    p_prev = pltpu.roll(p, 1, 0)
    for b in range(nb):
        p_prev = jnp.where(rows == b * ns * c, prev_ref[b], p_prev)
        prev_ref[b] = p[(b + 1) * ns * c - 1:(b + 1) * ns * c, :]
    xm = p + mu_ref[...] * (p_prev - p)

    r = xm[:, 0:tw]
    k = xm[:, tw:2 * tw]
    v = xm[:, 2 * tw:3 * tw]
    xw = xm[:, lora0:lora0 + 128]
    xa = xm[:, lora0:lora0 + 256]
    xg = xm[:, lora0 + 128:lora0 + 512]
    w0, a0, k_k, k_a, r_k, ln_w, ln_b = (vec_ref[i:i + 1, :] for i in range(7))

    z = -(w0 + _dot(_bf(jnp.tanh(xw)), w2_ref[...]))
    softplus = jnp.maximum(z, 0.0) + jnp.log(1.0 + jnp.exp(-jnp.abs(z)))
    ell = -jnp.exp(-softplus - 0.5)
    a = _sigmoid(a0 + _dot(_bf(xa), a2_ref[...]))
    gate = _dot(_bf(_sigmoid(xg)), g2_ref[...])
    kkraw = k * k_k
    k2 = k * (1.0 + (a - 1.0) * k_a)
    e = e_ref[...]
    kap = kkraw * lax.rsqrt(jnp.maximum(_seg_sum(kkraw * kkraw, e, True), 1e-24))
    bet = kap * a

    ell_hi, ell_lo = _split(ell)
    gc = _dot(lt_ref[...], ell_hi) + _dot(lt_ref[...], ell_lo)
    glast = _bcast_rows(gc, c - 1, groups, c)
    egi = jnp.exp(-gc)
    el = jnp.exp(glast - gc)
    xk = kap * jnp.exp(gc - ell)
    xr = r * jnp.exp(gc)
    kb = k2 * egi
    bb = bet * egi
    kh = k2 * el
    bh = bet * el
    gam = jnp.exp(glast)
    ti = lax.broadcasted_iota(jnp.int32, (c, c), 0)
    si = lax.broadcasted_iota(jnp.int32, (c, c), 1)
    strict = si < ti
    incl = si <= ti

    n_chains = nb * A_HEADS
    state = {i: s_ref[i] for i in range(n_chains)}
    y_h = {}
    for s in range(ns):
        ch = [(b * ns + s, h) for b in range(nb) for h in range(A_HEADS)]
        ys, state = _rwkv_chains(range(n_chains), ch, c, state, xk, xr, kb, bb, v, kh, bh, gam, strict, incl)
        for i in range(n_chains):
            y_h[ch[i]] = ys[i]
    for (grp, h), val in y_h.items():
        y_ref[grp * c:(grp + 1) * c, h * A_HEAD_DIM:(h + 1) * A_HEAD_DIM] = val
    for i in range(n_chains):
        s_ref[i] = state[i]

    y = y_ref[...]
    inv_n = 1.0 / A_HEAD_DIM
    mean = _seg_sum(y, e, False) * inv_n
    d = y - mean
    var = _seg_sum(d * d, e, False) * inv_n
    yn = d * lax.rsqrt(var + GN_EPS) * ln_w + ln_b
    bonus = _seg_sum(r * k2 * r_k, e, False) * v
    tok_ref[...] = ((yn + bonus) * gate).astype(tok_ref.dtype)

    @pl.when(ci == pl.num_programs(1) - 1)
    def _():
        for b in range(nb):
            sout_ref[b] = s_ref[b * A_HEADS:(b + 1) * A_HEADS]


def _block_ltri(nb, c):
    t = jnp.arange(nb * c)
    return ((t[None, :] <= t[:, None]) & (t[None, :] // c == t[:, None] // c)).astype(BF16)


def rwkv_mix(p, row0, pshift, s0, mu, vecs, w2, a2, g2, n_batch, seq, c, nb, ns):
    assert (nb == 1 and seq % (ns * c) == 0) or (ns == 1 and seq == c and c == 8)
    n_chunks = seq // (ns * c)
    rows = nb * ns * c
    t0 = row0 // rows
    lane = jnp.arange(LANES)
    e = (lane[:, None] // A_HEAD_DIM == lane[None, :] // A_HEAD_DIM).astype(BF16)
    const = lambda shape: pl.BlockSpec(shape, lambda b, ci: (0,) * len(shape))
    st_spec = pl.BlockSpec((nb, A_HEADS, A_HEAD_DIM, A_HEAD_DIM), lambda b, ci: (b, 0, 0, 0))
    return pl.pallas_call(
        functools.partial(_rwkv_kernel, c=c, nb=nb, ns=ns),
        grid=(n_batch // nb, n_chunks),
        in_specs=[pl.BlockSpec((rows, A_PROJ_COLS), lambda b, ci: (t0 + b * n_chunks + ci, 0)),
                  pl.BlockSpec((nb, 1, A_PROJ_COLS), lambda b, ci: (b, 0, 0)),
                  st_spec,
                  const((1, A_TOK_COLS)), const((8, TOK_WIDTH)),
                  const(w2.shape), const(a2.shape), const(g2.shape), const((LANES, LANES)), const((rows, rows))],
        out_specs=[pl.BlockSpec((rows, TOK_WIDTH), lambda b, ci: (b * n_chunks + ci, 0)), st_spec],
        out_shape=[jax.ShapeDtypeStruct((n_batch * seq, TOK_WIDTH), BF16),
                   jax.ShapeDtypeStruct((n_batch, A_HEADS, A_HEAD_DIM, A_HEAD_DIM), F32)],
        scratch_shapes=[pltpu.VMEM((nb, 1, A_TOK_COLS), F32),
                        pltpu.VMEM((nb * A_HEADS, A_HEAD_DIM, A_HEAD_DIM), F32),
                        pltpu.VMEM((rows, TOK_WIDTH), F32)],
        compiler_params=_cparams("parallel", "arbitrary"),
        name="rwkv_mix",
    )(p, pshift, s0, mu, vecs, w2, a2, g2, e, _block_ltri(nb * ns, c))


def _hgrn_kernel(p_ref, s0_ref, lbp_ref, gn_ref, lt_ref, tok_ref, sout_ref, st_ref, *, c, nb, ns, layer, col0,
                 single_chunk):
    ci = pl.program_id(1)
    tw = TOK_WIDTH
    hd = B_HEAD_DIM
    ch = [(b, h) for b in range(nb) for h in range(B_HEADS)]
    n = range(len(ch))

    if not single_chunk:
        @pl.when(ci == 0)
        def _():
            for i in n:
                st_ref[i] = s0_ref[ch[i][0], ch[i][1]].T

    lbp = lbp_ref[...]
    mx = jnp.max(lbp, axis=0, keepdims=True)
    ex = jnp.exp(lbp - mx)
    den = jnp.sum(ex, axis=0, keepdims=True)
    lb = jnp.zeros((1, tw), F32)
    for i in range(1, layer + 1):
        lb = lb + ex[i:i + 1, :] / den

    q = p_ref[:, col0:col0 + tw].astype(F32)
    f = p_ref[:, col0 + tw:col0 + 2 * tw].astype(F32)
    iv = p_ref[:, col0 + 2 * tw:col0 + 3 * tw].astype(F32)
    og = p_ref[:, col0 + 3 * tw:col0 + 4 * tw].astype(F32)
    fg = lb + (1.0 - lb) * _sigmoid(f)
    lf = jnp.log(fg)
    kk = 1.0 - fg
    qq = q * _sigmoid(q)
    lf_hi, lf_lo = _split(lf)
    bc = _dot(lt_ref[...], lf_hi) + _dot(lt_ref[...], lf_lo)
    blast = _bcast_rows(bc, c - 1, nb * ns, c)
    mid = _bcast_rows(bc, (c - 1) // 2, nb * ns, c)
    qe = qq * jnp.exp(bc - mid)
    ke = kk * jnp.exp(mid - bc)
    qs = qq * jnp.exp(bc)
    kl = kk * jnp.exp(blast - bc)
    gam = jnp.exp(blast)
    gate = og * _sigmoid(og)
    ti = lax.broadcasted_iota(jnp.int32, (c, c), 0)
    si = lax.broadcasted_iota(jnp.int32, (c, c), 1)
    incl = si <= ti
    gn = gn_ref[...]
    ls = [slice(h * hd, (h + 1) * hd) for _, h in ch]
    st = [s0_ref[ch[i][0], ch[i][1]] if single_chunk else st_ref[i] for i in n]
    out = []
    for s in range(ns):
        grp = [ch[i][0] * ns + s for i in n]
        rs = [slice(g * c, (g + 1) * c) for g in grp]
        v_h = [_bf(iv[rs[i], ls[i]]) for i in n]
        att = [jnp.where(incl, _dot_nt(_bf(qe[rs[i], ls[i]]), _bf(ke[rs[i], ls[i]])), 0.0) for i in n]
        if single_chunk:
            o = [_dot(_bf(att[i]), v_h[i]) + _dot(_bf(qs[rs[i], ls[i]]), _bf(st[i])) for i in n]
            ones = jnp.ones((c, hd), BF16)
            lf_parts = [_split(lf[rs[i], ls[i]]) for i in n]
            gam_col = [jnp.exp(_dot_tn(hi, ones) + _dot_tn(lo, ones)) for hi, lo in lf_parts]
            st = [gam_col[i] * st[i] + _dot_tn(_bf(kl[rs[i], ls[i]]), v_h[i]) for i in n]
        else:
            o = [_dot(_bf(att[i]), v_h[i]) + _dot_nt(_bf(qs[rs[i], ls[i]]), _bf(st[i])) for i in n]
            st = [st[i] * gam[grp[i] * c:grp[i] * c + 1, ls[i]] + _dot_tn(v_h[i], _bf(kl[rs[i], ls[i]]))
                  for i in n]
        on = [x * lax.rsqrt(jnp.mean(x * x, axis=-1, keepdims=True) + RMS_EPS) * gn for x in o]
        out += [(rs[i], ls[i], on[i]) for i in n]
    for r_sl, l_sl, val in out:
        tok_ref[r_sl, l_sl] = (val * gate[r_sl, l_sl]).astype(tok_ref.dtype)
    for i in n:
        if single_chunk:
            sout_ref[ch[i][0], ch[i][1]] = st[i]
        else:
            st_ref[i] = st[i]

    if not single_chunk:
        @pl.when(ci == pl.num_programs(1) - 1)
        def _():
            for i in n:
                sout_ref[ch[i][0], ch[i][1]] = st_ref[i].T


def hgrn_mix(p, row0, col0, s0, lbp, gn, n_batch, seq, c, nb, ns, layer):
    assert (nb == 1 and seq % (ns * c) == 0) or (ns == 1 and seq == c and c == 8)
    cols = p.shape[1]
    n_chunks = seq // (ns * c)
    rows = nb * ns * c
    t0 = row0 // rows
    const = lambda shape: pl.BlockSpec(shape, lambda b, ci: (0,) * len(shape))
    st_spec = pl.BlockSpec((nb, B_HEADS, B_HEAD_DIM, B_HEAD_DIM), lambda b, ci: (b, 0, 0, 0))
    return pl.pallas_call(
        functools.partial(_hgrn_kernel, c=c, nb=nb, ns=ns, layer=layer, col0=col0, single_chunk=seq == c),
        grid=(n_batch // nb, n_chunks),
        in_specs=[pl.BlockSpec((rows, cols), lambda b, ci: (t0 + b * n_chunks + ci, 0)),
                  st_spec, const((DEPTH, TOK_WIDTH)), const((1, B_HEAD_DIM)), const((rows, rows))],
        out_specs=[pl.BlockSpec((rows, TOK_WIDTH), lambda b, ci: (b * n_chunks + ci, 0)), st_spec],
        out_shape=[jax.ShapeDtypeStruct((n_batch * seq, TOK_WIDTH), BF16),
                   jax.ShapeDtypeStruct((n_batch, B_HEADS, B_HEAD_DIM, B_HEAD_DIM), F32)],
        scratch_shapes=[pltpu.VMEM((nb * B_HEADS, B_HEAD_DIM, B_HEAD_DIM), F32)],
        compiler_params=_cparams("parallel", "arbitrary"),
        name="hgrn_mix",
    )(p, s0, lbp, gn.reshape(1, B_HEAD_DIM), _block_ltri(nb * ns, c))


PROMPT_CHUNK = 64
A_CHUNKS_PER_STEP = 2
B_CHUNKS_PER_STEP = 4
SAMPLE_SEQS_PER_STEP = 8
TM_DENSE = 1024
TM_IN_PROJ = 512


def kernel(x_prompt, x_sample, mem_prompt, cache_mem_k, cache_mem_v, state_rwkv, state_shift, state_hgrn, state_conv, norm_mix, norm_ffn, norm_final, mem_norm, w_mem_kv, a_w_in, a_mu, a_w0, a_w2, a_a0, a_a2, a_g2, a_k_k, a_k_a, a_r_k, a_ln_w, a_ln_b, a_w_out, b_w_in, b_lower_bounds, b_g_norm, b_w_out, ffn_w_up, ffn_conv_w, ffn_conv_b, ffn_w_down):
    bp, sp, d = x_prompt.shape
    bs, ss, _ = x_sample.shape
    mp, ms = bp * sp, bs * ss
    nbs = SAMPLE_SEQS_PER_STEP

    pad_cols = A_PROJ_COLS - a_w_in.shape[2]
    wa = _bf(jnp.pad(a_w_in[0], ((0, 0), (0, pad_cols))))
    mu = jnp.pad(a_mu[0], (0, pad_cols)).reshape(1, A_TOK_COLS)
    vecs = jnp.stack([a_w0[0], a_a0[0], a_k_k[0], a_k_a[0], a_r_k[0], a_ln_w[0], a_ln_b[0],
                      jnp.zeros((TOK_WIDTH,), F32)])
    w2 = _bf(jnp.pad(a_w2[0], ((0, 128 - A_LORA), (0, 0))))
    a2 = _bf(jnp.pad(a_a2[0], ((A_LORA, 256 - 2 * A_LORA), (0, 0))))
    g2 = _bf(jnp.pad(a_g2[0], ((2 * A_LORA - 128, 384 - (2 * A_LORA - 128) - A_GATE_RANK), (0, 0))))
    wb = cast_bf16(b_w_in, 512)[0]
    w_out = [cast_bf16(a_w_out, 1024)[0], cast_bf16(b_w_out, 1024)[0]]
    w_up = cast_bf16(ffn_w_up, 256)
    w_down = cast_bf16(ffn_w_down, 1408)
    w_mem = cast_bf16(w_mem_kv, 2048)

    mem_rows = mem_prompt.reshape(bp * N_MEM, d)
    mem_kv = [norm_matmul([mem_rows], mem_norm[l], (w_mem, l), bp * N_MEM, 512) for l in range(DEPTH)]
    mem_k_prompt = jnp.stack([kv[:, :MEM_WIDTH] for kv in mem_kv]).reshape(DEPTH, bp, N_MEM, MEM_HEADS, MEM_HEAD_DIM)
    mem_v_prompt = jnp.stack([kv[:, MEM_WIDTH:] for kv in mem_kv]).reshape(DEPTH, bp, N_MEM, MEM_HEADS, MEM_HEAD_DIM)

    hist = [state_conv[l].reshape(bs, 2 * D_FF) for l in range(DEPTH)]

    def ffn(x, layer):
        u_p, cs_p = ffn_up_prompt(x, norm_ffn[layer], w_up, layer, ffn_conv_w[layer], ffn_conv_b[layer],
                                  bp, sp, 1024, 512)
        u_s, c6, c7 = ffn_up_sample(x, mp, norm_ffn[layer], w_up, layer, ffn_conv_w[layer], ffn_conv_b[layer],
                                    hist[layer], ss, 1024, 512)
        x_p = matmul([[u_p]], [(w_down, layer)], [x], D_MODEL, TM_DENSE, 512)
        x_s = matmul([[u_s]], [(w_down, layer)], [x], D_MODEL, TM_DENSE, 512, res_row0=mp)
        return [x_p, x_s], cs_p, jnp.stack([c6, c7], axis=1)

    def out_proj(tok, mem_o, w, res_segs):
        return matmul([tok, mem_o], [w[:TOK_WIDTH], w[TOK_WIDTH:]], res_segs, D_MODEL, TM_DENSE, 1024,
                      cols_outer=True)

    x_segs = [x_prompt.reshape(mp, d), x_sample.reshape(ms, d)]
    p = norm_matmul(x_segs, norm_mix[0], wa, TM_IN_PROJ, A_PROJ_COLS // 2, BF16)
    pshift = matmul([[state_shift[0]]], [wa], None, A_PROJ_COLS, bs, 512).reshape(bs, 1, A_PROJ_COLS)
    tok_p, rwkv_p = rwkv_mix(p, 0, jnp.zeros((bp, 1, A_PROJ_COLS), F32),
                             jnp.zeros((bp, A_HEADS, A_HEAD_DIM, A_HEAD_DIM), F32),
                             mu, vecs, w2, a2, g2, bp, sp, PROMPT_CHUNK, 1, A_CHUNKS_PER_STEP)
    tok_s, rwkv_s = rwkv_mix(p, mp, pshift, state_rwkv[0], mu, vecs, w2, a2, g2, bs, ss, ss, nbs, 1)
    mo_p = mem_attention(p, 0, 0, mem_kv[0], bp, sp, 512)
    mo_s = mem_attention_cache(p, mp, 0, cache_mem_k, cache_mem_v, 0, ss, 8)
    x = out_proj([tok_p, tok_s], [mo_p, mo_s], w_out[0], x_segs)
    x_segs, conv_p0, conv_s0 = ffn(x, 0)
    p = norm_matmul(x_segs, norm_mix[1], wb, TM_IN_PROJ, wb.shape[1] // 2, BF16)
    tok_p, hgrn_p = hgrn_mix(p, 0, MEM_WIDTH, jnp.zeros((bp, B_HEADS, B_HEAD_DIM, B_HEAD_DIM), F32),
                             b_lower_bounds, b_g_norm[0], bp, sp, PROMPT_CHUNK, 1, B_CHUNKS_PER_STEP, 1)
    tok_s, hgrn_s = hgrn_mix(p, mp, MEM_WIDTH, state_hgrn[0], b_lower_bounds, b_g_norm[0], bs, ss, ss, nbs, 1, 1)
    mo_p = mem_attention(p, 0, 0, mem_kv[1], bp, sp, 512)
    mo_s = mem_attention_cache(p, mp, 0, cache_mem_k, cache_mem_v, 1, ss, 8)
    x = out_proj([tok_p, tok_s], [mo_p, mo_s], w_out[1], x_segs)
    x_segs, conv_p1, conv_s1 = ffn(x, 1)
    y_p = rmsnorm(x_segs[0], norm_final, 1024)
    y_s = rmsnorm(x_segs[1], norm_final, 1024)
    shift_p = rmsnorm(x_prompt[:, -1], norm_mix[0], bp)
    shift_s = rmsnorm(x_sample[:, -1], norm_mix[0], bs)

    return (y_p.reshape(bp, sp, d), y_s.reshape(bs, ss, d), mem_k_prompt, mem_v_prompt,
            rwkv_p[None], rwkv_s[None], shift_p[None], shift_s[None], hgrn_p[None], hgrn_s[None],
            jnp.stack([conv_p0, conv_p1]), jnp.stack([conv_s0, conv_s1]))
```

```python
import functools

import jax
import jax.numpy as jnp
from jax import lax
from jax.experimental import pallas as pl
from jax.experimental.pallas import tpu as pltpu

F32 = jnp.float32
BF16 = jnp.bfloat16

D_MODEL = 2048
DEPTH = 2
TOK_WIDTH = 1536
MEM_WIDTH = 512
MEM_HEADS = 4
MEM_HEAD_DIM = 128
N_MEM = 256
A_HEAD_DIM = 64
A_HEADS = 24
A_LORA = 96
A_GATE_RANK = 256
A_TOK_COLS = 5120
A_PROJ_COLS = MEM_WIDTH + A_TOK_COLS
B_HEADS = 12
B_HEAD_DIM = 128
D_FF = 5632
RMS_EPS = 1e-6
GN_EPS = 64e-5

LANES = 128
VMEM_LIMIT = 56 * 1024 * 1024


def _cparams(*sem):
    return pltpu.CompilerParams(dimension_semantics=sem, vmem_limit_bytes=VMEM_LIMIT)


def _dot(a, b):
    return jnp.dot(a, b, preferred_element_type=F32)


def _dot_nt(a, b):
    return lax.dot_general(a, b, (((1,), (1,)), ((), ())), preferred_element_type=F32)


def _dot_tn(a, b):
    return lax.dot_general(a, b, (((0,), (0,)), ((), ())), preferred_element_type=F32)


def _bf(x):
    return x.astype(BF16)


def _split(x):
    hi = x.astype(BF16)
    lo = (x - hi.astype(F32)).astype(BF16)
    return hi, lo


def _sigmoid(x):
    return 0.5 * (jnp.tanh(0.5 * x) + 1.0)


def _rms(x, g):
    ms = jnp.mean(x * x, axis=-1, keepdims=True)
    return x * lax.rsqrt(ms + RMS_EPS) * g


def _seg_bounds(segs, tm):
    bounds, off = [], 0
    for a in segs:
        bounds.append(off)
        off += a.shape[0] // tm
    return tuple(bounds), off


def _seg_row_specs(segs, tm, width, col_of_j, tile0=0):
    assert tile0 == 0 or len(segs) == 1
    specs, off = [], 0
    for a in segs:
        n = a.shape[0] // tm
        if col_of_j:
            imap = lambda i, j, off=off, n=n: (jnp.clip(i - off + tile0, 0, n - 1), j)
        else:
            imap = lambda i, j, off=off, n=n: (jnp.clip(i - off + tile0, 0, n - 1), 0)
        specs.append(pl.BlockSpec((tm, width), imap))
        off += n
    return specs


def _for_segment(i, bounds, fn):
    if len(bounds) == 1:
        fn(0)
        return
    for s in range(len(bounds)):
        cond = i >= bounds[s]
        if s + 1 < len(bounds):
            cond = jnp.logical_and(cond, i < bounds[s + 1])
        pl.when(cond)(functools.partial(fn, s))


def _w_spec(w, tn, col_block0=0):
    if isinstance(w, tuple):
        arr, layer = w
        return arr, pl.BlockSpec((None, arr.shape[1], tn), lambda i, j: (layer, 0, col_block0 + j))
    return w, pl.BlockSpec((w.shape[0], tn), lambda i, j: (0, col_block0 + j))


def _cast_kernel(x_ref, o_ref):
    o_ref[...] = x_ref[...].astype(o_ref.dtype)


def cast_bf16(w, tm):
    n_l, m, n = w.shape
    spec = pl.BlockSpec((1, tm, n), lambda l, i: (l, i, 0))
    return pl.pallas_call(
        _cast_kernel,
        grid=(n_l, m // tm),
        in_specs=[spec],
        out_specs=spec,
        out_shape=jax.ShapeDtypeStruct(w.shape, BF16),
        compiler_params=_cparams("parallel", "parallel"),
        name="cast_bf16",
    )(w)


def _rmsnorm_kernel(x_ref, g_ref, o_ref):
    o_ref[...] = _rms(x_ref[...], g_ref[...])


def rmsnorm(x, g, tm, row0=0, rows=None):
    d = x.shape[1]
    rows = x.shape[0] if rows is None else rows
    t0 = row0 // tm
    return pl.pallas_call(
        _rmsnorm_kernel,
        grid=(rows // tm,),
        in_specs=[pl.BlockSpec((tm, d), lambda i: (t0 + i, 0)), pl.BlockSpec((1, d), lambda i: (0, 0))],
        out_specs=pl.BlockSpec((tm, d), lambda i: (i, 0)),
        out_shape=jax.ShapeDtypeStruct((rows, d), F32),
        compiler_params=_cparams("parallel"),
        name="rmsnorm",
    )(x, g.reshape(1, d))


NORM_ROWS = 256


def _norm_to_scratch(x_ref, g, hb_ref, tm, dst_off=0):
    for r in range(0, tm, NORM_ROWS):
        n = min(NORM_ROWS, tm - r)
        hb_ref[dst_off + r:dst_off + r + n, :] = _bf(_rms(x_ref[r:r + n, :], g))


def _norm_matmul_kernel(*refs, n_seg, bounds, tm):
    x_refs = refs[:n_seg]
    g_ref, w_ref, o_ref, hb_ref = refs[n_seg:]

    @pl.when(pl.program_id(1) == 0)
    def _():
        g = g_ref[...]
        _for_segment(pl.program_id(0), bounds, lambda s: _norm_to_scratch(x_refs[s], g, hb_ref, tm))

    o_ref[...] = _dot(hb_ref[...], w_ref[...]).astype(o_ref.dtype)


def norm_matmul(x_segs, g, w, tm, tn, out_dtype=F32):
    d = x_segs[0].shape[1]
    bounds, n_tiles = _seg_bounds(x_segs, tm)
    w_arr, w_spec = _w_spec(w, tn)
    n = w_arr.shape[-1]
    return pl.pallas_call(
        functools.partial(_norm_matmul_kernel, n_seg=len(x_segs), bounds=bounds, tm=tm),
        grid=(n_tiles, n // tn),
        in_specs=_seg_row_specs(x_segs, tm, d, False) + [pl.BlockSpec((1, d), lambda i, j: (0, 0)), w_spec],
        out_specs=pl.BlockSpec((tm, tn), lambda i, j: (i, j)),
        out_shape=jax.ShapeDtypeStruct((n_tiles * tm, n), out_dtype),
        scratch_shapes=[pltpu.VMEM((tm, d), BF16)],
        compiler_params=_cparams("parallel", "arbitrary"),
        name="norm_matmul",
    )(*x_segs, g.reshape(1, d), w_arr)


def _mm_kernel(*refs, a_counts, res_count, bounds, row_axis):
    pos = 0
    a_refs = []
    for cnt in a_counts:
        a_refs.append(refs[pos:pos + cnt])
        pos += cnt
    w_refs = refs[pos:pos + len(a_counts)]
    pos += len(a_counts)
    res_refs = refs[pos:pos + res_count]
    o_ref = refs[-1]

    def compute(s):
        acc = None
        for k, segs in enumerate(a_refs):
            term = _dot(_bf(segs[min(s, len(segs) - 1)][...]), w_refs[k][...])
            acc = term if acc is None else acc + term
        if res_count:
            acc = res_refs[min(s, res_count - 1)][...] + acc
        o_ref[...] = acc

    _for_segment(pl.program_id(row_axis), bounds, compute)


def matmul(a_list, w_list, res_segs, n_out, tm, tn, res_row0=0, cols_outer=False):
    longest = max(a_list + ([res_segs] if res_segs else []), key=len)
    bounds, n_tiles = _seg_bounds(longest, tm)
    in_specs, args = [], []
    for segs in a_list:
        assert len(segs) in (1, len(bounds))
        in_specs += _seg_row_specs(segs, tm, segs[0].shape[1], False)
        args += list(segs)
    for w in w_list:
        w_arr, w_spec = _w_spec(w, tn)
        in_specs.append(w_spec)
        args.append(w_arr)
    if res_segs:
        assert len(res_segs) in (1, len(bounds))
        in_specs += _seg_row_specs(res_segs, tm, tn, True, res_row0 // tm)
        args += list(res_segs)
    out_spec = pl.BlockSpec((tm, tn), lambda i, j: (i, j))
    grid = (n_tiles, n_out // tn)
    if cols_outer:
        swap = lambda spec: pl.BlockSpec(spec.block_shape, lambda j, i, f=spec.index_map: f(i, j))
        in_specs, out_spec, grid = [swap(sp) for sp in in_specs], swap(out_spec), grid[::-1]
    return pl.pallas_call(
        functools.partial(_mm_kernel, a_counts=tuple(len(s) for s in a_list),
                          res_count=len(res_segs) if res_segs else 0, bounds=bounds,
                          row_axis=1 if cols_outer else 0),
        grid=grid,
        in_specs=in_specs,
        out_specs=out_spec,
        out_shape=jax.ShapeDtypeStruct((n_tiles * tm, n_out), F32),
        compiler_params=_cparams("parallel", "parallel"),
        name="matmul",
    )(*args)


def _attn_kernel(q_ref, k_ref, v_ref, o_ref, *, n_seq, tq, cache_layout):
    scale = MEM_HEAD_DIM ** -0.5
    pairs = [(s, h) for s in range(n_seq) for h in range(MEM_HEADS)]
    hsl = [slice(h * MEM_HEAD_DIM, (h + 1) * MEM_HEAD_DIM) for h in range(MEM_HEADS)]
    q = [_bf(q_ref[s * tq:(s + 1) * tq, hsl[h]]) for s, h in pairs]
    if cache_layout:
        k = [_bf(k_ref[0, s, pl.ds(h, N_MEM, stride=MEM_HEADS), :]) for s, h in pairs]
        v = [_bf(v_ref[0, s, pl.ds(h, N_MEM, stride=MEM_HEADS), :]) for s, h in pairs]
    else:
        k = [_bf(k_ref[s * N_MEM:(s + 1) * N_MEM, hsl[h]]) for s, h in pairs]
        v = [_bf(v_ref[s * N_MEM:(s + 1) * N_MEM, hsl[h]]) for s, h in pairs]
    sc = [_dot_nt(q[i], k[i]) * scale for i in range(len(pairs))]
    e = [jnp.exp(x - jnp.max(x, axis=-1, keepdims=True)) for x in sc]
    p = [_bf(x / jnp.sum(x, axis=-1, keepdims=True)) for x in e]
    o = [_dot(p[i], v[i]) for i in range(len(pairs))]
    rows = [jnp.concatenate(o[s * MEM_HEADS:(s + 1) * MEM_HEADS], axis=1) for s in range(n_seq)]
    o_ref[...] = _bf(jnp.concatenate(rows, axis=0) if n_seq > 1 else rows[0])


def mem_attention(p, row0, q_colblock, kv, n_batch, seq, tq):
    q_tiles = seq // tq
    t0 = row0 // tq
    return pl.pallas_call(
        functools.partial(_attn_kernel, n_seq=1, tq=tq, cache_layout=False),
        grid=(n_batch, q_tiles),
        in_specs=[pl.BlockSpec((tq, MEM_WIDTH), lambda b, t: (t0 + b * q_tiles + t, q_colblock)),
                  pl.BlockSpec((N_MEM, MEM_WIDTH), lambda b, t: (b, 0)),
                  pl.BlockSpec((N_MEM, MEM_WIDTH), lambda b, t: (b, 1))],
        out_specs=pl.BlockSpec((tq, MEM_WIDTH), lambda b, t: (b * q_tiles + t, 0)),
        out_shape=jax.ShapeDtypeStruct((n_batch * seq, MEM_WIDTH), BF16),
        compiler_params=_cparams("parallel", "parallel"),
        name="mem_attention",
    )(p, kv, kv)


def mem_attention_cache(p, row0, q_colblock, cache_k, cache_v, layer, seq, n_seq):
    depth, n_batch = cache_k.shape[:2]
    cache_k = cache_k.reshape(depth, n_batch, N_MEM * MEM_HEADS, MEM_HEAD_DIM)
    cache_v = cache_v.reshape(depth, n_batch, N_MEM * MEM_HEADS, MEM_HEAD_DIM)
    rows = n_seq * seq
    t0 = row0 // rows
    kv_spec = pl.BlockSpec((1, n_seq, N_MEM * MEM_HEADS, MEM_HEAD_DIM), lambda b: (layer, b, 0, 0))
    return pl.pallas_call(
        functools.partial(_attn_kernel, n_seq=n_seq, tq=seq, cache_layout=True),
        grid=(n_batch // n_seq,),
        in_specs=[pl.BlockSpec((rows, MEM_WIDTH), lambda b: (t0 + b, q_colblock)), kv_spec, kv_spec],
        out_specs=pl.BlockSpec((rows, MEM_WIDTH), lambda b: (b, 0)),
        out_shape=jax.ShapeDtypeStruct((n_batch * seq, MEM_WIDTH), BF16),
        compiler_params=_cparams("parallel"),
        name="mem_attention_cache",
    )(p, cache_k, cache_v)


FFN_HALO = 16


def _gelu_gate(c, v):
    return _bf(jax.nn.gelu(c) * v)


def _ffn_up_prompt_kernel(x_ref, xh_ref, g_ref, wa_ref, wv_ref, cw_ref, cb_ref, u_ref, cs_ref, hb_ref,
                          *, tm, tiles_per_seq):
    i = pl.program_id(0)

    @pl.when(pl.program_id(1) == 0)
    def _():
        g = g_ref[...]
        hb_ref[0:FFN_HALO, :] = _bf(_rms(xh_ref[...], g))
        _norm_to_scratch(x_ref, g, hb_ref, tm, dst_off=FFN_HALO)

    a_ext = _dot(hb_ref[...], wa_ref[...])
    v = _dot(hb_ref[FFN_HALO:, :], wv_ref[...])
    rows = lax.broadcasted_iota(jnp.int32, (FFN_HALO + tm, 1), 0)
    n_zero = jnp.where((i % tiles_per_seq) == 0, FFN_HALO, 0)
    a_ext = jnp.where(rows < n_zero, 0.0, a_ext)
    a0 = a_ext[FFN_HALO:]
    a1 = pltpu.roll(a_ext, 1, 0)[FFN_HALO:]
    a2 = pltpu.roll(a_ext, 2, 0)[FFN_HALO:]
    c = cb_ref[...] + a2 * cw_ref[0:1, :] + a1 * cw_ref[1:2, :] + a0 * cw_ref[2:3, :]
    u_ref[...] = _gelu_gate(c, v)
    cs_ref[0] = a0[tm - 8:tm][6:8]


def ffn_up_prompt(x, g, w_up, layer, cw, cb, n_batch, seq, tm, tn):
    d = x.shape[1]
    m = n_batch * seq
    nf = D_FF // tn
    tiles_per_seq = seq // tm
    halo_blocks = tm // FFN_HALO
    w_arr, wa_spec = _w_spec((w_up, layer), tn)
    _, wv_spec = _w_spec((w_up, layer), tn, nf)
    u, cs = pl.pallas_call(
        functools.partial(_ffn_up_prompt_kernel, tm=tm, tiles_per_seq=tiles_per_seq),
        grid=(m // tm, nf),
        in_specs=[pl.BlockSpec((tm, d), lambda i, j: (i, 0)),
                  pl.BlockSpec((FFN_HALO, d), lambda i, j: (jnp.maximum(i * halo_blocks - 1, 0), 0)),
                  pl.BlockSpec((1, d), lambda i, j: (0, 0)),
                  wa_spec, wv_spec,
                  pl.BlockSpec((3, tn), lambda i, j: (0, j)),
                  pl.BlockSpec((1, tn), lambda i, j: (0, j))],
        out_specs=[pl.BlockSpec((tm, tn), lambda i, j: (i, j)),
                   pl.BlockSpec((1, 2, tn), lambda i, j: (i, 0, j))],
        out_shape=[jax.ShapeDtypeStruct((m, D_FF), BF16),
                   jax.ShapeDtypeStruct((m // tm, 2, D_FF), F32)],
        scratch_shapes=[pltpu.VMEM((FFN_HALO + tm, d), BF16)],
        compiler_params=_cparams("parallel", "arbitrary"),
        name="ffn_up_prompt",
    )(x, x, g.reshape(1, d), w_arr, w_arr, cw, cb.reshape(1, D_FF))
    return u, cs.reshape(n_batch, tiles_per_seq, 2, D_FF)[:, -1]


def _ffn_up_sample_kernel(x_ref, g_ref, wa_ref, wv_ref, cw_ref, cb_ref, h0_ref, h1_ref,
                          u_ref, c6_ref, c7_ref, hb_ref, a_ref, a1_ref, a2_ref, *, tm, seq):
    @pl.when(pl.program_id(1) == 0)
    def _():
        _norm_to_scratch(x_ref, g_ref[...], hb_ref, tm)

    a = _dot(hb_ref[...], wa_ref[...])
    v = _dot(hb_ref[...], wv_ref[...])
    r1 = pltpu.roll(a, 1, 0)
    r2 = pltpu.roll(a, 2, 0)
    nseq = tm // seq
    firsts = pl.ds(0, nseq, stride=seq)
    seconds = pl.ds(1, nseq, stride=seq)
    for q in range(a.shape[1] // LANES):
        sl = slice(q * LANES, (q + 1) * LANES)
        a1_ref[q] = r1[:, sl]
        a1_ref[q, firsts, :] = h1_ref[:, sl]
        a2_ref[q] = r2[:, sl]
        a2_ref[q, firsts, :] = h0_ref[:, sl]
        a2_ref[q, seconds, :] = h1_ref[:, sl]
        a_ref[q] = a[:, sl]
        c6_ref[:, sl] = a_ref[q, pl.ds(seq - 2, nseq, stride=seq), :]
        c7_ref[:, sl] = a_ref[q, pl.ds(seq - 1, nseq, stride=seq), :]
    n_q = a.shape[1] // LANES
    a1 = jnp.concatenate([a1_ref[q] for q in range(n_q)], axis=1)
    a2 = jnp.concatenate([a2_ref[q] for q in range(n_q)], axis=1)
    c = cb_ref[...] + a2 * cw_ref[0:1, :] + a1 * cw_ref[1:2, :] + a * cw_ref[2:3, :]
    u_ref[...] = _gelu_gate(c, v)


def ffn_up_sample(x, row0, g, w_up, layer, cw, cb, hist, seq, tm, tn):
    d = x.shape[1]
    m = hist.shape[0] * seq
    nf = D_FF // tn
    nb = tm // seq
    t0 = row0 // tm
    w_arr, wa_spec = _w_spec((w_up, layer), tn)
    _, wv_spec = _w_spec((w_up, layer), tn, nf)
    rows_scratch = pltpu.VMEM((tn // LANES, tm, LANES), F32)
    return pl.pallas_call(
        functools.partial(_ffn_up_sample_kernel, tm=tm, seq=seq),
        grid=(m // tm, nf),
        in_specs=[pl.BlockSpec((tm, d), lambda i, j: (t0 + i, 0)),
                  pl.BlockSpec((1, d), lambda i, j: (0, 0)),
                  wa_spec, wv_spec,
                  pl.BlockSpec((3, tn), lambda i, j: (0, j)),
                  pl.BlockSpec((1, tn), lambda i, j: (0, j)),
                  pl.BlockSpec((nb, tn), lambda i, j: (i, j)),
                  pl.BlockSpec((nb, tn), lambda i, j: (i, nf + j))],
        out_specs=[pl.BlockSpec((tm, tn), lambda i, j: (i, j)),
                   pl.BlockSpec((nb, tn), lambda i, j: (i, j)),
                   pl.BlockSpec((nb, tn), lambda i, j: (i, j))],
        out_shape=[jax.ShapeDtypeStruct((m, D_FF), BF16),
                   jax.ShapeDtypeStruct((m // seq, D_FF), F32),
                   jax.ShapeDtypeStruct((m // seq, D_FF), F32)],
        scratch_shapes=[pltpu.VMEM((tm, d), BF16), rows_scratch, rows_scratch, rows_scratch],
        compiler_params=_cparams("parallel", "arbitrary"),
        name="ffn_up_sample",
    )(x, g.reshape(1, d), w_arr, w_arr, cw, cb.reshape(1, D_FF), hist, hist)


def _seg_sum(x, e, two_pass):
    parts = []
    for j in range(x.shape[1] // LANES):
        xj = x[:, j * LANES:(j + 1) * LANES]
        if two_pass:
            hi, lo = _split(xj)
            parts.append(_dot(hi, e) + _dot(lo, e))
        else:
            parts.append(_dot(_bf(xj), e))
    return jnp.concatenate(parts, axis=1)


def _bcast_rows(x, idx, nb, c):
    parts = [jnp.broadcast_to(x[b * c + idx:b * c + idx + 1, :], (c, x.shape[1])) for b in range(nb)]
    return parts[0] if nb == 1 else jnp.concatenate(parts, axis=0)


def _unit_lower_solve(a_list, rhs_list, c):
    mm = lambda x, y: _dot(_bf(x), _bf(y))
    n = range(len(a_list))
    rows = lax.broadcasted_iota(jnp.int32, (c, c), 0)
    cols = lax.broadcasted_iota(jnp.int32, (c, c), 1)
    eye = (rows == cols).astype(F32)
    blk = min(c, 16)
    if c > blk:
        assert c // blk <= 4
        same = (rows // blk) == (cols // blk)
        ad = [jnp.where(same, a, 0.0) for a in a_list]
        ao = [a_list[i] - ad[i] for i in n]
    else:
        ad = a_list
    t = [eye - ad[i] for i in n]
    pw = ad
    span = 2
    while span < blk:
        pw = [mm(pw[i], pw[i]) for i in n]
        t = [t[i] + mm(t[i], pw[i]) for i in n]
        span *= 2
    x = [mm(t[i], rhs_list[i]) for i in n]
    if c > blk:
        nn = [mm(t[i], ao[i]) for i in n]
        n2 = [mm(nn[i], nn[i]) for i in n]
        x = [x[i] + mm(n2[i], x[i]) for i in n]
        x = [x[i] - mm(nn[i], x[i]) for i in n]
    return x


def _rwkv_chains(n, ch, c, s_old, xk, xr, kb, bb, v, kh, bh, gam, strict, incl):
    rs = {i: slice(ch[i][0] * c, (ch[i][0] + 1) * c) for i in n}
    ls = {i: slice(ch[i][1] * A_HEAD_DIM, (ch[i][1] + 1) * A_HEAD_DIM) for i in n}
    sb = {i: _bf(s_old[i]) for i in n}
    xk_h = {i: _bf(xk[rs[i], ls[i]]) for i in n}
    xr_h = {i: _bf(xr[rs[i], ls[i]]) for i in n}
    kb_h = {i: _bf(kb[rs[i], ls[i]]) for i in n}
    bb_h = {i: _bf(bb[rs[i], ls[i]]) for i in n}
    v_h = {i: _bf(v[rs[i], ls[i]]) for i in n}
    a_kk = {i: jnp.where(strict, _dot_nt(xk_h[i], kb_h[i]), 0.0) for i in n}
    a_kb = {i: jnp.where(strict, _dot_nt(xk_h[i], bb_h[i]), 0.0) for i in n}
    a_rk = {i: jnp.where(incl, _dot_nt(xr_h[i], kb_h[i]), 0.0) for i in n}
    a_rb = {i: jnp.where(incl, _dot_nt(xr_h[i], bb_h[i]), 0.0) for i in n}
    rhs = {i: -(_dot_nt(xk_h[i], sb[i]) + _dot(_bf(a_kk[i]), v_h[i])) for i in n}
    u = dict(zip(n, _unit_lower_solve([a_kb[i] for i in n], [rhs[i] for i in n], c)))
    u_h = {i: _bf(u[i]) for i in n}
    y_h = {i: _dot_nt(xr_h[i], sb[i]) + _dot(_bf(a_rk[i]), v_h[i]) + _dot(_bf(a_rb[i]), u_h[i]) for i in n}
    if c % 16 == 0:
        vu = {i: jnp.concatenate([v_h[i], u_h[i]], axis=0) for i in n}
        kbh = {i: jnp.concatenate([_bf(kh[rs[i], ls[i]]), _bf(bh[rs[i], ls[i]])], axis=0) for i in n}
    else:
        vu = {i: _bf(jnp.concatenate([v[rs[i], ls[i]], u[i]], axis=0)) for i in n}
        kbh = {i: _bf(jnp.concatenate([kh[rs[i], ls[i]], bh[rs[i], ls[i]]], axis=0)) for i in n}
    s_new = {i: s_old[i] * gam[ch[i][0] * c:ch[i][0] * c + 1, ls[i]] + _dot_tn(vu[i], kbh[i]) for i in n}
    return y_h, s_new


def _rwkv_kernel(p_ref, ps_ref, s0_ref, mu_ref, vec_ref, w2_ref, a2_ref, g2_ref, e_ref, lt_ref,
                 tok_ref, sout_ref, prev_ref, s_ref, y_ref, *, c, nb, ns):
    ci = pl.program_id(1)
    tw = TOK_WIDTH
    lora0 = 3 * tw
    groups = nb * ns

    @pl.when(ci == 0)
    def _():
        for b in range(nb):
            s_ref[b * A_HEADS:(b + 1) * A_HEADS] = s0_ref[b]
        prev_ref[...] = ps_ref[:, :, MEM_WIDTH:]

    p = p_ref[:, MEM_WIDTH:].astype(F32)
    rows = lax.broadcasted_iota(jnp.int32, (groups * c, 1), 0)
    p_prev = pltpu.roll(p, 1, 0)
    for b in range(nb):
        p_prev = jnp.where(rows == b * ns * c, prev_ref[b], p_prev)
        prev_ref[b] = p[(b + 1) * ns * c - 1:(b + 1) * ns * c, :]
    xm = p + mu_ref[...] * (p_prev - p)

    r = xm[:, 0:tw]
    k = xm[:, tw:2 * tw]
    v = xm[:, 2 * tw:3 * tw]
    xw = xm[:, lora0:lora0 + 128]
    xa = xm[:, lora0:lora0 + 256]
    xg = xm[:, lora0 + 128:lora0 + 512]
    w0, a0, k_k, k_a, r_k, ln_w, ln_b = (vec_ref[i:i + 1, :] for i in range(7))

    ell = -jnp.exp(-0.5) * _sigmoid(w0 + _dot(_bf(jnp.tanh(xw)), w2_ref[...]))
    a = _sigmoid(a0 + _dot(_bf(xa), a2_ref[...]))
    gate = _dot(_bf(_sigmoid(xg)), g2_ref[...])
    kkraw = k * k_k
    k2 = k * (1.0 + (a - 1.0) * k_a)
    e = e_ref[...]
    kap = kkraw * lax.rsqrt(jnp.maximum(_seg_sum(kkraw * kkraw, e, True), 1e-24))
    bet = kap * a

    ell_hi, ell_lo = _split(ell)
    gc = _dot(lt_ref[...], ell_hi) + _dot(lt_ref[...], ell_lo)
    glast = _bcast_rows(gc, c - 1, groups, c)
    egi = jnp.exp(-gc)
    el = jnp.exp(glast - gc)
    xk = kap * jnp.exp(gc - ell)
    xr = r * jnp.exp(gc)
    kb = k2 * egi
    bb = bet * egi
    kh = k2 * el
    bh = bet * el
    gam = jnp.exp(glast)
    ti = lax.broadcasted_iota(jnp.int32, (c, c), 0)
    si = lax.broadcasted_iota(jnp.int32, (c, c), 1)
    strict = si < ti
    incl = si <= ti

    n_chains = nb * A_HEADS
    state = {i: s_ref[i] for i in range(n_chains)}
    y_h = {}
    if c % 16 == 0:
        xk, xr, kb, bb, v_op, kh, bh = (_bf(t) for t in (xk, xr, kb, bb, v, kh, bh))
    else:
        v_op = v
    for s in range(ns):
        ch = [(b * ns + s, h) for b in range(nb) for h in range(A_HEADS)]
        ys, state = _rwkv_chains(range(n_chains), ch, c, state, xk, xr, kb, bb, v_op, kh, bh, gam, strict, incl)
        for i in range(n_chains):
            y_h[ch[i]] = ys[i]
    for (grp, h), val in y_h.items():
        y_ref[grp * c:(grp + 1) * c, h * A_HEAD_DIM:(h + 1) * A_HEAD_DIM] = val
    for i in range(n_chains):
        s_ref[i] = state[i]

    y = y_ref[...]
    inv_n = 1.0 / A_HEAD_DIM
    mean = _seg_sum(y, e, False) * inv_n
    d = y - mean
    var = _seg_sum(d * d, e, False) * inv_n
    yn = d * lax.rsqrt(var + GN_EPS) * ln_w + ln_b
    bonus = _seg_sum(r * k2 * r_k, e, False) * v
    tok_ref[...] = ((yn + bonus) * gate).astype(tok_ref.dtype)

    @pl.when(ci == pl.num_programs(1) - 1)
    def _():
        for b in range(nb):
            sout_ref[b] = s_ref[b * A_HEADS:(b + 1) * A_HEADS]


def _block_ltri(nb, c):
    t = jnp.arange(nb * c)
    return ((t[None, :] <= t[:, None]) & (t[None, :] // c == t[:, None] // c)).astype(BF16)


def rwkv_mix(p, row0, pshift, s0, mu, vecs, w2, a2, g2, n_batch, seq, c, nb, ns):
    assert (nb == 1 and seq % (ns * c) == 0) or (ns == 1 and seq == c and c == 8)
    n_chunks = seq // (ns * c)
    rows = nb * ns * c
    t0 = row0 // rows
    lane = jnp.arange(LANES)
    e = (lane[:, None] // A_HEAD_DIM == lane[None, :] // A_HEAD_DIM).astype(BF16)
    const = lambda shape: pl.BlockSpec(shape, lambda b, ci: (0,) * len(shape))
    st_spec = pl.BlockSpec((nb, A_HEADS, A_HEAD_DIM, A_HEAD_DIM), lambda b, ci: (b, 0, 0, 0))
    return pl.pallas_call(
        functools.partial(_rwkv_kernel, c=c, nb=nb, ns=ns),
        grid=(n_batch // nb, n_chunks),
        in_specs=[pl.BlockSpec((rows, A_PROJ_COLS), lambda b, ci: (t0 + b * n_chunks + ci, 0)),
                  pl.BlockSpec((nb, 1, A_PROJ_COLS), lambda b, ci: (b, 0, 0)),
                  st_spec,
                  const((1, A_TOK_COLS)), const((8, TOK_WIDTH)),
                  const(w2.shape), const(a2.shape), const(g2.shape), const((LANES, LANES)), const((rows, rows))],
        out_specs=[pl.BlockSpec((rows, TOK_WIDTH), lambda b, ci: (b * n_chunks + ci, 0)), st_spec],
        out_shape=[jax.ShapeDtypeStruct((n_batch * seq, TOK_WIDTH), BF16),
                   jax.ShapeDtypeStruct((n_batch, A_HEADS, A_HEAD_DIM, A_HEAD_DIM), F32)],
        scratch_shapes=[pltpu.VMEM((nb, 1, A_TOK_COLS), F32),
                        pltpu.VMEM((nb * A_HEADS, A_HEAD_DIM, A_HEAD_DIM), F32),
                        pltpu.VMEM((rows, TOK_WIDTH), F32)],
        compiler_params=_cparams("parallel", "arbitrary"),
        name="rwkv_mix",
    )(p, pshift, s0, mu, vecs, w2, a2, g2, e, _block_ltri(nb * ns, c))


def _hgrn_kernel(p_ref, s0_ref, lbp_ref, gn_ref, lt_ref, tok_ref, sout_ref, st_ref, *, c, nb, ns, layer, col0,
                 single_chunk):
    ci = pl.program_id(1)
    tw = TOK_WIDTH
    hd = B_HEAD_DIM
    ch = [(b, h) for b in range(nb) for h in range(B_HEADS)]
    n = range(len(ch))

    if not single_chunk:
        @pl.when(ci == 0)
        def _():
            for i in n:
                st_ref[i] = s0_ref[ch[i][0], ch[i][1]].T

    lbp = lbp_ref[...]
    mx = jnp.max(lbp, axis=0, keepdims=True)
    ex = jnp.exp(lbp - mx)
    den = jnp.sum(ex, axis=0, keepdims=True)
    lb = jnp.zeros((1, tw), F32)
    for i in range(1, layer + 1):
        lb = lb + ex[i:i + 1, :] / den

    q = p_ref[:, col0:col0 + tw].astype(F32)
    f = p_ref[:, col0 + tw:col0 + 2 * tw].astype(F32)
    iv = p_ref[:, col0 + 2 * tw:col0 + 3 * tw].astype(F32)
    og = p_ref[:, col0 + 3 * tw:col0 + 4 * tw].astype(F32)
    fg = lb + (1.0 - lb) * _sigmoid(f)
    lf = jnp.log(fg)
    kk = 1.0 - fg
    qq = q * _sigmoid(q)
    lf_hi, lf_lo = _split(lf)
    bc = _dot(lt_ref[...], lf_hi) + _dot(lt_ref[...], lf_lo)
    blast = _bcast_rows(bc, c - 1, nb * ns, c)
    mid = _bcast_rows(bc, (c - 1) // 2, nb * ns, c)
    qe = qq * jnp.exp(bc - mid)
    ke = kk * jnp.exp(mid - bc)
    qs = qq * jnp.exp(bc)
    kl = kk * jnp.exp(blast - bc)
    gam = jnp.exp(blast)
    gate = og * _sigmoid(og)
    ti = lax.broadcasted_iota(jnp.int32, (c, c), 0)
    si = lax.broadcasted_iota(jnp.int32, (c, c), 1)
    incl = si <= ti
    gn = gn_ref[...]
    ls = [slice(h * hd, (h + 1) * hd) for _, h in ch]
    st = [s0_ref[ch[i][0], ch[i][1]] if single_chunk else st_ref[i] for i in n]
    out = []
    for s in range(ns):
        grp = [ch[i][0] * ns + s for i in n]
        rs = [slice(g * c, (g + 1) * c) for g in grp]
        v_h = [_bf(iv[rs[i], ls[i]]) for i in n]
        att = [jnp.where(incl, _dot_nt(_bf(qe[rs[i], ls[i]]), _bf(ke[rs[i], ls[i]])), 0.0) for i in n]
        if single_chunk:
            o = [_dot(_bf(att[i]), v_h[i]) + _dot(_bf(qs[rs[i], ls[i]]), _bf(st[i])) for i in n]
            ones = jnp.ones((c, hd), BF16)
            lf_parts = [_split(lf[rs[i], ls[i]]) for i in n]
            gam_col = [jnp.exp(_dot_tn(hi, ones) + _dot_tn(lo, ones)) for hi, lo in lf_parts]
            st = [gam_col[i] * st[i] + _dot_tn(_bf(kl[rs[i], ls[i]]), v_h[i]) for i in n]
        else:
            o = [_dot(_bf(att[i]), v_h[i]) + _dot_nt(_bf(qs[rs[i], ls[i]]), _bf(st[i])) for i in n]
            st = [st[i] * gam[grp[i] * c:grp[i] * c + 1, ls[i]] + _dot_tn(v_h[i], _bf(kl[rs[i], ls[i]]))
                  for i in n]
        on = [x * lax.rsqrt(jnp.mean(x * x, axis=-1, keepdims=True) + RMS_EPS) * gn for x in o]
        out += [(rs[i], ls[i], on[i]) for i in n]
    for r_sl, l_sl, val in out:
        tok_ref[r_sl, l_sl] = (val * gate[r_sl, l_sl]).astype(tok_ref.dtype)
    for i in n:
        if single_chunk:
            sout_ref[ch[i][0], ch[i][1]] = st[i]
        else:
            st_ref[i] = st[i]

    if not single_chunk:
        @pl.when(ci == pl.num_programs(1) - 1)
        def _():
            for i in n:
                sout_ref[ch[i][0], ch[i][1]] = st_ref[i].T


def hgrn_mix(p, row0, col0, s0, lbp, gn, n_batch, seq, c, nb, ns, layer):
    assert (nb == 1 and seq % (ns * c) == 0) or (ns == 1 and seq == c and c == 8)
    cols = p.shape[1]
    n_chunks = seq // (ns * c)
    rows = nb * ns * c
    t0 = row0 // rows
    const = lambda shape: pl.BlockSpec(shape, lambda b, ci: (0,) * len(shape))
    st_spec = pl.BlockSpec((nb, B_HEADS, B_HEAD_DIM, B_HEAD_DIM), lambda b, ci: (b, 0, 0, 0))
    return pl.pallas_call(
        functools.partial(_hgrn_kernel, c=c, nb=nb, ns=ns, layer=layer, col0=col0, single_chunk=seq == c),
        grid=(n_batch // nb, n_chunks),
        in_specs=[pl.BlockSpec((rows, cols), lambda b, ci: (t0 + b * n_chunks + ci, 0)),
                  st_spec, const((DEPTH, TOK_WIDTH)), const((1, B_HEAD_DIM)), const((rows, rows))],
        out_specs=[pl.BlockSpec((rows, TOK_WIDTH), lambda b, ci: (b * n_chunks + ci, 0)), st_spec],
        out_shape=[jax.ShapeDtypeStruct((n_batch * seq, TOK_WIDTH), BF16),
                   jax.ShapeDtypeStruct((n_batch, B_HEADS, B_HEAD_DIM, B_HEAD_DIM), F32)],
        scratch_shapes=[pltpu.VMEM((nb * B_HEADS, B_HEAD_DIM, B_HEAD_DIM), F32)],
        compiler_params=_cparams("parallel", "arbitrary"),
        name="hgrn_mix",
    )(p, s0, lbp, gn.reshape(1, B_HEAD_DIM), _block_ltri(nb * ns, c))


PROMPT_CHUNK = 64
A_CHUNKS_PER_STEP = 2
B_CHUNKS_PER_STEP = 4
SAMPLE_SEQS_PER_STEP = 8
TM_DENSE = 1024
TM_IN_PROJ = 512


def kernel(x_prompt, x_sample, mem_prompt, cache_mem_k, cache_mem_v, state_rwkv, state_shift, state_hgrn, state_conv, norm_mix, norm_ffn, norm_final, mem_norm, w_mem_kv, a_w_in, a_mu, a_w0, a_w2, a_a0, a_a2, a_g2, a_k_k, a_k_a, a_r_k, a_ln_w, a_ln_b, a_w_out, b_w_in, b_lower_bounds, b_g_norm, b_w_out, ffn_w_up, ffn_conv_w, ffn_conv_b, ffn_w_down):
    bp, sp, d = x_prompt.shape
    bs, ss, _ = x_sample.shape
    mp, ms = bp * sp, bs * ss
    nbs = SAMPLE_SEQS_PER_STEP

    pad_cols = A_PROJ_COLS - a_w_in.shape[2]
    wa = _bf(jnp.pad(a_w_in[0], ((0, 0), (0, pad_cols))))
    mu = jnp.pad(a_mu[0], (0, pad_cols)).reshape(1, A_TOK_COLS)
    vecs = jnp.stack([a_w0[0], a_a0[0], a_k_k[0], a_k_a[0], a_r_k[0], a_ln_w[0], a_ln_b[0],
                      jnp.zeros((TOK_WIDTH,), F32)])
    w2 = _bf(jnp.pad(a_w2[0], ((0, 128 - A_LORA), (0, 0))))
    a2 = _bf(jnp.pad(a_a2[0], ((A_LORA, 256 - 2 * A_LORA), (0, 0))))
    g2 = _bf(jnp.pad(a_g2[0], ((2 * A_LORA - 128, 384 - (2 * A_LORA - 128) - A_GATE_RANK), (0, 0))))
    wb = cast_bf16(b_w_in, 512)[0]
    w_out = [cast_bf16(a_w_out, 1024)[0], cast_bf16(b_w_out, 1024)[0]]
    w_up = cast_bf16(ffn_w_up, 256)
    w_down = cast_bf16(ffn_w_down, 1408)

    mem_rows = mem_prompt.reshape(bp * N_MEM, d)
    mem_kv = [norm_matmul([mem_rows], mem_norm[l], _bf(w_mem_kv[l]), bp * N_MEM, 512) for l in range(DEPTH)]
    mem_k_prompt = jnp.stack([kv[:, :MEM_WIDTH] for kv in mem_kv]).reshape(DEPTH, bp, N_MEM, MEM_HEADS, MEM_HEAD_DIM)
    mem_v_prompt = jnp.stack([kv[:, MEM_WIDTH:] for kv in mem_kv]).reshape(DEPTH, bp, N_MEM, MEM_HEADS, MEM_HEAD_DIM)

    hist = [state_conv[l].reshape(bs, 2 * D_FF) for l in range(DEPTH)]

    def ffn(x, layer):
        u_p, cs_p = ffn_up_prompt(x, norm_ffn[layer], w_up, layer, ffn_conv_w[layer], ffn_conv_b[layer],
                                  bp, sp, 1024, 512)
        u_s, c6, c7 = ffn_up_sample(x, mp, norm_ffn[layer], w_up, layer, ffn_conv_w[layer], ffn_conv_b[layer],
                                    hist[layer], ss, 1024, 512)
        x_p = matmul([[u_p]], [(w_down, layer)], [x], D_MODEL, TM_DENSE, 512)
        x_s = matmul([[u_s]], [(w_down, layer)], [x], D_MODEL, TM_DENSE, 512, res_row0=mp)
        return [x_p, x_s], cs_p, jnp.stack([c6, c7], axis=1)

    def out_proj(tok, mem_o, w, res_segs):
        return matmul([tok, mem_o], [w[:TOK_WIDTH], w[TOK_WIDTH:]], res_segs, D_MODEL, TM_DENSE, 1024,
                      cols_outer=True)

    x_segs = [x_prompt.reshape(mp, d), x_sample.reshape(ms, d)]
    p = norm_matmul(x_segs, norm_mix[0], wa, TM_IN_PROJ, A_PROJ_COLS // 2, BF16)
    pshift = matmul([[state_shift[0]]], [wa], None, A_PROJ_COLS, bs, 512).reshape(bs, 1, A_PROJ_COLS)
    tok_p, rwkv_p = rwkv_mix(p, 0, jnp.zeros((bp, 1, A_PROJ_COLS), F32),
                             jnp.zeros((bp, A_HEADS, A_HEAD_DIM, A_HEAD_DIM), F32),
                             mu, vecs, w2, a2, g2, bp, sp, PROMPT_CHUNK, 1, A_CHUNKS_PER_STEP)
    tok_s, rwkv_s = rwkv_mix(p, mp, pshift, state_rwkv[0], mu, vecs, w2, a2, g2, bs, ss, ss, nbs, 1)
    mo_p = mem_attention(p, 0, 0, mem_kv[0], bp, sp, 512)
    mo_s = mem_attention_cache(p, mp, 0, cache_mem_k, cache_mem_v, 0, ss, 8)
    x = out_proj([tok_p, tok_s], [mo_p, mo_s], w_out[0], x_segs)
    x_segs, conv_p0, conv_s0 = ffn(x, 0)
    p = norm_matmul(x_segs, norm_mix[1], wb, TM_IN_PROJ, wb.shape[1] // 2, BF16)
    tok_p, hgrn_p = hgrn_mix(p, 0, MEM_WIDTH, jnp.zeros((bp, B_HEADS, B_HEAD_DIM, B_HEAD_DIM), F32),
                             b_lower_bounds, b_g_norm[0], bp, sp, PROMPT_CHUNK, 1, B_CHUNKS_PER_STEP, 1)
    tok_s, hgrn_s = hgrn_mix(p, mp, MEM_WIDTH, state_hgrn[0], b_lower_bounds, b_g_norm[0], bs, ss, ss, nbs, 1, 1)
    mo_p = mem_attention(p, 0, 0, mem_kv[1], bp, sp, 512)
    mo_s = mem_attention_cache(p, mp, 0, cache_mem_k, cache_mem_v, 1, ss, 8)
    x = out_proj([tok_p, tok_s], [mo_p, mo_s], w_out[1], x_segs)
    x_segs, conv_p1, conv_s1 = ffn(x, 1)
    y_p = rmsnorm(x_segs[0], norm_final, 1024)
    y_s = rmsnorm(x_segs[1], norm_final, 1024)
    shift_p = rmsnorm(x_prompt[:, -1], norm_mix[0], bp)
    shift_s = rmsnorm(x_sample[:, -1], norm_mix[0], bs)

    return (y_p.reshape(bp, sp, d), y_s.reshape(bs, ss, d), mem_k_prompt, mem_v_prompt,
            rwkv_p[None], rwkv_s[None], shift_p[None], shift_s[None], hgrn_p[None], hgrn_s[None],
            jnp.stack([conv_p0, conv_p1]), jnp.stack([conv_s0, conv_s1]))
```

```python
import functools

import jax
import jax.numpy as jnp
from jax import lax
from jax.experimental import pallas as pl
from jax.experimental.pallas import tpu as pltpu

F32 = jnp.float32
BF16 = jnp.bfloat16

D_MODEL = 2048
DEPTH = 2
TOK_WIDTH = 1536
MEM_WIDTH = 512
MEM_HEADS = 4
MEM_HEAD_DIM = 128
N_MEM = 256
A_HEAD_DIM = 64
A_HEADS = 24
A_LORA = 96
A_GATE_RANK = 256
A_TOK_COLS = 5120
A_PROJ_COLS = MEM_WIDTH + A_TOK_COLS
B_HEADS = 12
B_HEAD_DIM = 128
D_FF = 5632
RMS_EPS = 1e-6
GN_EPS = 64e-5

LANES = 128
VMEM_LIMIT = 56 * 1024 * 1024


def _cparams(*sem):
    return pltpu.CompilerParams(dimension_semantics=sem, vmem_limit_bytes=VMEM_LIMIT)


def _dot(a, b):
    return jnp.dot(a, b, preferred_element_type=F32)


def _dot_nt(a, b):
    return lax.dot_general(a, b, (((1,), (1,)), ((), ())), preferred_element_type=F32)


def _dot_tn(a, b):
    return lax.dot_general(a, b, (((0,), (0,)), ((), ())), preferred_element_type=F32)


def _bf(x):
    return x.astype(BF16)


def _split(x):
    hi = x.astype(BF16)
    lo = (x - hi.astype(F32)).astype(BF16)
    return hi, lo


def _sigmoid(x):
    return 0.5 * (jnp.tanh(0.5 * x) + 1.0)


def _rms(x, g):
    ms = jnp.mean(x * x, axis=-1, keepdims=True)
    return x * lax.rsqrt(ms + RMS_EPS) * g


def _seg_bounds(segs, tm):
    bounds, off = [], 0
    for a in segs:
        bounds.append(off)
        off += a.shape[0] // tm
    return tuple(bounds), off


def _seg_row_specs(segs, tm, width, col_of_j, tile0=0):
    assert tile0 == 0 or len(segs) == 1
    specs, off = [], 0
    for a in segs:
        n = a.shape[0] // tm
        if col_of_j:
            imap = lambda i, j, off=off, n=n: (jnp.clip(i - off + tile0, 0, n - 1), j)
        else:
            imap = lambda i, j, off=off, n=n: (jnp.clip(i - off + tile0, 0, n - 1), 0)
        specs.append(pl.BlockSpec((tm, width), imap))
        off += n
    return specs


def _for_segment(i, bounds, fn):
    if len(bounds) == 1:
        fn(0)
        return
    for s in range(len(bounds)):
        cond = i >= bounds[s]
        if s + 1 < len(bounds):
            cond = jnp.logical_and(cond, i < bounds[s + 1])
        pl.when(cond)(functools.partial(fn, s))


def _w_spec(w, tn, col_block0=0):
    if isinstance(w, tuple):
        arr, layer = w
        return arr, pl.BlockSpec((None, arr.shape[1], tn), lambda i, j: (layer, 0, col_block0 + j))
    return w, pl.BlockSpec((w.shape[0], tn), lambda i, j: (0, col_block0 + j))


def _cast_kernel(x_ref, o_ref):
    o_ref[...] = x_ref[...].astype(o_ref.dtype)


def cast_bf16(w, tm):
    n_l, m, n = w.shape
    spec = pl.BlockSpec((1, tm, n), lambda l, i: (l, i, 0))
    return pl.pallas_call(
        _cast_kernel,
        grid=(n_l, m // tm),
        in_specs=[spec],
        out_specs=spec,
        out_shape=jax.ShapeDtypeStruct(w.shape, BF16),
        compiler_params=_cparams("parallel", "parallel"),
        name="cast_bf16",
    )(w)


def _rmsnorm_kernel(x_ref, g_ref, o_ref):
    o_ref[...] = _rms(x_ref[...], g_ref[...])


def rmsnorm(x, g, tm, row0=0, rows=None):
    d = x.shape[1]
    rows = x.shape[0] if rows is None else rows
    t0 = row0 // tm
    return pl.pallas_call(
        _rmsnorm_kernel,
        grid=(rows // tm,),
        in_specs=[pl.BlockSpec((tm, d), lambda i: (t0 + i, 0)), pl.BlockSpec((1, d), lambda i: (0, 0))],
        out_specs=pl.BlockSpec((tm, d), lambda i: (i, 0)),
        out_shape=jax.ShapeDtypeStruct((rows, d), F32),
        compiler_params=_cparams("parallel"),
        name="rmsnorm",
    )(x, g.reshape(1, d))


NORM_ROWS = 256


def _norm_to_scratch(x_ref, g, hb_ref, tm, dst_off=0):
    for r in range(0, tm, NORM_ROWS):
        n = min(NORM_ROWS, tm - r)
        hb_ref[dst_off + r:dst_off + r + n, :] = _bf(_rms(x_ref[r:r + n, :], g))


def _norm_matmul_kernel(*refs, n_seg, bounds, tm):
    x_refs = refs[:n_seg]
    g_ref, w_ref, o_ref, hb_ref = refs[n_seg:]

    @pl.when(pl.program_id(1) == 0)
    def _():
        g = g_ref[...]
        _for_segment(pl.program_id(0), bounds, lambda s: _norm_to_scratch(x_refs[s], g, hb_ref, tm))

    o_ref[...] = _dot(hb_ref[...], w_ref[...]).astype(o_ref.dtype)


def norm_matmul(x_segs, g, w, tm, tn, out_dtype=F32):
    d = x_segs[0].shape[1]
    bounds, n_tiles = _seg_bounds(x_segs, tm)
    w_arr, w_spec = _w_spec(w, tn)
    n = w_arr.shape[-1]
    return pl.pallas_call(
        functools.partial(_norm_matmul_kernel, n_seg=len(x_segs), bounds=bounds, tm=tm),
        grid=(n_tiles, n // tn),
        in_specs=_seg_row_specs(x_segs, tm, d, False) + [pl.BlockSpec((1, d), lambda i, j: (0, 0)), w_spec],
        out_specs=pl.BlockSpec((tm, tn), lambda i, j: (i, j)),
        out_shape=jax.ShapeDtypeStruct((n_tiles * tm, n), out_dtype),
        scratch_shapes=[pltpu.VMEM((tm, d), BF16)],
        compiler_params=_cparams("parallel", "arbitrary"),
        name="norm_matmul",
    )(*x_segs, g.reshape(1, d), w_arr)


def _mm_kernel(*refs, a_counts, res_count, bounds, row_axis):
    pos = 0
    a_refs = []
    for cnt in a_counts:
        a_refs.append(refs[pos:pos + cnt])
        pos += cnt
    w_refs = refs[pos:pos + len(a_counts)]
    pos += len(a_counts)
    res_refs = refs[pos:pos + res_count]
    o_ref = refs[-1]

    def compute(s):
        acc = None
        for k, segs in enumerate(a_refs):
            term = _dot(_bf(segs[min(s, len(segs) - 1)][...]), w_refs[k][...])
            acc = term if acc is None else acc + term
        if res_count:
            acc = res_refs[min(s, res_count - 1)][...] + acc
        o_ref[...] = acc

    _for_segment(pl.program_id(row_axis), bounds, compute)


def matmul(a_list, w_list, res_segs, n_out, tm, tn, res_row0=0, cols_outer=False):
    longest = max(a_list + ([res_segs] if res_segs else []), key=len)
    bounds, n_tiles = _seg_bounds(longest, tm)
    in_specs, args = [], []
    for segs in a_list:
        assert len(segs) in (1, len(bounds))
        in_specs += _seg_row_specs(segs, tm, segs[0].shape[1], False)
        args += list(segs)
    for w in w_list:
        w_arr, w_spec = _w_spec(w, tn)
        in_specs.append(w_spec)
        args.append(w_arr)
    if res_segs:
        assert len(res_segs) in (1, len(bounds))
        in_specs += _seg_row_specs(res_segs, tm, tn, True, res_row0 // tm)
        args += list(res_segs)
    out_spec = pl.BlockSpec((tm, tn), lambda i, j: (i, j))
    grid = (n_tiles, n_out // tn)
    if cols_outer:
        swap = lambda spec: pl.BlockSpec(spec.block_shape, lambda j, i, f=spec.index_map: f(i, j))
        in_specs, out_spec, grid = [swap(sp) for sp in in_specs], swap(out_spec), grid[::-1]
    return pl.pallas_call(
        functools.partial(_mm_kernel, a_counts=tuple(len(s) for s in a_list),
                          res_count=len(res_segs) if res_segs else 0, bounds=bounds,
                          row_axis=1 if cols_outer else 0),
        grid=grid,
        in_specs=in_specs,
        out_specs=out_spec,
        out_shape=jax.ShapeDtypeStruct((n_tiles * tm, n_out), F32),
        compiler_params=_cparams("parallel", "parallel"),
        name="matmul",
    )(*args)


def _attn_kernel(q_ref, k_ref, v_ref, o_ref, *, n_seq, tq, cache_layout):
    scale = MEM_HEAD_DIM ** -0.5
    pairs = [(s, h) for s in range(n_seq) for h in range(MEM_HEADS)]
    hsl = [slice(h * MEM_HEAD_DIM, (h + 1) * MEM_HEAD_DIM) for h in range(MEM_HEADS)]
    q = [_bf(q_ref[s * tq:(s + 1) * tq, hsl[h]]) for s, h in pairs]
    if cache_layout:
        k = [_bf(k_ref[0, s, pl.ds(h, N_MEM, stride=MEM_HEADS), :]) for s, h in pairs]
        v = [_bf(v_ref[0, s, pl.ds(h, N_MEM, stride=MEM_HEADS), :]) for s, h in pairs]
    else:
        k = [_bf(k_ref[s * N_MEM:(s + 1) * N_MEM, hsl[h]]) for s, h in pairs]
        v = [_bf(v_ref[s * N_MEM:(s + 1) * N_MEM, hsl[h]]) for s, h in pairs]
    sc = [_dot_nt(q[i], k[i]) * scale for i in range(len(pairs))]
    e = [jnp.exp(x - jnp.max(x, axis=-1, keepdims=True)) for x in sc]
    p = [_bf(x / jnp.sum(x, axis=-1, keepdims=True)) for x in e]
    o = [_dot(p[i], v[i]) for i in range(len(pairs))]
    rows = [jnp.concatenate(o[s * MEM_HEADS:(s + 1) * MEM_HEADS], axis=1) for s in range(n_seq)]
    o_ref[...] = _bf(jnp.concatenate(rows, axis=0) if n_seq > 1 else rows[0])


def mem_attention(p, row0, q_colblock, kv, n_batch, seq, tq):
    q_tiles = seq // tq
    t0 = row0 // tq
    return pl.pallas_call(
        functools.partial(_attn_kernel, n_seq=1, tq=tq, cache_layout=False),
        grid=(n_batch, q_tiles),
        in_specs=[pl.BlockSpec((tq, MEM_WIDTH), lambda b, t: (t0 + b * q_tiles + t, q_colblock)),
                  pl.BlockSpec((N_MEM, MEM_WIDTH), lambda b, t: (b, 0)),
                  pl.BlockSpec((N_MEM, MEM_WIDTH), lambda b, t: (b, 1))],
        out_specs=pl.BlockSpec((tq, MEM_WIDTH), lambda b, t: (b * q_tiles + t, 0)),
        out_shape=jax.ShapeDtypeStruct((n_batch * seq, MEM_WIDTH), BF16),
        compiler_params=_cparams("parallel", "parallel"),
        name="mem_attention",
    )(p, kv, kv)


def mem_attention_cache(p, row0, q_colblock, cache_k, cache_v, layer, seq, n_seq):
    depth, n_batch = cache_k.shape[:2]
    cache_k = cache_k.reshape(depth, n_batch, N_MEM * MEM_HEADS, MEM_HEAD_DIM)
    cache_v = cache_v.reshape(depth, n_batch, N_MEM * MEM_HEADS, MEM_HEAD_DIM)
    rows = n_seq * seq
    t0 = row0 // rows
    kv_spec = pl.BlockSpec((1, n_seq, N_MEM * MEM_HEADS, MEM_HEAD_DIM), lambda b: (layer, b, 0, 0))
    return pl.pallas_call(
        functools.partial(_attn_kernel, n_seq=n_seq, tq=seq, cache_layout=True),
        grid=(n_batch // n_seq,),
        in_specs=[pl.BlockSpec((rows, MEM_WIDTH), lambda b: (t0 + b, q_colblock)), kv_spec, kv_spec],
        out_specs=pl.BlockSpec((rows, MEM_WIDTH), lambda b: (b, 0)),
        out_shape=jax.ShapeDtypeStruct((n_batch * seq, MEM_WIDTH), BF16),
        compiler_params=_cparams("parallel"),
        name="mem_attention_cache",
    )(p, cache_k, cache_v)


FFN_HALO = 16


def _gelu_gate(c, v):
    return _bf(jax.nn.gelu(c) * v)


def _ffn_up_prompt_kernel(x_ref, xh_ref, g_ref, wa_ref, wv_ref, cw_ref, cb_ref, u_ref, cs_ref, hb_ref,
                          *, tm, tiles_per_seq):
    i = pl.program_id(0)

    @pl.when(pl.program_id(1) == 0)
    def _():
        g = g_ref[...]
        hb_ref[0:FFN_HALO, :] = _bf(_rms(xh_ref[...], g))
        _norm_to_scratch(x_ref, g, hb_ref, tm, dst_off=FFN_HALO)

    a_ext = _dot(hb_ref[...], wa_ref[...])
    v = _dot(hb_ref[FFN_HALO:, :], wv_ref[...])
    rows = lax.broadcasted_iota(jnp.int32, (FFN_HALO + tm, 1), 0)
    n_zero = jnp.where((i % tiles_per_seq) == 0, FFN_HALO, 0)
    a_ext = jnp.where(rows < n_zero, 0.0, a_ext)
    a0 = a_ext[FFN_HALO:]
    a1 = pltpu.roll(a_ext, 1, 0)[FFN_HALO:]
    a2 = pltpu.roll(a_ext, 2, 0)[FFN_HALO:]
    c = cb_ref[...] + a2 * cw_ref[0:1, :] + a1 * cw_ref[1:2, :] + a0 * cw_ref[2:3, :]
    u_ref[...] = _gelu_gate(c, v)
    cs_ref[0] = a0[tm - 8:tm][6:8]


def ffn_up_prompt(x, g, w_up, layer, cw, cb, n_batch, seq, tm, tn):
    d = x.shape[1]
    m = n_batch * seq
    nf = D_FF // tn
    tiles_per_seq = seq // tm
    halo_blocks = tm // FFN_HALO
    w_arr, wa_spec = _w_spec((w_up, layer), tn)
    _, wv_spec = _w_spec((w_up, layer), tn, nf)
    u, cs = pl.pallas_call(
        functools.partial(_ffn_up_prompt_kernel, tm=tm, tiles_per_seq=tiles_per_seq),
        grid=(m // tm, nf),
        in_specs=[pl.BlockSpec((tm, d), lambda i, j: (i, 0)),
                  pl.BlockSpec((FFN_HALO, d), lambda i, j: (jnp.maximum(i * halo_blocks - 1, 0), 0)),
                  pl.BlockSpec((1, d), lambda i, j: (0, 0)),
                  wa_spec, wv_spec,
                  pl.BlockSpec((3, tn), lambda i, j: (0, j)),
                  pl.BlockSpec((1, tn), lambda i, j: (0, j))],
        out_specs=[pl.BlockSpec((tm, tn), lambda i, j: (i, j)),
                   pl.BlockSpec((1, 2, tn), lambda i, j: (i, 0, j))],
        out_shape=[jax.ShapeDtypeStruct((m, D_FF), BF16),
                   jax.ShapeDtypeStruct((m // tm, 2, D_FF), F32)],
        scratch_shapes=[pltpu.VMEM((FFN_HALO + tm, d), BF16)],
        compiler_params=_cparams("parallel", "arbitrary"),
        name="ffn_up_prompt",
    )(x, x, g.reshape(1, d), w_arr, w_arr, cw, cb.reshape(1, D_FF))
    return u, cs.reshape(n_batch, tiles_per_seq, 2, D_FF)[:, -1]


def _ffn_up_sample_kernel(x_ref, g_ref, wa_ref, wv_ref, cw_ref, cb_ref, h0_ref, h1_ref,
                          u_ref, c6_ref, c7_ref, hb_ref, a_ref, a1_ref, a2_ref, *, tm, seq):
    @pl.when(pl.program_id(1) == 0)
    def _():
        _norm_to_scratch(x_ref, g_ref[...], hb_ref, tm)

    a = _dot(hb_ref[...], wa_ref[...])
    v = _dot(hb_ref[...], wv_ref[...])
    r1 = pltpu.roll(a, 1, 0)
    r2 = pltpu.roll(a, 2, 0)
    nseq = tm // seq
    firsts = pl.ds(0, nseq, stride=seq)
    seconds = pl.ds(1, nseq, stride=seq)
    for q in range(a.shape[1] // LANES):
        sl = slice(q * LANES, (q + 1) * LANES)
        a1_ref[q] = r1[:, sl]
        a1_ref[q, firsts, :] = h1_ref[:, sl]
        a2_ref[q] = r2[:, sl]
        a2_ref[q, firsts, :] = h0_ref[:, sl]
        a2_ref[q, seconds, :] = h1_ref[:, sl]
        a_ref[q] = a[:, sl]
        c6_ref[:, sl] = a_ref[q, pl.ds(seq - 2, nseq, stride=seq), :]
        c7_ref[:, sl] = a_ref[q, pl.ds(seq - 1, nseq, stride=seq), :]
    n_q = a.shape[1] // LANES
    a1 = jnp.concatenate([a1_ref[q] for q in range(n_q)], axis=1)
    a2 = jnp.concatenate([a2_ref[q] for q in range(n_q)], axis=1)
    c = cb_ref[...] + a2 * cw_ref[0:1, :] + a1 * cw_ref[1:2, :] + a * cw_ref[2:3, :]
    u_ref[...] = _gelu_gate(c, v)


def ffn_up_sample(x, row0, g, w_up, layer, cw, cb, hist, seq, tm, tn):
    d = x.shape[1]
    m = hist.shape[0] * seq
    nf = D_FF // tn
    nb = tm // seq
    t0 = row0 // tm
    w_arr, wa_spec = _w_spec((w_up, layer), tn)
    _, wv_spec = _w_spec((w_up, layer), tn, nf)
    rows_scratch = pltpu.VMEM((tn // LANES, tm, LANES), F32)
    return pl.pallas_call(
        functools.partial(_ffn_up_sample_kernel, tm=tm, seq=seq),
        grid=(m // tm, nf),
        in_specs=[pl.BlockSpec((tm, d), lambda i, j: (t0 + i, 0)),
                  pl.BlockSpec((1, d), lambda i, j: (0, 0)),
                  wa_spec, wv_spec,
                  pl.BlockSpec((3, tn), lambda i, j: (0, j)),
                  pl.BlockSpec((1, tn), lambda i, j: (0, j)),
                  pl.BlockSpec((nb, tn), lambda i, j: (i, j)),
                  pl.BlockSpec((nb, tn), lambda i, j: (i, nf + j))],
        out_specs=[pl.BlockSpec((tm, tn), lambda i, j: (i, j)),
                   pl.BlockSpec((nb, tn), lambda i, j: (i, j)),
                   pl.BlockSpec((nb, tn), lambda i, j: (i, j))],
        out_shape=[jax.ShapeDtypeStruct((m, D_FF), BF16),
                   jax.ShapeDtypeStruct((m // seq, D_FF), F32),
                   jax.ShapeDtypeStruct((m // seq, D_FF), F32)],
        scratch_shapes=[pltpu.VMEM((tm, d), BF16), rows_scratch, rows_scratch, rows_scratch],
        compiler_params=_cparams("parallel", "arbitrary"),
        name="ffn_up_sample",
    )(x, g.reshape(1, d), w_arr, w_arr, cw, cb.reshape(1, D_FF), hist, hist)


def _seg_sum(x, e, two_pass):
    parts = []
    for j in range(x.shape[1] // LANES):
        xj = x[:, j * LANES:(j + 1) * LANES]
        if two_pass:
            hi, lo = _split(xj)
            parts.append(_dot(hi, e) + _dot(lo, e))
        else:
            parts.append(_dot(_bf(xj), e))
    return jnp.concatenate(parts, axis=1)


def _group_rows(x, idx, nb, c):
    return [x[b * c + idx:b * c + idx + 1, :] for b in range(nb)]


def _bcast_groups(rows, c):
    parts = [jnp.broadcast_to(r, (c, r.shape[1])) for r in rows]
    return parts[0] if len(parts) == 1 else jnp.concatenate(parts, axis=0)


def _unit_lower_solve(a_list, rhs_list, c):
    mm = lambda x, y: _dot(_bf(x), _bf(y))
    n = range(len(a_list))
    rows = lax.broadcasted_iota(jnp.int32, (c, c), 0)
    cols = lax.broadcasted_iota(jnp.int32, (c, c), 1)
    eye = (rows == cols).astype(F32)
    blk = min(c, 16)
    if c > blk:
        assert c // blk <= 4
        same = (rows // blk) == (cols // blk)
        ad = [jnp.where(same, a, 0.0) for a in a_list]
        ao = [a_list[i] - ad[i] for i in n]
    else:
        ad = a_list
    t = [eye - ad[i] for i in n]
    pw = ad
    span = 2
    while span < blk:
        pw = [mm(pw[i], pw[i]) for i in n]
        t = [t[i] + mm(t[i], pw[i]) for i in n]
        span *= 2
    x = [mm(t[i], rhs_list[i]) for i in n]
    if c > blk:
        nn = [mm(t[i], ao[i]) for i in n]
        n2 = [mm(nn[i], nn[i]) for i in n]
        x = [x[i] + mm(n2[i], x[i]) for i in n]
        x = [x[i] - mm(nn[i], x[i]) for i in n]
    return x


def _rwkv_chains(n, ch, c, s_old, xk, xr, kb, bb, v, kh, bh, gam, strict, incl):
    rs = {i: slice(ch[i][0] * c, (ch[i][0] + 1) * c) for i in n}
    ls = {i: slice(ch[i][1] * A_HEAD_DIM, (ch[i][1] + 1) * A_HEAD_DIM) for i in n}
    sb = {i: _bf(s_old[i]) for i in n}
    xk_h = {i: _bf(xk[rs[i], ls[i]]) for i in n}
    xr_h = {i: _bf(xr[rs[i], ls[i]]) for i in n}
    kb_h = {i: _bf(kb[rs[i], ls[i]]) for i in n}
    bb_h = {i: _bf(bb[rs[i], ls[i]]) for i in n}
    v_h = {i: _bf(v[rs[i], ls[i]]) for i in n}
    a_kk = {i: jnp.where(strict, _dot_nt(xk_h[i], kb_h[i]), 0.0) for i in n}
    a_kb = {i: jnp.where(strict, _dot_nt(xk_h[i], bb_h[i]), 0.0) for i in n}
    a_rk = {i: jnp.where(incl, _dot_nt(xr_h[i], kb_h[i]), 0.0) for i in n}
    a_rb = {i: jnp.where(incl, _dot_nt(xr_h[i], bb_h[i]), 0.0) for i in n}
    rhs = {i: -(_dot_nt(xk_h[i], sb[i]) + _dot(_bf(a_kk[i]), v_h[i])) for i in n}
    u = dict(zip(n, _unit_lower_solve([a_kb[i] for i in n], [rhs[i] for i in n], c)))
    u_h = {i: _bf(u[i]) for i in n}
    y_h = {i: _dot_nt(xr_h[i], sb[i]) + _dot(_bf(a_rk[i]), v_h[i]) + _dot(_bf(a_rb[i]), u_h[i]) for i in n}
    if c % 16 == 0:
        vu = {i: jnp.concatenate([v_h[i], u_h[i]], axis=0) for i in n}
        kbh = {i: jnp.concatenate([_bf(kh[rs[i], ls[i]]), _bf(bh[rs[i], ls[i]])], axis=0) for i in n}
    else:
        vu = {i: _bf(jnp.concatenate([v[rs[i], ls[i]], u[i]], axis=0)) for i in n}
        kbh = {i: _bf(jnp.concatenate([kh[rs[i], ls[i]], bh[rs[i], ls[i]]], axis=0)) for i in n}
    s_new = {i: s_old[i] * gam[ch[i][0] * c:ch[i][0] * c + 1, ls[i]] + _dot_tn(vu[i], kbh[i]) for i in n}
    return y_h, s_new


def _rwkv_kernel(p_ref, ps_ref, s0_ref, mu_ref, vec_ref, w2_ref, a2_ref, g2_ref, e_ref, lt_ref,
                 tok_ref, sout_ref, prev_ref, s_ref, y_ref, *, c, nb, ns):
    ci = pl.program_id(1)
    tw = TOK_WIDTH
    lora0 = 3 * tw
    groups = nb * ns

    @pl.when(ci == 0)
    def _():
        for b in range(nb):
            s_ref[b * A_HEADS:(b + 1) * A_HEADS] = s0_ref[b]
        prev_ref[...] = ps_ref[:, :, MEM_WIDTH:]

    p = p_ref[:, MEM_WIDTH:].astype(F32)
    rows = lax.broadcasted_iota(jnp.int32, (groups * c, 1), 0)
    p_prev = pltpu.roll(p, 1, 0)
    for b in range(nb):
        p_prev = jnp.where(rows == b * ns * c, prev_ref[b], p_prev)
        prev_ref[b] = p[(b + 1) * ns * c - 1:(b + 1) * ns * c, :]
    xm = p + mu_ref[...] * (p_prev - p)

    r = xm[:, 0:tw]
    k = xm[:, tw:2 * tw]
    v = xm[:, 2 * tw:3 * tw]
    xw = xm[:, lora0:lora0 + 128]
    xa = xm[:, lora0:lora0 + 256]
    xg = xm[:, lora0 + 128:lora0 + 512]
    w0, a0, k_k, k_a, r_k, ln_w, ln_b = (vec_ref[i:i + 1, :] for i in range(7))

    ell = -jnp.exp(-0.5) * _sigmoid(w0 + _dot(_bf(jnp.tanh(xw)), w2_ref[...]))
    a = _sigmoid(a0 + _dot(_bf(xa), a2_ref[...]))
    gate = _dot(_bf(_sigmoid(xg)), g2_ref[...])
    kkraw = k * k_k
    k2 = k * (1.0 + (a - 1.0) * k_a)
    e = e_ref[...]
    kap = kkraw * lax.rsqrt(jnp.maximum(_seg_sum(kkraw * kkraw, e, True), 1e-24))
    bet = kap * a

    ell_hi, ell_lo = _split(ell)
    gc = _dot(lt_ref[...], ell_hi) + _dot(lt_ref[...], ell_lo)
    gam = _bcast_groups([jnp.exp(x) for x in _group_rows(gc, c - 1, groups, c)], c)
    egi = jnp.exp(-gc)
    el = egi * gam
    xk = kap * jnp.exp(gc - ell)
    xr = r * jnp.exp(gc)
    kb = k2 * egi
    bb = bet * egi
    kh = k2 * el
    bh = bet * el
    ti = lax.broadcasted_iota(jnp.int32, (c, c), 0)
    si = lax.broadcasted_iota(jnp.int32, (c, c), 1)
    strict = si < ti
    incl = si <= ti

    n_chains = nb * A_HEADS
    state = {i: s_ref[i] for i in range(n_chains)}
    y_h = {}
    if c % 16 == 0:
        xk, xr, kb, bb, v_op, kh, bh = (_bf(t) for t in (xk, xr, kb, bb, v, kh, bh))
    else:
        v_op = v
    for s in range(ns):
        ch = [(b * ns + s, h) for b in range(nb) for h in range(A_HEADS)]
        ys, state = _rwkv_chains(range(n_chains), ch, c, state, xk, xr, kb, bb, v_op, kh, bh, gam, strict, incl)
        for i in range(n_chains):
            y_h[ch[i]] = ys[i]
    for (grp, h), val in y_h.items():
        y_ref[grp * c:(grp + 1) * c, h * A_HEAD_DIM:(h + 1) * A_HEAD_DIM] = val
    for i in range(n_chains):
        s_ref[i] = state[i]

    y = y_ref[...]
    inv_n = 1.0 / A_HEAD_DIM
    mean = _seg_sum(y, e, False) * inv_n
    d = y - mean
    var = _seg_sum(d * d, e, False) * inv_n
    yn = d * lax.rsqrt(var + GN_EPS) * ln_w + ln_b
    bonus = _seg_sum(r * k2 * r_k, e, False) * v
    tok_ref[...] = ((yn + bonus) * gate).astype(tok_ref.dtype)

    @pl.when(ci == pl.num_programs(1) - 1)
    def _():
        for b in range(nb):
            sout_ref[b] = s_ref[b * A_HEADS:(b + 1) * A_HEADS]


def _block_ltri(nb, c):
    t = jnp.arange(nb * c)
    return ((t[None, :] <= t[:, None]) & (t[None, :] // c == t[:, None] // c)).astype(BF16)


def rwkv_mix(p, row0, pshift, s0, mu, vecs, w2, a2, g2, n_batch, seq, c, nb, ns):
    assert (nb == 1 and seq % (ns * c) == 0) or (ns == 1 and seq == c and c == 8)
    n_chunks = seq // (ns * c)
    rows = nb * ns * c
    t0 = row0 // rows
    lane = jnp.arange(LANES)
    e = (lane[:, None] // A_HEAD_DIM == lane[None, :] // A_HEAD_DIM).astype(BF16)
    const = lambda shape: pl.BlockSpec(shape, lambda b, ci: (0,) * len(shape))
    st_spec = pl.BlockSpec((nb, A_HEADS, A_HEAD_DIM, A_HEAD_DIM), lambda b, ci: (b, 0, 0, 0))
    return pl.pallas_call(
        functools.partial(_rwkv_kernel, c=c, nb=nb, ns=ns),
        grid=(n_batch // nb, n_chunks),
        in_specs=[pl.BlockSpec((rows, A_PROJ_COLS), lambda b, ci: (t0 + b * n_chunks + ci, 0)),
                  pl.BlockSpec((nb, 1, A_PROJ_COLS), lambda b, ci: (b, 0, 0)),
                  st_spec,
                  const((1, A_TOK_COLS)), const((8, TOK_WIDTH)),
                  const(w2.shape), const(a2.shape), const(g2.shape), const((LANES, LANES)), const((rows, rows))],
        out_specs=[pl.BlockSpec((rows, TOK_WIDTH), lambda b, ci: (b * n_chunks + ci, 0)), st_spec],
        out_shape=[jax.ShapeDtypeStruct((n_batch * seq, TOK_WIDTH), BF16),
                   jax.ShapeDtypeStruct((n_batch, A_HEADS, A_HEAD_DIM, A_HEAD_DIM), F32)],
        scratch_shapes=[pltpu.VMEM((nb, 1, A_TOK_COLS), F32),
                        pltpu.VMEM((nb * A_HEADS, A_HEAD_DIM, A_HEAD_DIM), F32),
                        pltpu.VMEM((rows, TOK_WIDTH), F32)],
        compiler_params=_cparams("parallel", "arbitrary"),
        name="rwkv_mix",
    )(p, pshift, s0, mu, vecs, w2, a2, g2, e, _block_ltri(nb * ns, c))


def _hgrn_kernel(p_ref, s0_ref, lbp_ref, gn_ref, lt_ref, tok_ref, sout_ref, st_ref, *, c, nb, ns, layer, col0,
                 single_chunk):
    ci = pl.program_id(1)
    tw = TOK_WIDTH
    hd = B_HEAD_DIM
    ch = [(b, h) for b in range(nb) for h in range(B_HEADS)]
    n = range(len(ch))

    if not single_chunk:
        @pl.when(ci == 0)
        def _():
            for i in n:
                st_ref[i] = s0_ref[ch[i][0], ch[i][1]].T

    lbp = lbp_ref[...]
    mx = jnp.max(lbp, axis=0, keepdims=True)
    ex = jnp.exp(lbp - mx)
    den = jnp.sum(ex, axis=0, keepdims=True)
    lb = jnp.zeros((1, tw), F32)
    for i in range(1, layer + 1):
        lb = lb + ex[i:i + 1, :] / den

    q = p_ref[:, col0:col0 + tw].astype(F32)
    f = p_ref[:, col0 + tw:col0 + 2 * tw].astype(F32)
    iv = p_ref[:, col0 + 2 * tw:col0 + 3 * tw].astype(F32)
    og = p_ref[:, col0 + 3 * tw:col0 + 4 * tw].astype(F32)
    fg = lb + (1.0 - lb) * _sigmoid(f)
    lf = jnp.log(fg)
    kk = 1.0 - fg
    qq = q * _sigmoid(q)
    lf_hi, lf_lo = _split(lf)
    bc = _dot(lt_ref[...], lf_hi) + _dot(lt_ref[...], lf_lo)
    last_r = _group_rows(bc, c - 1, nb * ns, c)
    mid_r = _group_rows(bc, (c - 1) // 2, nb * ns, c)
    mid = _bcast_groups(mid_r, c)
    qe = qq * jnp.exp(bc - mid)
    ke = kk * jnp.exp(mid - bc)
    qs = qe * _bcast_groups([jnp.exp(m) for m in mid_r], c)
    kl = ke * _bcast_groups([jnp.exp(l - m) for l, m in zip(last_r, mid_r)], c)
    gam = _bcast_groups([jnp.exp(l) for l in last_r], c)
    gate = og * _sigmoid(og)
    ti = lax.broadcasted_iota(jnp.int32, (c, c), 0)
    si = lax.broadcasted_iota(jnp.int32, (c, c), 1)
    incl = si <= ti
    gn = gn_ref[...]
    ls = [slice(h * hd, (h + 1) * hd) for _, h in ch]
    st = [s0_ref[ch[i][0], ch[i][1]] if single_chunk else st_ref[i] for i in n]
    out = []
    for s in range(ns):
        grp = [ch[i][0] * ns + s for i in n]
        rs = [slice(g * c, (g + 1) * c) for g in grp]
        v_h = [_bf(iv[rs[i], ls[i]]) for i in n]
        att = [jnp.where(incl, _dot_nt(_bf(qe[rs[i], ls[i]]), _bf(ke[rs[i], ls[i]])), 0.0) for i in n]
        if single_chunk:
            o = [_dot(_bf(att[i]), v_h[i]) + _dot(_bf(qs[rs[i], ls[i]]), _bf(st[i])) for i in n]
            ones = jnp.ones((c, hd), BF16)
            lf_parts = [_split(lf[rs[i], ls[i]]) for i in n]
            gam_col = [jnp.exp(_dot_tn(hi, ones) + _dot_tn(lo, ones)) for hi, lo in lf_parts]
            st = [gam_col[i] * st[i] + _dot_tn(_bf(kl[rs[i], ls[i]]), v_h[i]) for i in n]
        else:
            o = [_dot(_bf(att[i]), v_h[i]) + _dot_nt(_bf(qs[rs[i], ls[i]]), _bf(st[i])) for i in n]
            st = [st[i] * gam[grp[i] * c:grp[i] * c + 1, ls[i]] + _dot_tn(v_h[i], _bf(kl[rs[i], ls[i]]))
                  for i in n]
        on = [x * lax.rsqrt(jnp.mean(x * x, axis=-1, keepdims=True) + RMS_EPS) * gn for x in o]
        out += [(rs[i], ls[i], on[i]) for i in n]
    for r_sl, l_sl, val in out:
        tok_ref[r_sl, l_sl] = (val * gate[r_sl, l_sl]).astype(tok_ref.dtype)
    for i in n:
        if single_chunk:
            sout_ref[ch[i][0], ch[i][1]] = st[i]
        else:
            st_ref[i] = st[i]

    if not single_chunk:
        @pl.when(ci == pl.num_programs(1) - 1)
        def _():
            for i in n:
                sout_ref[ch[i][0], ch[i][1]] = st_ref[i].T


def hgrn_mix(p, row0, col0, s0, lbp, gn, n_batch, seq, c, nb, ns, layer):
    assert (nb == 1 and seq % (ns * c) == 0) or (ns == 1 and seq == c and c == 8)
    cols = p.shape[1]
    n_chunks = seq // (ns * c)
    rows = nb * ns * c
    t0 = row0 // rows
    const = lambda shape: pl.BlockSpec(shape, lambda b, ci: (0,) * len(shape))
    st_spec = pl.BlockSpec((nb, B_HEADS, B_HEAD_DIM, B_HEAD_DIM), lambda b, ci: (b, 0, 0, 0))
    return pl.pallas_call(
        functools.partial(_hgrn_kernel, c=c, nb=nb, ns=ns, layer=layer, col0=col0, single_chunk=seq == c),
        grid=(n_batch // nb, n_chunks),
        in_specs=[pl.BlockSpec((rows, cols), lambda b, ci: (t0 + b * n_chunks + ci, 0)),
                  st_spec, const((DEPTH, TOK_WIDTH)), const((1, B_HEAD_DIM)), const((rows, rows))],
        out_specs=[pl.BlockSpec((rows, TOK_WIDTH), lambda b, ci: (b * n_chunks + ci, 0)), st_spec],
        out_shape=[jax.ShapeDtypeStruct((n_batch * seq, TOK_WIDTH), BF16),
                   jax.ShapeDtypeStruct((n_batch, B_HEADS, B_HEAD_DIM, B_HEAD_DIM), F32)],
        scratch_shapes=[pltpu.VMEM((nb * B_HEADS, B_HEAD_DIM, B_HEAD_DIM), F32)],
        compiler_params=_cparams("parallel", "arbitrary"),
        name="hgrn_mix",
    )(p, s0, lbp, gn.reshape(1, B_HEAD_DIM), _block_ltri(nb * ns, c))


PROMPT_CHUNK = 64
A_CHUNKS_PER_STEP = 2
B_CHUNKS_PER_STEP = 4
SAMPLE_SEQS_PER_STEP = 8
TM_DENSE = 1024
TM_IN_PROJ = 512


def kernel(x_prompt, x_sample, mem_prompt, cache_mem_k, cache_mem_v, state_rwkv, state_shift, state_hgrn, state_conv, norm_mix, norm_ffn, norm_final, mem_norm, w_mem_kv, a_w_in, a_mu, a_w0, a_w2, a_a0, a_a2, a_g2, a_k_k, a_k_a, a_r_k, a_ln_w, a_ln_b, a_w_out, b_w_in, b_lower_bounds, b_g_norm, b_w_out, ffn_w_up, ffn_conv_w, ffn_conv_b, ffn_w_down):
    bp, sp, d = x_prompt.shape
    bs, ss, _ = x_sample.shape
    mp, ms = bp * sp, bs * ss
    nbs = SAMPLE_SEQS_PER_STEP

    pad_cols = A_PROJ_COLS - a_w_in.shape[2]
    wa = _bf(jnp.pad(a_w_in[0], ((0, 0), (0, pad_cols))))
    mu = jnp.pad(a_mu[0], (0, pad_cols)).reshape(1, A_TOK_COLS)
    vecs = jnp.stack([a_w0[0], a_a0[0], a_k_k[0], a_k_a[0], a_r_k[0], a_ln_w[0], a_ln_b[0],
                      jnp.zeros((TOK_WIDTH,), F32)])
    w2 = _bf(jnp.pad(a_w2[0], ((0, 128 - A_LORA), (0, 0))))
    a2 = _bf(jnp.pad(a_a2[0], ((A_LORA, 256 - 2 * A_LORA), (0, 0))))
    g2 = _bf(jnp.pad(a_g2[0], ((2 * A_LORA - 128, 384 - (2 * A_LORA - 128) - A_GATE_RANK), (0, 0))))
    wb = cast_bf16(b_w_in, 512)[0]
    w_out = [cast_bf16(a_w_out, 1024)[0], cast_bf16(b_w_out, 1024)[0]]
    w_up = cast_bf16(ffn_w_up, 256)
    w_down = cast_bf16(ffn_w_down, 1408)

    mem_rows = mem_prompt.reshape(bp * N_MEM, d)
    mem_kv = [norm_matmul([mem_rows], mem_norm[l], _bf(w_mem_kv[l]), bp * N_MEM, 512) for l in range(DEPTH)]
    mem_k_prompt = jnp.stack([kv[:, :MEM_WIDTH] for kv in mem_kv]).reshape(DEPTH, bp, N_MEM, MEM_HEADS, MEM_HEAD_DIM)
    mem_v_prompt = jnp.stack([kv[:, MEM_WIDTH:] for kv in mem_kv]).reshape(DEPTH, bp, N_MEM, MEM_HEADS, MEM_HEAD_DIM)

    hist = [state_conv[l].reshape(bs, 2 * D_FF) for l in range(DEPTH)]

    def ffn(x, layer):
        u_p, cs_p = ffn_up_prompt(x, norm_ffn[layer], w_up, layer, ffn_conv_w[layer], ffn_conv_b[layer],
                                  bp, sp, 1024, 512)
        u_s, c6, c7 = ffn_up_sample(x, mp, norm_ffn[layer], w_up, layer, ffn_conv_w[layer], ffn_conv_b[layer],
                                    hist[layer], ss, 1024, 512)
        x_p = matmul([[u_p]], [(w_down, layer)], [x], D_MODEL, TM_DENSE, 512)
        x_s = matmul([[u_s]], [(w_down, layer)], [x], D_MODEL, TM_DENSE, 512, res_row0=mp)
        return [x_p, x_s], cs_p, jnp.stack([c6, c7], axis=1)

    def out_proj(tok, mem_o, w, res_segs):
        return matmul([tok, mem_o], [w[:TOK_WIDTH], w[TOK_WIDTH:]], res_segs, D_MODEL, TM_DENSE, 1024,
                      cols_outer=True)

    x_segs = [x_prompt.reshape(mp, d), x_sample.reshape(ms, d)]
    p = norm_matmul(x_segs, norm_mix[0], wa, TM_IN_PROJ, A_PROJ_COLS // 2, BF16)
    pshift = matmul([[state_shift[0]]], [wa], None, A_PROJ_COLS, bs, 512).reshape(bs, 1, A_PROJ_COLS)
    tok_p, rwkv_p = rwkv_mix(p, 0, jnp.zeros((bp, 1, A_PROJ_COLS), F32),
                             jnp.zeros((bp, A_HEADS, A_HEAD_DIM, A_HEAD_DIM), F32),
                             mu, vecs, w2, a2, g2, bp, sp, PROMPT_CHUNK, 1, A_CHUNKS_PER_STEP)
    tok_s, rwkv_s = rwkv_mix(p, mp, pshift, state_rwkv[0], mu, vecs, w2, a2, g2, bs, ss, ss, nbs, 1)
    mo_p = mem_attention(p, 0, 0, mem_kv[0], bp, sp, 512)
    mo_s = mem_attention_cache(p, mp, 0, cache_mem_k, cache_mem_v, 0, ss, 8)
    x = out_proj([tok_p, tok_s], [mo_p, mo_s], w_out[0], x_segs)
    x_segs, conv_p0, conv_s0 = ffn(x, 0)
    p = norm_matmul(x_segs, norm_mix[1], wb, TM_IN_PROJ, wb.shape[1] // 2, BF16)
    tok_p, hgrn_p = hgrn_mix(p, 0, MEM_WIDTH, jnp.zeros((bp, B_HEADS, B_HEAD_DIM, B_HEAD_DIM), F32),
                             b_lower_bounds, b_g_norm[0], bp, sp, PROMPT_CHUNK, 1, B_CHUNKS_PER_STEP, 1)
    tok_s, hgrn_s = hgrn_mix(p, mp, MEM_WIDTH, state_hgrn[0], b_lower_bounds, b_g_norm[0], bs, ss, ss, nbs, 1, 1)
    mo_p = mem_attention(p, 0, 0, mem_kv[1], bp, sp, 512)
    mo_s = mem_attention_cache(p, mp, 0, cache_mem_k, cache_mem_v, 1, ss, 8)
    x = out_proj([tok_p, tok_s], [mo_p, mo_s], w_out[1], x_segs)
    x_segs, conv_p1, conv_s1 = ffn(x, 1)
    y_p = rmsnorm(x_segs[0], norm_final, 1024)
    y_s = rmsnorm(x_segs[1], norm_final, 1024)
    shift_p = rmsnorm(x_prompt[:, -1], norm_mix[0], bp)
    shift_s = rmsnorm(x_sample[:, -1], norm_mix[0], bs)

    return (y_p.reshape(bp, sp, d), y_s.reshape(bs, ss, d), mem_k_prompt, mem_v_prompt,
            rwkv_p[None], rwkv_s[None], shift_p[None], shift_s[None], hgrn_p[None], hgrn_s[None],
            jnp.stack([conv_p0, conv_p1]), jnp.stack([conv_s0, conv_s1]))
```
